```python
import numpy as np
import jax
import jax.numpy as jnp
from jax import lax

D_MODEL = 2048
BATCH = 2
SEQ = 8192
DEPTH = 2

CTX_LEN = 256
GRID_W = 64
EPS = 1e-6

N_BRANCHES = 4
BRANCH_WIDTH = D_MODEL // 4

A_HEAD_DIM = 64
A_HEADS = BRANCH_WIDTH // A_HEAD_DIM
A_KV_HEADS = A_HEADS // 4
A_GROUP = A_HEADS // A_KV_HEADS
A_WINDOW = 128
A_BLOCK = 128
ROPE_BASE = 10000.0

P_GROUPS = 4
P_GROUP_W = BRANCH_WIDTH // P_GROUPS
P_WINDOWS = (2, 4, 8, 16)

C_GROUPS = 4
C_GROUP_W = BRANCH_WIDTH // C_GROUPS
C_CHUNK = 128

G_HEADS = 4
G_KEY_DIM = BRANCH_WIDTH // 2 // G_HEADS
G_VAL_DIM = BRANCH_WIDTH // G_HEADS
G_GATE_RANK = 16
G_GATE_TAU = 16.0
G_CHUNK = 64

N_EXPERTS = 16
N_EXPERT_GROUPS = 4
EXPERTS_PER_GROUP = N_EXPERTS // N_EXPERT_GROUPS
GROUP_SCORE_TOP = 2
TOP_K = 2
D_EXPERT = D_MODEL // 2

IN_SPLITS = (A_HEADS * A_HEAD_DIM, A_KV_HEADS * A_HEAD_DIM, A_KV_HEADS * A_HEAD_DIM,
             BRANCH_WIDTH, 2 * BRANCH_WIDTH,
             G_HEADS * G_KEY_DIM, G_HEADS * G_KEY_DIM, G_HEADS * G_VAL_DIM, BRANCH_WIDTH, 2 * G_GATE_RANK)
D_IN = sum(IN_SPLITS)

kernel_name = 'hybrid_flow_backbone_block'

f32 = jnp.float32


def rms_norm(x, g):
    xf = x.astype(f32)
    y = xf * lax.rsqrt(jnp.mean(xf * xf, axis=-1, keepdims=True) + EPS)
    return (y * g.astype(f32)).astype(x.dtype)


def layer_norm(x, g, b):
    xf = x.astype(f32)
    mu = jnp.mean(xf, axis=-1, keepdims=True)
    var = jnp.mean(jnp.square(xf - mu), axis=-1, keepdims=True)
    return ((xf - mu) * lax.rsqrt(var + EPS) * g.astype(f32) + b.astype(f32)).astype(x.dtype)


def modulate(h, shift, scale):
    return h * (1 + scale) + shift


def axial_rope_tables(n):
    rows = n // GRID_W
    row = jnp.repeat(jnp.arange(rows), GRID_W)
    col = jnp.tile(jnp.arange(GRID_W), rows)
    nf = A_HEAD_DIM // 4
    inv_freq = ROPE_BASE ** (-jnp.arange(nf, dtype=f32) / nf)
    ang = jnp.concatenate([row[:, None].astype(f32) * inv_freq,
                           col[:, None].astype(f32) * inv_freq], axis=-1)
    return jnp.cos(ang), jnp.sin(ang)


def apply_axial_rope(x, cos, sin):
    nf = A_HEAD_DIM // 4
    xf = x.astype(f32)
    parts = []
    for a in range(2):
        p = xf[..., a * 2 * nf:(a + 1) * 2 * nf]
        cs, sn = cos[:, a * nf:(a + 1) * nf], sin[:, a * nf:(a + 1) * nf]
        p1, p2 = p[..., :nf], p[..., nf:]
        parts += [p1 * cs - p2 * sn, p2 * cs + p1 * sn]
    return jnp.concatenate(parts, axis=-1).astype(x.dtype)


def q_heads(t):
    b, n, _ = t.shape
    return t.reshape(b, n, A_KV_HEADS, A_GROUP, A_HEAD_DIM).transpose(0, 2, 3, 1, 4)


def kv_heads(t):
    b, n, _ = t.shape
    return t.reshape(b, n, A_KV_HEADS, A_HEAD_DIM).transpose(0, 2, 1, 3)


def merge_attn_heads(o):
    b, kv, g, n, d = o.shape
    return o.transpose(0, 3, 1, 2, 4).reshape(b, n, kv * g * d)


def sink_softmax(scores, sink):
    m = sink
    for s in scores:
        m = jnp.maximum(m, jnp.max(s, axis=-1, keepdims=True))
    ps = [jnp.exp(s - m) for s in scores]
    denom = jnp.exp(sink - m)
    for p in ps:
        denom = denom + jnp.sum(p, axis=-1, keepdims=True)
    return [p / denom for p in ps]


def window_attention(q, k, v, kc, vc, sink):
    b, kvh, g, n, hd = q.shape
    nb = n // A_BLOCK
    scale = hd ** -0.5
    qb = q.reshape(b, kvh, g, nb, A_BLOCK, hd)

    def band(t):
        tb = jnp.pad(t.reshape(b, kvh, nb, A_BLOCK, hd), ((0, 0), (0, 0), (1, 1), (0, 0), (0, 0)))
        return jnp.concatenate([tb[:, :, :-2], tb[:, :, 1:-1], tb[:, :, 2:]], axis=3)

    kb, vb = band(k), band(v)
    s_band = jnp.einsum('bhgnqd,bhnkd->bhgnqk', qb, kb, preferred_element_type=f32) * scale
    s_ctx = jnp.einsum('bhgnqd,bhkd->bhgnqk', qb, kc, preferred_element_type=f32) * scale
    r = jnp.arange(A_BLOCK)[:, None]
    j = jnp.arange(3 * A_BLOCK)[None, :]
    kpos = (jnp.arange(nb)[:, None, None] - 1) * A_BLOCK + j
    rel = j - A_BLOCK - r
    valid = (jnp.abs(rel) <= A_WINDOW)[None] & (kpos >= 0) & (kpos < n)
    s_band = jnp.where(valid, s_band, -1e30)
    p_band, p_ctx = sink_softmax([s_band, s_ctx], sink[None, :, :, None, None, None])
    o = (jnp.einsum('bhgnqk,bhnkd->bhgnqd', p_band, vb.astype(f32))
         + jnp.einsum('bhgnqk,bhkd->bhgnqd', p_ctx, vc.astype(f32)))
    return o.reshape(b, kvh, g, n, hd).astype(q.dtype)


def context_attention(q, k, v, sink):
    s = jnp.einsum('bhgqd,bhkd->bhgqk', q, k, preferred_element_type=f32) * (A_HEAD_DIM ** -0.5)
    (p,) = sink_softmax([s], sink[None, :, :, None, None])
    return jnp.einsum('bhgqk,bhkd->bhgqd', p, v.astype(f32)).astype(q.dtype)


def multi_scale_pool(x, p_w, p_scale):
    b, n, _ = x.shape
    xf = x.astype(f32).reshape(b, n, P_GROUPS, P_GROUP_W)
    csum = jnp.pad(jnp.cumsum(xf, axis=1), ((0, 0), (1, 0), (0, 0), (0, 0)))
    t = jnp.arange(n)
    outs = []
    for gi, w in enumerate(P_WINDOWS):
        lo = jnp.clip(t - w // 2, 0, n)
        hi = jnp.clip(t + w // 2, 0, n)
        cg = csum[:, :, gi]
        win_sum = jnp.take(cg, hi, axis=1) - jnp.take(cg, lo, axis=1)
        cnt = (hi - lo).astype(f32)[None, :, None]
        outs.append(win_sum / cnt - xf[:, :, gi])
    pooled = jnp.stack(outs, axis=2).astype(x.dtype)
    y = jnp.einsum('bngc,gcd->bngd', pooled, p_w).reshape(b, n, BRANCH_WIDTH)
    return y * p_scale


def chunk_spatial_gate(uv, ln_g, ln_b, ws, bs):
    u, v = jnp.split(jax.nn.gelu(uv), 2, axis=-1)
    v = layer_norm(v, ln_g, ln_b)
    b, n, _ = v.shape
    vc = v.reshape(b, n // C_CHUNK, C_CHUNK, C_GROUPS, C_GROUP_W)
    mixed = jnp.einsum('gpq,bcqgd->bcpgd', ws, vc) + bs.T[None, None, :, :, None]
    return u * mixed.reshape(b, n, BRANCH_WIDTH)


def to_heads(t, h):
    b, n, _ = t.shape
    return t.reshape(b, n, h, -1).transpose(0, 2, 1, 3).astype(f32)


def gla_chunk_scan(q, k, v, logg, s0):
    b, h, n, _ = q.shape
    nc = n // G_CHUNK

    def chunks(t):
        return jnp.moveaxis(t.reshape(b, h, nc, G_CHUNK, t.shape[-1]), 2, 0)

    lower = jnp.tril(jnp.ones((G_CHUNK, G_CHUNK), dtype=bool))[:, :, None]

    def step(state, blk):
        qb, kb, vb, gb = blk
        cum = jnp.cumsum(gb, axis=2)
        o_state = jnp.einsum('bhld,bhde->bhle', qb * jnp.exp(cum), state)
        rel = cum[:, :, :, None, :] - cum[:, :, None, :, :]
        decay = jnp.exp(jnp.where(lower, rel, -jnp.inf))
        att = jnp.einsum('bhid,bhjd,bhijd->bhij', qb, kb, decay)
        o_local = jnp.einsum('bhij,bhje->bhie', att, vb)
        last = cum[:, :, -1:, :]
        new_state = (jnp.exp(last[:, :, 0, :, None]) * state
                     + jnp.einsum('bhld,bhle->bhde', kb * jnp.exp(last - cum), vb))
        return new_state, o_state + o_local

    s_fin, o = lax.scan(step, s0, (chunks(q), chunks(k), chunks(v), chunks(logg)))
    return jnp.moveaxis(o, 0, 2).reshape(b, h, n, -1), s_fin


def gla_output(o_dirs, r, norm_g):
    o = o_dirs[0] + o_dirs[1]
    o = o * lax.rsqrt(jnp.mean(o * o, axis=-1, keepdims=True) + EPS)
    b, h, n, dv = o.shape
    o = o.transpose(0, 2, 1, 3).reshape(b, n, h * dv) * norm_g.astype(f32)
    return (o * jax.nn.silu(r.astype(f32))).astype(r.dtype)


def gla_bidirectional(zx, zc, g_w2, g_b, g_norm_g, need_ctx_out):
    outs_x, outs_c = [], []
    for d in range(2):
        def prep(z):
            q, k, v, _, glr = z
            logit = glr[..., d * G_GATE_RANK:(d + 1) * G_GATE_RANK] @ g_w2[d] + g_b[d]
            logg = jax.nn.log_sigmoid(logit.astype(f32)) / G_GATE_TAU
            t = (to_heads(q, G_HEADS) * (G_KEY_DIM ** -0.5), to_heads(k, G_HEADS),
                 to_heads(v, G_HEADS), to_heads(logg, G_HEADS))
            return tuple(jnp.flip(a, axis=2) for a in t) if d == 1 else t
        s0 = jnp.zeros((zc[0].shape[0], G_HEADS, G_KEY_DIM, G_VAL_DIM), f32)
        oc, s_ctx = gla_chunk_scan(*prep(zc), s0)
        ox, _ = gla_chunk_scan(*prep(zx), s_ctx)
        if d == 1:
            oc, ox = jnp.flip(oc, axis=2), jnp.flip(ox, axis=2)
        outs_c.append(oc)
        outs_x.append(ox)
    out_x = gla_output(outs_x, zx[3], g_norm_g)
    out_c = gla_output(outs_c, zc[3], g_norm_g) if need_ctx_out else None
    return out_x, out_c


def gated_merge(h, branch_outs, w_branch, w_gate, w_out):
    acc = jnp.zeros(h.shape, f32)
    for i, o in enumerate(branch_outs):
        gate = jax.nn.sigmoid((h @ w_gate[i]).astype(f32))
        acc = acc + gate * (o @ w_branch[i]).astype(f32)
    return acc.astype(h.dtype) @ w_out


def mixer_sublayer(hx, hc, w_in, a_sink, p_w, p_scale, c_ln_g, c_ln_b, c_ws, c_bs,
                   g_w2, g_b, g_norm_g, w_branch, w_gate, w_out, rope_cos, rope_sin, need_ctx_out):
    split_at = np.cumsum(IN_SPLITS)[:-1].tolist()
    zx = jnp.split(hx @ w_in, split_at, axis=-1)
    zc = jnp.split(hc @ w_in, split_at, axis=-1)
    sink = a_sink.astype(f32).reshape(A_KV_HEADS, A_GROUP)
    qx = apply_axial_rope(q_heads(zx[0]), rope_cos, rope_sin)
    kx = apply_axial_rope(kv_heads(zx[1]), rope_cos, rope_sin)
    vx = kv_heads(zx[2])
    qc, kc, vc = q_heads(zc[0]), kv_heads(zc[1]), kv_heads(zc[2])
    a_x = merge_attn_heads(window_attention(qx, kx, vx, kc, vc, sink))
    p_x = multi_scale_pool(zx[3], p_w, p_scale)
    s_x = chunk_spatial_gate(zx[4], c_ln_g, c_ln_b, c_ws, c_bs)
    g_x, g_c = gla_bidirectional(zx[5:], zc[5:], g_w2, g_b, g_norm_g, need_ctx_out)
    out_x = gated_merge(hx, (a_x, p_x, s_x, g_x), w_branch, w_gate, w_out)
    if not need_ctx_out:
        return out_x, None
    a_c = merge_attn_heads(context_attention(qc, kc, vc, sink))
    p_c = multi_scale_pool(zc[3], p_w, p_scale)
    s_c = chunk_spatial_gate(zc[4], c_ln_g, c_ln_b, c_ws, c_bs)
    out_c = gated_merge(hc, (a_c, p_c, s_c, g_c), w_branch, w_gate, w_out)
    return out_x, out_c


def moe_ffn(h, router_w, router_b, e_gate, e_up, e_down):
    scores = jax.nn.sigmoid((h @ router_w).astype(f32))
    biased = scores + router_b.astype(f32)
    grp = biased.reshape(*biased.shape[:-1], N_EXPERT_GROUPS, EXPERTS_PER_GROUP)
    grp_score = jnp.sum(lax.top_k(grp, GROUP_SCORE_TOP)[0], axis=-1)
    best = jnp.argmax(grp_score, axis=-1)
    in_grp = (jnp.arange(N_EXPERTS) // EXPERTS_PER_GROUP) == best[..., None]
    _, idx = lax.top_k(jnp.where(in_grp, biased, -jnp.inf), TOP_K)
    w_sel = jnp.take_along_axis(scores, idx, axis=-1)
    w_sel = w_sel / jnp.sum(w_sel, axis=-1, keepdims=True)
    combine = jnp.einsum('bnk,bnke->bne', w_sel, jax.nn.one_hot(idx, N_EXPERTS, dtype=f32))
    out = jnp.zeros(h.shape, f32)
    for e in range(N_EXPERTS):
        act = jax.nn.silu(h @ e_gate[e]) * (h @ e_up[e])
        out = out + combine[..., e:e + 1] * (act @ e_down[e]).astype(f32)
    return out.astype(h.dtype)


def setup_inputs(seed: int = 0) -> dict:
    key = jax.random.key(seed)
    ks = jax.random.split(key, 32)
    L, D = DEPTH, D_MODEL

    def nrm(k, shape, scale):
        return jax.random.normal(k, shape, f32) * scale

    return {
        'x': nrm(ks[0], (BATCH, SEQ, D), 1.0),
        'c': nrm(ks[1], (BATCH, D), 1.0),
        'ctx': nrm(ks[2], (BATCH, CTX_LEN, D), 1.0),
        'c_ctx': nrm(ks[3], (D,), 1.0),
        'w_mod': nrm(ks[4], (L, D, 6 * D), 0.5 * D ** -0.5),
        'b_mod': nrm(ks[5], (L, 6 * D), 0.01),
        'norm1_g': 1 + nrm(ks[6], (L, D), 0.01),
        'norm2_g': 1 + nrm(ks[7], (L, D), 0.01),
        'final_norm_g': 1 + nrm(ks[8], (D,), 0.01),
        'w_in': nrm(ks[9], (L, D, D_IN), D ** -0.5),
        'a_sink': nrm(ks[10], (L, A_HEADS), 1.0),
        'p_w': nrm(ks[11], (L, P_GROUPS, P_GROUP_W, P_GROUP_W), P_GROUP_W ** -0.5),
        'p_scale': 1 + nrm(ks[12], (L, BRANCH_WIDTH), 0.02),
        'c_ln_g': 1 + nrm(ks[13], (L, BRANCH_WIDTH), 0.01),
        'c_ln_b': nrm(ks[14], (L, BRANCH_WIDTH), 0.01),
        'c_ws': nrm(ks[15], (L, C_GROUPS, C_CHUNK, C_CHUNK), C_CHUNK ** -0.5),
        'c_bs': 1 + nrm(ks[16], (L, C_GROUPS, C_CHUNK), 0.01),
        'g_w2': nrm(ks[17], (L, 2, G_GATE_RANK, G_HEADS * G_KEY_DIM), G_GATE_RANK ** -0.5),
        'g_b': nrm(ks[18], (L, 2, G_HEADS * G_KEY_DIM), 0.1),
        'g_norm_g': 1 + nrm(ks[19], (L, BRANCH_WIDTH), 0.01),
        'w_branch': nrm(ks[20], (L, N_BRANCHES, BRANCH_WIDTH, D), BRANCH_WIDTH ** -0.5),
        'w_gate': nrm(ks[21], (L, N_BRANCHES, D, D), D ** -0.5),
        'w_out': nrm(ks[22], (L, D, D), D ** -0.5),
        'router_w': nrm(ks[23], (D, N_EXPERTS), D ** -0.5),
        'router_b': nrm(ks[24], (N_EXPERTS,), 0.01),
        'e_gate': nrm(ks[25], (L, N_EXPERTS, D, D_EXPERT), D ** -0.5),
        'e_up': nrm(ks[26], (L, N_EXPERTS, D, D_EXPERT), D ** -0.5),
        'e_down': nrm(ks[27], (L, N_EXPERTS, D_EXPERT, D), D_EXPERT ** -0.5),
    }


def reference(x, c, ctx, c_ctx, w_mod, b_mod, norm1_g, norm2_g, final_norm_g, w_in, a_sink,
              p_w, p_scale, c_ln_g, c_ln_b, c_ws, c_bs, g_w2, g_b, g_norm_g,
              w_branch, w_gate, w_out, router_w, router_b, e_gate, e_up, e_down):
    n_ctx = ctx.shape[1]
    rope_cos, rope_sin = axial_rope_tables(x.shape[1])
    for l in range(DEPTH):
        last = l == DEPTH - 1
        mod_x = jax.nn.silu(c) @ w_mod[l] + b_mod[l]
        mod_c = jax.nn.silu(c_ctx) @ w_mod[l] + b_mod[l]
        shx1, scx1, gx1, shx2, scx2, gx2 = jnp.split(mod_x[:, None, :], 6, axis=-1)
        shc1, scc1, gc1, shc2, scc2, gc2 = jnp.split(mod_c, 6, axis=-1)
        hx = modulate(rms_norm(x, norm1_g[l]), shx1, scx1)
        hc = modulate(rms_norm(ctx, norm1_g[l]), shc1, scc1)
        mx, mc = mixer_sublayer(hx, hc, w_in[l], a_sink[l], p_w[l], p_scale[l], c_ln_g[l], c_ln_b[l],
                                c_ws[l], c_bs[l], g_w2[l], g_b[l], g_norm_g[l], w_branch[l], w_gate[l],
                                w_out[l], rope_cos, rope_sin, not last)
        x = x + gx1 * mx
        hx = modulate(rms_norm(x, norm2_g[l]), shx2, scx2)
        if last:
            x = x + gx2 * moe_ffn(hx, router_w, router_b, e_gate[l], e_up[l], e_down[l])
        else:
            ctx = ctx + gc1 * mc
            hc = modulate(rms_norm(ctx, norm2_g[l]), shc2, scc2)
            f = moe_ffn(jnp.concatenate([hc, hx], axis=1), router_w, router_b, e_gate[l], e_up[l], e_down[l])
            ctx = ctx + gc2 * f[:, :n_ctx]
            x = x + gx2 * f[:, n_ctx:]
    return rms_norm(x, final_norm_g)
```

```python
import functools

import numpy as np
import jax
import jax.numpy as jnp
from jax import lax
from jax.experimental import pallas as pl
from jax.experimental.pallas import tpu as pltpu

f32 = jnp.float32
bf16 = jnp.bfloat16

EPS = 1e-6
GRID_W = 64
ROPE_BASE = 10000.0

HEAD_DIM = 64
N_Q_HEADS = 8
N_KV_HEADS = 2
Q_PER_KV = N_Q_HEADS // N_KV_HEADS
ATT_BLOCK = 128
POOL_WINDOWS = (2, 4, 8, 16)
POOL_HALO = 8
SG_CHUNK = 128
GLA_HEADS = 4
GLA_DK = 64
GLA_DV = 128
GLA_RANK = 16
GLA_TAU = 16.0
GLA_CHUNK = 64
GLA_LEVELS = (2, 4, 8, 16, 32, 64)
N_EXPERTS = 16
EXPERTS_PER_GROUP = 4
N_GROUPS = 4

ROW_TILE = 512
GLA_BLOCK = 256
MOE_TILE = 256
VMEM_LIMIT = 56 * 1024 * 1024


def _cparams(*sem):
    return pltpu.CompilerParams(dimension_semantics=sem, vmem_limit_bytes=VMEM_LIMIT)


def _dot(a, b, dims=(((1,), (0,)), ((), ())), precision=None):
    return lax.dot_general(a, b, dims, precision=precision, preferred_element_type=f32)


_NT = (((1,), (1,)), ((), ()))
_TN = (((0,), (0,)), ((), ()))


def _mod_kernel(c_ref, w_ref, b_ref, o_ref):
    c = c_ref[...]
    a = c * jax.nn.sigmoid(c)
    o_ref[...] = _dot(a, w_ref[...], precision=lax.Precision.HIGHEST) + b_ref[...]


def _modulation(cond, w_mod, b_mod):
    n_layers, d, d6 = w_mod.shape
    tn = 1024
    return pl.pallas_call(
        _mod_kernel,
        grid=(n_layers, d6 // tn),
        in_specs=[
            pl.BlockSpec((8, d), lambda l, j: (0, 0)),
            pl.BlockSpec((None, d, tn), lambda l, j: (l, 0, j)),
            pl.BlockSpec((None, 1, tn), lambda l, j: (l, 0, j)),
        ],
        out_specs=pl.BlockSpec((None, 8, tn), lambda l, j: (l, 0, j)),
        out_shape=jax.ShapeDtypeStruct((n_layers, 8, d6), f32),
        compiler_params=_cparams("arbitrary", "arbitrary"),
        name="modulation",
    )(cond, w_mod, b_mod.reshape(n_layers, 1, d6))


def _norm_mod(x, g, shift, scale):
    y = x * lax.rsqrt(jnp.mean(x * x, axis=-1, keepdims=True) + EPS)
    return (y * g) * (1.0 + scale) + shift


def _in_proj_kernel(x_ref, g_ref, sh_ref, sc_ref, cos_ref, sin_ref, w_ref,
                    h_ref, qkv_ref, pool_ref, uv_ref, gla_ref):
    hb = _norm_mod(x_ref[...], g_ref[...], sh_ref[...], sc_ref[...]).astype(bf16)
    h_ref[...] = hb

    cos = cos_ref[...]
    sin = sin_ref[...]
    lane = lax.broadcasted_iota(jnp.int32, cos.shape, 1)
    first_half = (lane % 32) < 16

    def rope(z):
        rot = jnp.where(first_half, pltpu.roll(z, 112, 1), pltpu.roll(z, 16, 1))
        return z * cos + rot * sin

    zq = _dot(hb, w_ref[:, 0:512])
    for c in range(4):
        qkv_ref[:, c * 128:(c + 1) * 128] = (
            rope(zq[:, c * 128:(c + 1) * 128]) * (HEAD_DIM ** -0.5)).astype(bf16)
    zkv = _dot(hb, w_ref[:, 512:768])
    qkv_ref[:, 512:640] = rope(zkv[:, 0:128]).astype(bf16)
    qkv_ref[:, 640:768] = zkv[:, 128:256].astype(bf16)
    pool_ref[...] = _dot(hb, w_ref[:, 768:1280])
    uv_ref[...] = _dot(hb, w_ref[:, 1280:2304])
    gla_ref[...] = _dot(hb, w_ref[:, 2304:3968])


def _in_proj(x_all, g, mods, cos_t, sin_t, w_pad, tiles_per_mod):
    rows, d = x_all.shape
    n_w = w_pad.shape[1]
    tm = ROW_TILE
    mod_spec = lambda k: pl.BlockSpec((None, 1, d), lambda i: (i // tiles_per_mod, 0, k))
    row_spec = lambda w: pl.BlockSpec((tm, w), lambda i: (i, 0))
    return pl.pallas_call(
        _in_proj_kernel,
        grid=(rows // tm,),
        in_specs=[
            row_spec(d),
            pl.BlockSpec((1, d), lambda i: (0, 0)),
            mod_spec(0), mod_spec(1),
            row_spec(128), row_spec(128),
            pl.BlockSpec((d, n_w), lambda i: (0, 0)),
        ],
        out_specs=[row_spec(d), row_spec(768), row_spec(512), row_spec(1024), row_spec(1664)],
        out_shape=[
            jax.ShapeDtypeStruct((rows, d), bf16),
            jax.ShapeDtypeStruct((rows, 768), bf16),
            jax.ShapeDtypeStruct((rows, 512), f32),
            jax.ShapeDtypeStruct((rows, 1024), f32),
            jax.ShapeDtypeStruct((rows, 1664), f32),
        ],
        compiler_params=_cparams("arbitrary"),
        name="in_proj",
    )(x_all, g, mods, mods, cos_t, sin_t, w_pad)


def _sink_column(sink_ref, kvh, rows_per_head):
    r = lax.broadcasted_iota(jnp.int32, (Q_PER_KV * rows_per_head, 1), 0) // rows_per_head
    col = jnp.full(r.shape, sink_ref[kvh * Q_PER_KV], f32)
    for g in range(1, Q_PER_KV):
        col = jnp.where(r == g, sink_ref[kvh * Q_PER_KV + g], col)
    return col


def _win_attn_kernel(sink_ref, main_ref, prev_ref, next_ref, ctx_ref, o_ref, *, tiles_per_seq, seq_len):
    i = pl.program_id(0)
    tq = main_ref.shape[0]
    n_sub = tq // ATT_BLOCK
    blk0 = (i % tiles_per_seq) * n_sub
    r = lax.broadcasted_iota(jnp.int32, (Q_PER_KV * ATT_BLOCK, 3 * ATT_BLOCK), 0) % ATT_BLOCK
    j = lax.broadcasted_iota(jnp.int32, (Q_PER_KV * ATT_BLOCK, 3 * ATT_BLOCK), 1)
    rel = j - ATT_BLOCK - r
    in_window = jnp.abs(rel) <= ATT_BLOCK
    n_ctx = ctx_ref.shape[0]
    for kvh in range(N_KV_HEADS):
        kc = 512 + kvh * HEAD_DIM
        vc = 640 + kvh * HEAD_DIM
        k_all = jnp.concatenate([prev_ref[:, kc:kc + HEAD_DIM], main_ref[:, kc:kc + HEAD_DIM],
                                 next_ref[:, kc:kc + HEAD_DIM]], axis=0)
        v_all = jnp.concatenate([prev_ref[:, vc:vc + HEAD_DIM], main_ref[:, vc:vc + HEAD_DIM],
                                 next_ref[:, vc:vc + HEAD_DIM]], axis=0)
        k_ctx = ctx_ref[:, kc:kc + HEAD_DIM]
        v_ctx = ctx_ref[:, vc:vc + HEAD_DIM]
        sink = _sink_column(sink_ref, kvh, ATT_BLOCK)
        for sb in range(n_sub):
            rows = slice(sb * ATT_BLOCK, (sb + 1) * ATT_BLOCK)
            q = jnp.concatenate(
                [main_ref[rows, (kvh * Q_PER_KV + g) * HEAD_DIM:(kvh * Q_PER_KV + g + 1) * HEAD_DIM]
                 for g in range(Q_PER_KV)], axis=0)
            band = slice(sb * ATT_BLOCK, (sb + 3) * ATT_BLOCK)
            s_band = _dot(q, k_all[band], _NT)
            s_ctx = _dot(q, k_ctx, _NT)
            kpos = (blk0 + sb - 1) * ATT_BLOCK + j
            valid = in_window & (kpos >= 0) & (kpos < seq_len)
            s_band = jnp.where(valid, s_band, -1e30)
            m = jnp.maximum(sink, jnp.maximum(jnp.max(s_band, axis=-1, keepdims=True),
                                              jnp.max(s_ctx, axis=-1, keepdims=True)))
            p_band = jnp.exp(s_band - m)
            p_ctx = jnp.exp(s_ctx - m)
            denom = (jnp.exp(sink - m) + jnp.sum(p_band, axis=-1, keepdims=True)
                     + jnp.sum(p_ctx, axis=-1, keepdims=True))
            o = (_dot(p_band.astype(bf16), v_all[band]) + _dot(p_ctx.astype(bf16), v_ctx)) / denom
            for g in range(Q_PER_KV):
                c0 = (kvh * Q_PER_KV + g) * HEAD_DIM
                o_ref[rows, c0:c0 + HEAD_DIM] = o[g * ATT_BLOCK:(g + 1) * ATT_BLOCK].astype(o_ref.dtype)


def _window_attention(qkv, sink, n_batch, seq_len, n_ctx):
    tq = ROW_TILE
    n_lat = n_batch * seq_len
    tiles_per_seq = seq_len // tq
    sub = tq // ATT_BLOCK
    n_blocks = n_lat // ATT_BLOCK
    ctx_blk0 = n_lat // n_ctx
    w = qkv.shape[1]
    kern = functools.partial(_win_attn_kernel, tiles_per_seq=tiles_per_seq, seq_len=seq_len)
    return pl.pallas_call(
        kern,
        grid=(n_lat // tq,),
        in_specs=[
            pl.BlockSpec(memory_space=pltpu.SMEM),
            pl.BlockSpec((tq, w), lambda i: (i, 0)),
            pl.BlockSpec((ATT_BLOCK, w), lambda i: (jnp.maximum(i * sub - 1, 0), 0)),
            pl.BlockSpec((ATT_BLOCK, w), lambda i: (jnp.minimum(i * sub + sub, n_blocks - 1), 0)),
            pl.BlockSpec((n_ctx, w), lambda i: (ctx_blk0 + i // tiles_per_seq, 0)),
        ],
        out_specs=pl.BlockSpec((tq, 512), lambda i: (i, 0)),
        out_shape=jax.ShapeDtypeStruct((n_lat, 512), bf16),
        compiler_params=_cparams("arbitrary"),
        name="window_attention",
    )(sink, qkv, qkv, qkv, qkv)


def _ctx_attn_kernel(sink_ref, qkv_ref, o_ref):
    n = qkv_ref.shape[0]
    for kvh in range(N_KV_HEADS):
        kc = 512 + kvh * HEAD_DIM
        vc = 640 + kvh * HEAD_DIM
        q = jnp.concatenate(
            [qkv_ref[:, (kvh * Q_PER_KV + g) * HEAD_DIM:(kvh * Q_PER_KV + g + 1) * HEAD_DIM]
             for g in range(Q_PER_KV)], axis=0)
        s = _dot(q, qkv_ref[:, kc:kc + HEAD_DIM], _NT)
        sink = _sink_column(sink_ref, kvh, n)
        m = jnp.maximum(sink, jnp.max(s, axis=-1, keepdims=True))
        p = jnp.exp(s - m)
        denom = jnp.exp(sink - m) + jnp.sum(p, axis=-1, keepdims=True)
        o = _dot(p.astype(bf16), qkv_ref[:, vc:vc + HEAD_DIM]) / denom
        for g in range(Q_PER_KV):
            c0 = (kvh * Q_PER_KV + g) * HEAD_DIM
            o_ref[:, c0:c0 + HEAD_DIM] = o[g * n:(g + 1) * n].astype(o_ref.dtype)


def _context_attention(qkv, sink, n_batch, n_lat, n_ctx):
    w = qkv.shape[1]
    blk0 = n_lat // n_ctx
    return pl.pallas_call(
        _ctx_attn_kernel,
        grid=(n_batch,),
        in_specs=[pl.BlockSpec(memory_space=pltpu.SMEM),
                  pl.BlockSpec((n_ctx, w), lambda b: (blk0 + b, 0))],
        out_specs=pl.BlockSpec((n_ctx, 512), lambda b: (b, 0)),
        out_shape=jax.ShapeDtypeStruct((n_batch * n_ctx, 512), bf16),
        compiler_params=_cparams("arbitrary"),
        name="context_attention",
    )(sink, qkv)


def _pool_kernel(main_ref, prev_ref, next_ref, pw_ref, ps_ref, o_ref, xe_ref, *, tiles_per_seq, seq_len):
    i = pl.program_id(0)
    tp = main_ref.shape[0]
    t_in_seq = i % tiles_per_seq
    h = POOL_HALO
    xe_ref[0:h, :] = jnp.where(t_in_seq == 0, 0.0, prev_ref[...])
    xe_ref[h:h + tp, :] = main_ref[...]
    xe_ref[h + tp:2 * h + tp, :] = jnp.where(t_in_seq == tiles_per_seq - 1, 0.0, next_ref[...])
    pos = t_in_seq * tp + lax.broadcasted_iota(jnp.int32, (tp, 1), 0)
    for gi, w in enumerate(POOL_WINDOWS):
        cols = slice(gi * 128, (gi + 1) * 128)
        acc = xe_ref[h - w // 2:h - w // 2 + tp, cols]
        for u in range(-w // 2 + 1, w // 2):
            acc = acc + xe_ref[h + u:h + u + tp, cols]
        lo = jnp.maximum(pos - w // 2, 0)
        hi = jnp.minimum(pos + w // 2, seq_len)
        cnt = (hi - lo).astype(f32)
        pooled = acc / cnt - main_ref[:, cols]
        y = _dot(pooled.astype(bf16), pw_ref[gi])
        o_ref[:, cols] = (y * ps_ref[:, cols]).astype(o_ref.dtype)


def _multi_scale_pool(z, row0, n_rows, seq_len, tp, p_w, p_scale):
    tiles_per_seq = seq_len // tp
    t0 = row0 // tp
    h0 = row0 // POOL_HALO
    hb = tp // POOL_HALO
    n_halo = n_rows // POOL_HALO
    kern = functools.partial(_pool_kernel, tiles_per_seq=tiles_per_seq, seq_len=seq_len)
    return pl.pallas_call(
        kern,
        grid=(n_rows // tp,),
        in_specs=[
            pl.BlockSpec((tp, 512), lambda i: (t0 + i, 0)),
            pl.BlockSpec((POOL_HALO, 512), lambda i: (h0 + jnp.maximum(i * hb - 1, 0), 0)),
            pl.BlockSpec((POOL_HALO, 512), lambda i: (h0 + jnp.minimum((i + 1) * hb, n_halo - 1), 0)),
            pl.BlockSpec((4, 128, 128), lambda i: (0, 0, 0)),
            pl.BlockSpec((1, 512), lambda i: (0, 0)),
        ],
        out_specs=pl.BlockSpec((tp, 512), lambda i: (i, 0)),
        out_shape=jax.ShapeDtypeStruct((n_rows, 512), bf16),
        scratch_shapes=[pltpu.VMEM((tp + 2 * POOL_HALO, 512), f32)],
        compiler_params=_cparams("arbitrary"),
        name="multi_scale_pool",
    )(z, z, z, p_w, p_scale)


def _spatial_gate_kernel(uv_ref, lg_ref, lb_ref, ws_ref, bs_ref, o_ref):
    a = jax.nn.gelu(uv_ref[...])
    u = a[:, 0:512]
    v = a[:, 512:1024]
    mu = jnp.mean(v, axis=-1, keepdims=True)
    var = jnp.mean(jnp.square(v - mu), axis=-1, keepdims=True)
    vn = ((v - mu) * lax.rsqrt(var + EPS) * lg_ref[...] + lb_ref[...]).astype(bf16)
    for c in range(uv_ref.shape[0] // SG_CHUNK):
        rows = slice(c * SG_CHUNK, (c + 1) * SG_CHUNK)
        for g in range(4):
            cols = slice(g * 128, (g + 1) * 128)
            mixed = _dot(ws_ref[g], vn[rows, cols]) + bs_ref[g]
            o_ref[rows, cols] = (u[rows, cols] * mixed).astype(o_ref.dtype)


def _spatial_gate(uv, n_rows, ln_g, ln_b, ws, bs_b):
    tm = ROW_TILE
    return pl.pallas_call(
        _spatial_gate_kernel,
        grid=(n_rows // tm,),
        in_specs=[
            pl.BlockSpec((tm, 1024), lambda i: (i, 0)),
            pl.BlockSpec((1, 512), lambda i: (0, 0)),
            pl.BlockSpec((1, 512), lambda i: (0, 0)),
            pl.BlockSpec((4, 128, 128), lambda i: (0, 0, 0)),
            pl.BlockSpec((4, 128, 128), lambda i: (0, 0, 0)),
        ],
        out_specs=pl.BlockSpec((tm, 512), lambda i: (i, 0)),
        out_shape=jax.ShapeDtypeStruct((n_rows, 512), bf16),
        compiler_params=_cparams("arbitrary"),
        name="spatial_gate",
    )(uv, ln_g, ln_b, ws, bs_b)


def _gla_tables():
    c = GLA_CHUNK
    i = np.arange(c)[:, None]
    j = np.arange(c)[None, :]
    seg = np.zeros((2, 2 * len(GLA_LEVELS) * c, c), np.float32)
    for lv, s in enumerate(GLA_LEVELS):
        same = (i // s) == (j // s)
        seg[0, lv * c:(lv + 1) * c] = same & (j <= i)
        seg[1, lv * c:(lv + 1) * c] = same & (j >= i)
        o = (len(GLA_LEVELS) + lv) * c
        seg[0, o:o + c] = same & (j > i)
        seg[1, o:o + c] = same & (j < i)
    mask = np.zeros((2, 7, c, c), np.float32)
    mask[:, 0] = np.eye(c)
    for lv, s in enumerate((1, 2, 4, 8, 16, 32)):
        m = ((i // (2 * s)) == (j // (2 * s))) & ((i // s) % 2 == 1) & ((j // s) % 2 == 0)
        mask[0, 1 + lv] = m
        mask[1, 1 + lv] = m.T
    return seg, mask


def _log_sigmoid(x):
    return jnp.minimum(x, 0.0) - jnp.log(1.0 + jnp.exp(-jnp.abs(x)))


def _gla_chunk(d, q, k, v, glr, seg_ref, mask_ref, w2_ref, gb_ref, st_ref):
    c = GLA_CHUNK
    nl = len(GLA_LEVELS)
    logit = _dot(glr, w2_ref[d], precision=lax.Precision.HIGHEST) + gb_ref[d]
    g = _log_sigmoid(logit) / GLA_TAU
    g_hi = g.astype(bf16)
    r1 = g - g_hi.astype(f32)
    g_mid = r1.astype(bf16)
    g_lo = (r1 - g_mid.astype(f32)).astype(bf16)
    gs = jnp.concatenate([g_hi, g_mid, g_lo], axis=1)
    tab = _dot(seg_ref[d], gs)
    w = g.shape[1]
    tab = tab[:, 0:w] + tab[:, w:2 * w] + tab[:, 2 * w:3 * w]
    cq = [g] + [tab[lv * c:(lv + 1) * c] for lv in range(nl)]
    ck = [None] + [tab[(nl + lv) * c:(nl + lv + 1) * c] for lv in range(nl)]
    q = q * (GLA_DK ** -0.5)
    kb = k.astype(bf16)
    qs = [q.astype(bf16)] + [(q * jnp.exp(cq[lv])).astype(bf16) for lv in range(nl + 1)]
    ks = [kb, kb] + [(k * jnp.exp(ck[lv])).astype(bf16) for lv in range(1, nl + 1)]
    tot_row = 0 if d == 1 else c - 1
    total = cq[nl][tot_row:tot_row + 1]
    vb = v.astype(bf16)
    outs = []
    for h in range(GLA_HEADS):
        kc = slice(h * GLA_DK, (h + 1) * GLA_DK)
        vc = slice(h * GLA_DV, (h + 1) * GLA_DV)
        att = mask_ref[d, 0] * _dot(qs[0][:, kc], ks[0][:, kc], _NT)
        for lv in range(1, 7):
            att = att + mask_ref[d, lv] * _dot(qs[lv][:, kc], ks[lv][:, kc], _NT)
        st = st_ref[d, h]
        o = _dot(att.astype(bf16), vb[:, vc]) + _dot(qs[7][:, kc], st.astype(bf16), _NT)
        outs.append(o)
        st_ref[d, h] = st * jnp.exp(total[:, kc]) + _dot(vb[:, vc], ks[7][:, kc], _TN)
    return jnp.concatenate(outs, axis=1)


def _gla_kernel(qf_ref, kf_ref, vf_ref, gf_ref, qb_ref, kb_ref, vb_ref, gb_in_ref,
                seg_ref, mask_ref, w2_ref, gbias_ref, of_ref, ob_ref, st_ref):
    @pl.when(pl.program_id(1) == 0)
    def _():
        st_ref[...] = jnp.zeros_like(st_ref)

    n_chunks = qf_ref.shape[0] // GLA_CHUNK

    def body(ci, carry):
        rf = pl.ds(pl.multiple_of(ci * GLA_CHUNK, GLA_CHUNK), GLA_CHUNK)
        of_ref[rf, :] = _gla_chunk(0, qf_ref[rf, :], kf_ref[rf, :], vf_ref[rf, :], gf_ref[rf, :],
                                   seg_ref, mask_ref, w2_ref, gbias_ref, st_ref)
        rb = pl.ds(pl.multiple_of((n_chunks - 1 - ci) * GLA_CHUNK, GLA_CHUNK), GLA_CHUNK)
        ob_ref[rb, :] = _gla_chunk(1, qb_ref[rb, :], kb_ref[rb, :], vb_ref[rb, :], gb_in_ref[rb, :],
                                   seg_ref, mask_ref, w2_ref, gbias_ref, st_ref)
        return carry

    lax.fori_loop(0, n_chunks, body, 0)


def _gla_scan(z_gla, n_batch, seq_len, n_ctx, w2p, gbias):
    rows = z_gla.shape[0]
    tb = GLA_BLOCK
    assert n_ctx == tb
    lat_blocks = seq_len // tb
    ctx_blk0 = n_batch * lat_blocks
    seg, mask = _gla_tables()

    def fwd_row(b, s):
        return jnp.where(s == 0, ctx_blk0 + b, b * lat_blocks + s - 1)

    def bwd_row(b, s):
        return jnp.where(s == 0, ctx_blk0 + b, b * lat_blocks + lat_blocks - s)

    def specs(row):
        return [
            pl.BlockSpec((tb, 256), lambda b, s: (row(b, s), 0)),
            pl.BlockSpec((tb, 256), lambda b, s: (row(b, s), 1)),
            pl.BlockSpec((tb, 512), lambda b, s: (row(b, s), 1)),
            pl.BlockSpec((tb, 128), lambda b, s: (row(b, s), 12)),
        ]

    const = lambda shape: pl.BlockSpec(shape, lambda b, s: (0,) * len(shape))
    return pl.pallas_call(
        _gla_kernel,
        grid=(n_batch, 1 + lat_blocks),
        in_specs=specs(fwd_row) + specs(bwd_row) + [
            const(seg.shape), const(mask.shape), const(w2p.shape), const(gbias.shape)],
        out_specs=[pl.BlockSpec((tb, 512), lambda b, s: (fwd_row(b, s), 0)),
                   pl.BlockSpec((tb, 512), lambda b, s: (bwd_row(b, s), 0))],
        out_shape=[jax.ShapeDtypeStruct((rows, 512), f32)] * 2,
        scratch_shapes=[pltpu.VMEM((2, GLA_HEADS, GLA_DV, GLA_DK), f32)],
        compiler_params=_cparams("arbitrary", "arbitrary"),
        name="gla_scan",
    )(*([z_gla] * 8), jnp.asarray(seg, bf16), jnp.asarray(mask, f32), w2p, gbias)


def _gla_out_kernel(of_ref, ob_ref, r_ref, ng_ref, o_ref):
    o = of_ref[...] + ob_ref[...]
    r = r_ref[...]
    gate = r * jax.nn.sigmoid(r)
    for h in range(GLA_HEADS):
        cols = slice(h * GLA_DV, (h + 1) * GLA_DV)
        oh = o[:, cols]
        oh = oh * lax.rsqrt(jnp.mean(oh * oh, axis=-1, keepdims=True) + EPS)
        o_ref[:, cols] = (oh * ng_ref[:, cols] * gate[:, cols]).astype(o_ref.dtype)


def _gla_output(o_f, o_b, z_gla, n_rows, norm_g):
    tm = ROW_TILE
    return pl.pallas_call(
        _gla_out_kernel,
        grid=(n_rows // tm,),
        in_specs=[
            pl.BlockSpec((tm, 512), lambda i: (i, 0)),
            pl.BlockSpec((tm, 512), lambda i: (i, 0)),
            pl.BlockSpec((tm, 512), lambda i: (i, 2)),
            pl.BlockSpec((1, 512), lambda i: (0, 0)),
        ],
        out_specs=pl.BlockSpec((tm, 512), lambda i: (i, 0)),
        out_shape=jax.ShapeDtypeStruct((n_rows, 512), bf16),
        compiler_params=_cparams("arbitrary"),
        name="gla_output",
    )(o_f, o_b, z_gla, norm_g)


def _merge_kernel(h_ref, a_ref, p_ref, s_ref, g_ref, wg_ref, wb_ref, o_ref):
    hb = h_ref[...]
    acc = None
    for br, b_ref in enumerate((a_ref, p_ref, s_ref, g_ref)):
        gate = jax.nn.sigmoid(_dot(hb, wg_ref[br]))
        term = gate * _dot(b_ref[...], wb_ref[br])
        acc = term if acc is None else acc + term
    o_ref[...] = acc.astype(o_ref.dtype)


def _gated_merge(h, branches, n_rows, wg, wb):
    d = h.shape[1]
    tm, tn = ROW_TILE, 512
    bspec = pl.BlockSpec((tm, 512), lambda i, j: (i, 0))
    return pl.pallas_call(
        _merge_kernel,
        grid=(n_rows // tm, d // tn),
        in_specs=[pl.BlockSpec((tm, d), lambda i, j: (i, 0)), bspec, bspec, bspec, bspec,
                  pl.BlockSpec((4, d, tn), lambda i, j: (0, 0, j)),
                  pl.BlockSpec((4, 512, tn), lambda i, j: (0, 0, j))],
        out_specs=pl.BlockSpec((tm, tn), lambda i, j: (i, j)),
        out_shape=jax.ShapeDtypeStruct((n_rows, d), bf16),
        compiler_params=_cparams("arbitrary", "arbitrary"),
        name="gated_merge",
    )(h, *branches, wg, wb)


def _out_proj_kernel(acc_ref, wo_ref, x_ref, gate_ref, o_ref):
    o_ref[...] = x_ref[...] + gate_ref[...] * _dot(acc_ref[...], wo_ref[...])


def _out_proj_residual(acc, wo, x_all, mods, n_rows, tiles_per_mod):
    d = acc.shape[1]
    tm = ROW_TILE
    return pl.pallas_call(
        _out_proj_kernel,
        grid=(n_rows // tm,),
        in_specs=[pl.BlockSpec((tm, d), lambda i: (i, 0)),
                  pl.BlockSpec((d, d), lambda i: (0, 0)),
                  pl.BlockSpec((tm, d), lambda i: (i, 0)),
                  pl.BlockSpec((None, 1, d), lambda i: (i // tiles_per_mod, 0, 2))],
        out_specs=pl.BlockSpec((tm, d), lambda i: (i, 0)),
        out_shape=jax.ShapeDtypeStruct((n_rows, d), f32),
        compiler_params=_cparams("arbitrary"),
        name="out_proj_residual",
    )(acc, wo, x_all, mods)


def _router_kernel(x_ref, g_ref, sh_ref, sc_ref, rw_ref, rb_ref, tri_ref, info_ref, cnt_ref, carry_ref):
    i = pl.program_id(0)

    @pl.when(i == 0)
    def _():
        carry_ref[...] = jnp.zeros_like(carry_ref)

    h2 = _norm_mod(x_ref[...], g_ref[...], sh_ref[...], sc_ref[...])
    tm = h2.shape[0]
    logits = _dot(rw_ref[...], h2, _NT, precision=lax.Precision.HIGHEST)
    score = jax.nn.sigmoid(logits)
    biased = score + rb_ref[...]
    b = [biased[e:e + 1] for e in range(N_EXPERTS)]
    sc = [score[e:e + 1] for e in range(N_EXPERTS)]

    def top2_sum(v):
        hi01, lo01 = jnp.maximum(v[0], v[1]), jnp.minimum(v[0], v[1])
        hi23, lo23 = jnp.maximum(v[2], v[3]), jnp.minimum(v[2], v[3])
        return jnp.maximum(hi01, hi23) + jnp.maximum(jnp.minimum(hi01, hi23), jnp.maximum(lo01, lo23))

    best = jnp.zeros((1, tm), jnp.int32)
    best_score = top2_sum(b[0:4])
    for gidx in range(1, N_GROUPS):
        gs = top2_sum(b[4 * gidx:4 * gidx + 4])
        take = gs > best_score
        best = jnp.where(take, gidx, best)
        best_score = jnp.where(take, gs, best_score)
    cb, cs = [], []
    for kk in range(EXPERTS_PER_GROUP):
        vb, vs = b[kk], sc[kk]
        for gidx in range(1, N_GROUPS):
            vb = jnp.where(best == gidx, b[4 * gidx + kk], vb)
            vs = jnp.where(best == gidx, sc[4 * gidx + kk], vs)
        cb.append(vb)
        cs.append(vs)
    i1 = jnp.zeros((1, tm), jnp.int32)
    m1, w1 = cb[0], cs[0]
    for kk in range(1, EXPERTS_PER_GROUP):
        take = cb[kk] > m1
        i1 = jnp.where(take, kk, i1)
        m1 = jnp.where(take, cb[kk], m1)
        w1 = jnp.where(take, cs[kk], w1)
    i2 = jnp.where(i1 == 0, 1, 0).astype(jnp.int32)
    m2 = jnp.where(i1 == 0, cb[1], cb[0])
    w2 = jnp.where(i1 == 0, cs[1], cs[0])
    for kk in range(1, EXPERTS_PER_GROUP):
        take = (cb[kk] > m2) & (i1 != kk) & (i2 != kk)
        i2 = jnp.where(take, kk, i2)
        m2 = jnp.where(take, cb[kk], m2)
        w2 = jnp.where(take, cs[kk], w2)
    e1 = best * EXPERTS_PER_GROUP + i1
    e2 = best * EXPERTS_PER_GROUP + i2
    wsum = w1 + w2
    erow = lax.broadcasted_iota(jnp.int32, (N_EXPERTS, tm), 0)
    oh1 = (erow == e1).astype(f32)
    oh2 = (erow == e2).astype(f32)
    oh = oh1 + oh2
    rank = _dot(oh.astype(bf16), tri_ref[...]) + carry_ref[:, 0:1]
    carry_ref[...] = carry_ref[...] + jnp.sum(oh, axis=1, keepdims=True)
    r1 = jnp.sum(oh1 * rank, axis=0, keepdims=True)
    r2 = jnp.sum(oh2 * rank, axis=0, keepdims=True)
    info_ref[...] = jnp.concatenate(
        [e1.astype(f32), e2.astype(f32), w1 / wsum, w2 / wsum, r1, r2,
         jnp.zeros((2, tm), f32)], axis=0)
    cnt_ref[...] = carry_ref[...]


def _router(x_all, n_rows, g, mods, rw_t, rb_col, tiles_per_mod):
    d = x_all.shape[1]
    tm = ROW_TILE
    tri = jnp.asarray(np.triu(np.ones((tm, tm), np.float32), k=1), bf16)
    mod_spec = lambda k: pl.BlockSpec((None, 1, d), lambda i: (i // tiles_per_mod, 0, k))
    return pl.pallas_call(
        _router_kernel,
        grid=(n_rows // tm,),
        in_specs=[pl.BlockSpec((tm, d), lambda i: (i, 0)),
                  pl.BlockSpec((1, d), lambda i: (0, 0)),
                  mod_spec(3), mod_spec(4),
                  pl.BlockSpec((N_EXPERTS, d), lambda i: (0, 0)),
                  pl.BlockSpec((N_EXPERTS, 1), lambda i: (0, 0)),
                  pl.BlockSpec((tm, tm), lambda i: (0, 0))],
        out_specs=[pl.BlockSpec((8, tm), lambda i: (0, i)),
                   pl.BlockSpec((N_EXPERTS, 128), lambda i: (0, 0))],
        out_shape=[jax.ShapeDtypeStruct((8, n_rows), f32),
                   jax.ShapeDtypeStruct((N_EXPERTS, 128), f32)],
        scratch_shapes=[pltpu.VMEM((N_EXPERTS, 128), f32)],
        compiler_params=_cparams("arbitrary"),
        name="moe_router",
    )(x_all, g, mods, mods, rw_t, rb_col, tri)


def _dispatch_kernel(pos_ref, x_ref, g_ref, sh_ref, sc_ref, xs_in_ref, xs_ref, buf_ref, sem):
    del xs_in_ref
    i = pl.program_id(0)
    tm = x_ref.shape[0]
    buf_ref[...] = _norm_mod(x_ref[...], g_ref[...], sh_ref[...], sc_ref[...])

    def row_copy(r, k):
        dst = pos_ref[k, i * tm + r]
        return pltpu.make_async_copy(buf_ref.at[pl.ds(r, 1)], xs_ref.at[pl.ds(dst, 1)], sem)

    def issue(r, c):
        row_copy(r, 0).start()
        row_copy(r, 1).start()
        return c

    lax.fori_loop(0, tm, issue, 0)

    def drain(r, c):
        row_copy(r, 0).wait()
        row_copy(r, 1).wait()
        return c

    lax.fori_loop(0, tm, drain, 0)


def _dispatch(pos, x_all, n_rows, g, mods, tiles_per_mod, p_rows):
    d = x_all.shape[1]
    tm = MOE_TILE
    tpm = tiles_per_mod * (ROW_TILE // tm)
    mod_spec = lambda k: pl.BlockSpec((None, 1, d), lambda i, pos: (i // tpm, 0, k))
    xs0 = jnp.zeros((p_rows, d), f32)
    return pl.pallas_call(
        _dispatch_kernel,
        grid_spec=pltpu.PrefetchScalarGridSpec(
            num_scalar_prefetch=1,
            grid=(n_rows // tm,),
            in_specs=[pl.BlockSpec((tm, d), lambda i, pos: (i, 0)),
                      pl.BlockSpec((1, d), lambda i, pos: (0, 0)),
                      mod_spec(3), mod_spec(4),
                      pl.BlockSpec(memory_space=pl.ANY)],
            out_specs=pl.BlockSpec(memory_space=pl.ANY),
            scratch_shapes=[pltpu.VMEM((tm, d), f32), pltpu.SemaphoreType.DMA(())],
        ),
        out_shape=jax.ShapeDtypeStruct((p_rows, d), f32),
        input_output_aliases={5: 0},
        compiler_params=_cparams("arbitrary"),
        name="moe_dispatch",
    )(pos, x_all, g, mods, mods, xs0)


def _expert_kernel(te_ref, nu_ref, xs_ref, wg_ref, wu_ref, wd_ref, y_ref):
    i = pl.program_id(0)

    @pl.when(i < nu_ref[0])
    def _():
        xb = xs_ref[...].astype(bf16)
        a = _dot(xb, wg_ref[...])
        act = (a * jax.nn.sigmoid(a)) * _dot(xb, wu_ref[...])
        y_ref[...] = _dot(act.astype(bf16), wd_ref[...])

    @pl.when(i >= nu_ref[0])
    def _():
        y_ref[...] = jnp.zeros_like(y_ref)


def _expert_ffn(tile_expert, n_used, xs, wg, wu, wd):
    p_rows, d = xs.shape
    de = wg.shape[2]
    tm = MOE_TILE
    return pl.pallas_call(
        _expert_kernel,
        grid_spec=pltpu.PrefetchScalarGridSpec(
            num_scalar_prefetch=2,
            grid=(p_rows // tm,),
            in_specs=[pl.BlockSpec((tm, d), lambda i, te, nu: (i, 0)),
                      pl.BlockSpec((None, d, de), lambda i, te, nu: (te[i], 0, 0)),
                      pl.BlockSpec((None, d, de), lambda i, te, nu: (te[i], 0, 0)),
                      pl.BlockSpec((None, de, d), lambda i, te, nu: (te[i], 0, 0))],
            out_specs=pl.BlockSpec((tm, d), lambda i, te, nu: (i, 0)),
        ),
        out_shape=jax.ShapeDtypeStruct((p_rows, d), f32),
        compiler_params=_cparams("arbitrary"),
        name="moe_experts",
    )(tile_expert, n_used, xs, wg, wu, wd)


def _combine_kernel(pos_ref, x_ref, w_ref, gate_ref, fg_ref, y_ref, o_ref, y1_ref, y2_ref, sem, *, final_norm):
    i = pl.program_id(0)
    tm = x_ref.shape[0]

    def row_copy(r, k, buf):
        src = pos_ref[k, i * tm + r]
        return pltpu.make_async_copy(y_ref.at[pl.ds(src, 1)], buf.at[pl.ds(r, 1)], sem)

    def issue(r, c):
        row_copy(r, 0, y1_ref).start()
        row_copy(r, 1, y2_ref).start()
        return c

    lax.fori_loop(0, tm, issue, 0)

    def drain(r, c):
        row_copy(r, 0, y1_ref).wait()
        row_copy(r, 1, y2_ref).wait()
        return c

    lax.fori_loop(0, tm, drain, 0)
    w = w_ref[...]
    moe = w[:, 0:1] * y1_ref[...] + w[:, 1:2] * y2_ref[...]
    x = x_ref[...] + gate_ref[...] * moe
    if final_norm:
        x = x * lax.rsqrt(jnp.mean(x * x, axis=-1, keepdims=True) + EPS) * fg_ref[...]
    o_ref[...] = x


def _combine(pos, x_all, n_rows, w_tok, mods, final_g, y, tiles_per_mod, final_norm):
    d = x_all.shape[1]
    tm = MOE_TILE
    tpm = tiles_per_mod * (ROW_TILE // tm)
    kern = functools.partial(_combine_kernel, final_norm=final_norm)
    return pl.pallas_call(
        kern,
        grid_spec=pltpu.PrefetchScalarGridSpec(
            num_scalar_prefetch=1,
            grid=(n_rows // tm,),
            in_specs=[pl.BlockSpec((tm, d), lambda i, pos: (i, 0)),
                      pl.BlockSpec((tm, 2), lambda i, pos: (i, 0)),
                      pl.BlockSpec((None, 1, d), lambda i, pos: (i // tpm, 0, 5)),
                      pl.BlockSpec((1, d), lambda i, pos: (0, 0)),
                      pl.BlockSpec(memory_space=pl.ANY)],
            out_specs=pl.BlockSpec((tm, d), lambda i, pos: (i, 0)),
            scratch_shapes=[pltpu.VMEM((tm, d), f32), pltpu.VMEM((tm, d), f32),
                            pltpu.SemaphoreType.DMA(())],
        ),
        out_shape=jax.ShapeDtypeStruct((n_rows, d), f32),
        compiler_params=_cparams("arbitrary"),
        name="moe_combine",
    )(pos, x_all, w_tok, mods, final_g, y)


def _moe(x_all, n_rows, norm_g, mods, tiles_per_mod, rw_t, rb_col, wg, wu, wd, final_g, final_norm):
    info, counts = _router(x_all, n_rows, norm_g, mods, rw_t, rb_col, tiles_per_mod)
    tm = MOE_TILE
    p_rows = 2 * n_rows + N_EXPERTS * tm
    cnt = counts[:, 0].astype(jnp.int32)
    padded = ((cnt + tm - 1) // tm) * tm
    ends = jnp.cumsum(padded)
    starts = ends - padded
    e1 = info[0].astype(jnp.int32)
    e2 = info[1].astype(jnp.int32)
    pos = jnp.stack([starts[e1] + info[4].astype(jnp.int32),
                     starts[e2] + info[5].astype(jnp.int32)])
    n_used = (ends[-1] // tm).astype(jnp.int32)
    tile_start = jnp.arange(p_rows // tm, dtype=jnp.int32) * tm
    tile_expert = jnp.searchsorted(ends, jnp.minimum(tile_start, ends[-1] - 1), side="right")
    tile_expert = jnp.minimum(tile_expert, N_EXPERTS - 1).astype(jnp.int32)
    xs = _dispatch(pos, x_all, n_rows, norm_g, mods, tiles_per_mod, p_rows)
    y = _expert_ffn(tile_expert, n_used.reshape(1), xs, wg, wu, wd)
    w_tok = jnp.transpose(info[2:4])
    return _combine(pos, x_all, n_rows, w_tok, mods, final_g, y, tiles_per_mod, final_norm)


def _rope_tables(seq_len, n_batch, n_ctx_rows):
    rows = seq_len // GRID_W
    row = jnp.repeat(jnp.arange(rows), GRID_W)
    col = jnp.tile(jnp.arange(GRID_W), rows)
    nf = HEAD_DIM // 4
    inv_freq = ROPE_BASE ** (-jnp.arange(nf, dtype=f32) / nf)
    ang_r = row[:, None].astype(f32) * inv_freq
    ang_c = col[:, None].astype(f32) * inv_freq
    cos64 = jnp.concatenate([jnp.cos(ang_r)] * 2 + [jnp.cos(ang_c)] * 2, axis=-1)
    sin64 = jnp.concatenate([-jnp.sin(ang_r), jnp.sin(ang_r), -jnp.sin(ang_c), jnp.sin(ang_c)], axis=-1)
    cos_t = jnp.tile(jnp.concatenate([cos64, cos64], axis=-1), (n_batch, 1))
    sin_t = jnp.tile(jnp.concatenate([sin64, sin64], axis=-1), (n_batch, 1))
    cos_t = jnp.concatenate([cos_t, jnp.ones((n_ctx_rows, 128), f32)], axis=0)
    sin_t = jnp.concatenate([sin_t, jnp.zeros((n_ctx_rows, 128), f32)], axis=0)
    return cos_t, sin_t


def kernel(x, c, ctx, c_ctx, w_mod, b_mod, norm1_g, norm2_g, final_norm_g, w_in, a_sink,
           p_w, p_scale, c_ln_g, c_ln_b, c_ws, c_bs, g_w2, g_b, g_norm_g,
           w_branch, w_gate, w_out, router_w, router_b, e_gate, e_up, e_down):
    n_batch, seq_len, d = x.shape
    n_ctx = ctx.shape[1]
    depth = w_mod.shape[0]
    n_lat = n_batch * seq_len
    n_all = n_lat + n_batch * n_ctx
    tiles_per_mod = seq_len // ROW_TILE

    cond = jnp.zeros((8, d), f32).at[0:n_batch].set(c).at[n_batch].set(c_ctx)
    mods_all = _modulation(cond, w_mod, b_mod)
    cos_t, sin_t = _rope_tables(seq_len, n_batch, n_batch * n_ctx)
    rw_t = jnp.transpose(router_w)
    rb_col = router_b.reshape(N_EXPERTS, 1)
    final_g = final_norm_g.reshape(1, d)

    x_all = jnp.concatenate([x.reshape(n_lat, d), ctx.reshape(n_batch * n_ctx, d)], axis=0)
    for l in range(depth):
        last = l == depth - 1
        n_out = n_lat if last else n_all
        mods = mods_all[l].reshape(8, 1, 6 * d)
        w_pad = jnp.pad(w_in[l], ((0, 0), (0, 96))).astype(bf16)
        h, qkv, z_pool, z_uv, z_gla = _in_proj(x_all, norm1_g[l].reshape(1, d), mods, cos_t, sin_t,
                                               w_pad, tiles_per_mod)
        a_br = _window_attention(qkv, a_sink[l], n_batch, seq_len, n_ctx)
        pw = p_w[l].astype(bf16)
        ps = p_scale[l].reshape(1, 512)
        p_br = _multi_scale_pool(z_pool, 0, n_lat, seq_len, ROW_TILE, pw, ps)
        bs_b = jnp.broadcast_to(c_bs[l][:, :, None], (4, SG_CHUNK, 128))
        s_br = _spatial_gate(z_uv, n_out, c_ln_g[l].reshape(1, 512), c_ln_b[l].reshape(1, 512),
                             c_ws[l].astype(bf16), bs_b)
        w2p = jnp.zeros((2, 128, 256), f32)
        w2p = w2p.at[0, 0:GLA_RANK].set(g_w2[l, 0]).at[1, GLA_RANK:2 * GLA_RANK].set(g_w2[l, 1])
        o_f, o_b = _gla_scan(z_gla, n_batch, seq_len, n_ctx, w2p, g_b[l].reshape(2, 1, 256))
        g_br = _gla_output(o_f, o_b, z_gla, n_out, g_norm_g[l].reshape(1, 512))
        if not last:
            a_c = _context_attention(qkv, a_sink[l], n_batch, n_lat, n_ctx)
            p_c = _multi_scale_pool(z_pool, n_lat, n_batch * n_ctx, n_ctx, n_ctx, pw, ps)
            a_br = jnp.concatenate([a_br, a_c], axis=0)
            p_br = jnp.concatenate([p_br, p_c], axis=0)
        acc = _gated_merge(h, (a_br, p_br, s_br, g_br), n_out, w_gate[l].astype(bf16),
                           w_branch[l].astype(bf16))
        x_all = _out_proj_residual(acc, w_out[l].astype(bf16), x_all, mods, n_out, tiles_per_mod)
        x_all = _moe(x_all, n_out, norm2_g[l].reshape(1, d), mods, tiles_per_mod, rw_t, rb_col,
                     e_gate[l].astype(bf16), e_up[l].astype(bf16), e_down[l].astype(bf16),
                     final_g, last)
    return x_all.reshape(n_batch, seq_len, d)
```

```python
import functools

import numpy as np
import jax
import jax.numpy as jnp
from jax import lax
from jax.experimental import pallas as pl
from jax.experimental.pallas import tpu as pltpu

f32 = jnp.float32
bf16 = jnp.bfloat16

EPS = 1e-6
GRID_W = 64
ROPE_BASE = 10000.0

HEAD_DIM = 64
N_Q_HEADS = 8
N_KV_HEADS = 2
Q_PER_KV = N_Q_HEADS // N_KV_HEADS
ATT_BLOCK = 128
POOL_WINDOWS = (2, 4, 8, 16)
POOL_HALO = 8
SG_CHUNK = 128
GLA_HEADS = 4
GLA_DK = 64
GLA_DV = 128
GLA_RANK = 16
GLA_TAU = 16.0
GLA_CHUNK = 64
GLA_LEVELS = (2, 4, 8, 16, 32, 64)
N_EXPERTS = 16
EXPERTS_PER_GROUP = 4
N_GROUPS = 4

ROW_TILE = 512
GLA_BLOCK = 256
MOE_TILE = 256
VMEM_LIMIT = 56 * 1024 * 1024


def _cparams(*sem):
    return pltpu.CompilerParams(dimension_semantics=sem, vmem_limit_bytes=VMEM_LIMIT)


def _dot(a, b, dims=(((1,), (0,)), ((), ())), precision=None):
    return lax.dot_general(a, b, dims, precision=precision, preferred_element_type=f32)


_NT = (((1,), (1,)), ((), ()))
_TN = (((0,), (0,)), ((), ()))


def _mod_kernel(c_ref, w_ref, b_ref, o_ref):
    c = c_ref[...]
    a = c * jax.nn.sigmoid(c)
    o_ref[...] = _dot(a, w_ref[...], precision=lax.Precision.HIGHEST) + b_ref[...]


def _modulation(cond, w_mod, b_mod):
    n_layers, d, d6 = w_mod.shape
    tn = 1024
    return pl.pallas_call(
        _mod_kernel,
        grid=(n_layers, d6 // tn),
        in_specs=[
            pl.BlockSpec((8, d), lambda l, j: (0, 0)),
            pl.BlockSpec((None, d, tn), lambda l, j: (l, 0, j)),
            pl.BlockSpec((None, 1, tn), lambda l, j: (l, 0, j)),
        ],
        out_specs=pl.BlockSpec((None, 8, tn), lambda l, j: (l, 0, j)),
        out_shape=jax.ShapeDtypeStruct((n_layers, 8, d6), f32),
        compiler_params=_cparams("arbitrary", "arbitrary"),
        name="modulation",
    )(cond, w_mod, b_mod.reshape(n_layers, 1, d6))


def _norm_mod(x, g, shift, scale):
    y = x * lax.rsqrt(jnp.mean(x * x, axis=-1, keepdims=True) + EPS)
    return (y * g) * (1.0 + scale) + shift


def _two_source_rows(xa_ref, xb_ref, n_a_tiles):
    return jnp.where(pl.program_id(0) < n_a_tiles, xa_ref[...], xb_ref[...])


def _two_source_specs(tm, d, n_a_tiles, b_tile0):
    return [pl.BlockSpec((tm, d), lambda i: (jnp.minimum(i, n_a_tiles - 1), 0)),
            pl.BlockSpec((tm, d), lambda i: (b_tile0 + jnp.maximum(i - n_a_tiles, 0), 0))]


def _in_proj_kernel(xa_ref, xb_ref, g_ref, sh_ref, sc_ref, cos_ref, sin_ref, w_ref,
                    h_ref, qkv_ref, pool_ref, uv_ref, gla_ref, *, n_a_tiles):
    x = _two_source_rows(xa_ref, xb_ref, n_a_tiles)
    hb = _norm_mod(x, g_ref[...], sh_ref[...], sc_ref[...]).astype(bf16)
    h_ref[...] = hb

    cos = cos_ref[...]
    sin = sin_ref[...]
    lane = lax.broadcasted_iota(jnp.int32, cos.shape, 1)
    first_half = (lane % 32) < 16

    def rope(z):
        rot = jnp.where(first_half, pltpu.roll(z, 112, 1), pltpu.roll(z, 16, 1))
        return z * cos + rot * sin

    zq = _dot(hb, w_ref[:, 0:512])
    for c in range(4):
        qkv_ref[:, c * 128:(c + 1) * 128] = (
            rope(zq[:, c * 128:(c + 1) * 128]) * (HEAD_DIM ** -0.5)).astype(bf16)
    zkv = _dot(hb, w_ref[:, 512:768])
    qkv_ref[:, 512:640] = rope(zkv[:, 0:128]).astype(bf16)
    qkv_ref[:, 640:768] = zkv[:, 128:256].astype(bf16)
    pool_ref[...] = _dot(hb, w_ref[:, 768:1280])
    uv_ref[...] = _dot(hb, w_ref[:, 1280:2304])
    gla_ref[...] = _dot(hb, w_ref[:, 2304:3968])


def _in_proj(xa, xb, n_a_tiles, b_tile0, rows, layer, g, mods, cos_t, sin_t, w_pad, tiles_per_mod):
    d = xa.shape[1]
    n_w = w_pad.shape[2]
    tm = ROW_TILE
    mod_spec = lambda k: pl.BlockSpec((None, 1, d), lambda i: (i // tiles_per_mod, 0, k))
    row_spec = lambda w: pl.BlockSpec((tm, w), lambda i: (i, 0))
    return pl.pallas_call(
        functools.partial(_in_proj_kernel, n_a_tiles=n_a_tiles),
        grid=(rows // tm,),
        in_specs=_two_source_specs(tm, d, n_a_tiles, b_tile0) + [
            pl.BlockSpec((1, d), lambda i: (0, 0)),
            mod_spec(0), mod_spec(1),
            row_spec(128), row_spec(128),
            pl.BlockSpec((None, d, n_w), lambda i: (layer, 0, 0), pipeline_mode=pl.Buffered(1)),
        ],
        out_specs=[row_spec(d), row_spec(768), row_spec(512), row_spec(1024), row_spec(1664)],
        out_shape=[
            jax.ShapeDtypeStruct((rows, d), bf16),
            jax.ShapeDtypeStruct((rows, 768), bf16),
            jax.ShapeDtypeStruct((rows, 512), f32),
            jax.ShapeDtypeStruct((rows, 1024), f32),
            jax.ShapeDtypeStruct((rows, 1664), f32),
        ],
        compiler_params=_cparams("arbitrary"),
        name="in_proj",
    )(xa, xb, g, mods, mods, cos_t, sin_t, w_pad)


def _sink_column(sink_ref, kvh, rows_per_head):
    r = lax.broadcasted_iota(jnp.int32, (Q_PER_KV * rows_per_head, 1), 0) // rows_per_head
    col = jnp.full(r.shape, sink_ref[kvh * Q_PER_KV], f32)
    for g in range(1, Q_PER_KV):
        col = jnp.where(r == g, sink_ref[kvh * Q_PER_KV + g], col)
    return col


def _win_attn_kernel(sink_ref, main_ref, prev_ref, next_ref, ctx_ref, o_ref, *, tiles_per_seq, seq_len):
    i = pl.program_id(0)
    tq = main_ref.shape[0]
    n_sub = tq // ATT_BLOCK
    blk0 = (i % tiles_per_seq) * n_sub
    r = lax.broadcasted_iota(jnp.int32, (Q_PER_KV * ATT_BLOCK, 3 * ATT_BLOCK), 0) % ATT_BLOCK
    j = lax.broadcasted_iota(jnp.int32, (Q_PER_KV * ATT_BLOCK, 3 * ATT_BLOCK), 1)
    rel = j - ATT_BLOCK - r
    in_window = jnp.abs(rel) <= ATT_BLOCK
    n_ctx = ctx_ref.shape[0]
    for kvh in range(N_KV_HEADS):
        kc = 512 + kvh * HEAD_DIM
        vc = 640 + kvh * HEAD_DIM
        k_all = jnp.concatenate([prev_ref[:, kc:kc + HEAD_DIM], main_ref[:, kc:kc + HEAD_DIM],
                                 next_ref[:, kc:kc + HEAD_DIM]], axis=0)
        v_all = jnp.concatenate([prev_ref[:, vc:vc + HEAD_DIM], main_ref[:, vc:vc + HEAD_DIM],
                                 next_ref[:, vc:vc + HEAD_DIM]], axis=0)
        k_ctx = ctx_ref[:, kc:kc + HEAD_DIM]
        v_ctx = ctx_ref[:, vc:vc + HEAD_DIM]
        sink = _sink_column(sink_ref, kvh, ATT_BLOCK)
        for sb in range(n_sub):
            rows = slice(sb * ATT_BLOCK, (sb + 1) * ATT_BLOCK)
            q = jnp.concatenate(
                [main_ref[rows, (kvh * Q_PER_KV + g) * HEAD_DIM:(kvh * Q_PER_KV + g + 1) * HEAD_DIM]
                 for g in range(Q_PER_KV)], axis=0)
            band = slice(sb * ATT_BLOCK, (sb + 3) * ATT_BLOCK)
            s_band = _dot(q, k_all[band], _NT)
            s_ctx = _dot(q, k_ctx, _NT)
            kpos = (blk0 + sb - 1) * ATT_BLOCK + j
            valid = in_window & (kpos >= 0) & (kpos < seq_len)
            s_band = jnp.where(valid, s_band, -1e30)
            m = jnp.maximum(sink, jnp.maximum(jnp.max(s_band, axis=-1, keepdims=True),
                                              jnp.max(s_ctx, axis=-1, keepdims=True)))
            p_band = jnp.exp(s_band - m)
            p_ctx = jnp.exp(s_ctx - m)
            denom = (jnp.exp(sink - m) + jnp.sum(p_band, axis=-1, keepdims=True)
                     + jnp.sum(p_ctx, axis=-1, keepdims=True))
            o = (_dot(p_band.astype(bf16), v_all[band]) + _dot(p_ctx.astype(bf16), v_ctx)) / denom
            for g in range(Q_PER_KV):
                c0 = (kvh * Q_PER_KV + g) * HEAD_DIM
                o_ref[rows, c0:c0 + HEAD_DIM] = o[g * ATT_BLOCK:(g + 1) * ATT_BLOCK].astype(o_ref.dtype)


def _window_attention(qkv, sink, n_batch, seq_len, n_ctx):
    tq = ROW_TILE
    n_lat = n_batch * seq_len
    tiles_per_seq = seq_len // tq
    sub = tq // ATT_BLOCK
    n_blocks = n_lat // ATT_BLOCK
    ctx_blk0 = n_lat // n_ctx
    w = qkv.shape[1]
    kern = functools.partial(_win_attn_kernel, tiles_per_seq=tiles_per_seq, seq_len=seq_len)
    return pl.pallas_call(
        kern,
        grid=(n_lat // tq,),
        in_specs=[
            pl.BlockSpec(memory_space=pltpu.SMEM),
            pl.BlockSpec((tq, w), lambda i: (i, 0)),
            pl.BlockSpec((ATT_BLOCK, w), lambda i: (jnp.maximum(i * sub - 1, 0), 0)),
            pl.BlockSpec((ATT_BLOCK, w), lambda i: (jnp.minimum(i * sub + sub, n_blocks - 1), 0)),
            pl.BlockSpec((n_ctx, w), lambda i: (ctx_blk0 + i // tiles_per_seq, 0)),
        ],
        out_specs=pl.BlockSpec((tq, 512), lambda i: (i, 0)),
        out_shape=jax.ShapeDtypeStruct((n_lat, 512), bf16),
        compiler_params=_cparams("arbitrary"),
        name="window_attention",
    )(sink, qkv, qkv, qkv, qkv)


def _ctx_attn_kernel(sink_ref, qkv_ref, o_ref):
    n = qkv_ref.shape[0]
    for kvh in range(N_KV_HEADS):
        kc = 512 + kvh * HEAD_DIM
        vc = 640 + kvh * HEAD_DIM
        q = jnp.concatenate(
            [qkv_ref[:, (kvh * Q_PER_KV + g) * HEAD_DIM:(kvh * Q_PER_KV + g + 1) * HEAD_DIM]
             for g in range(Q_PER_KV)], axis=0)
        s = _dot(q, qkv_ref[:, kc:kc + HEAD_DIM], _NT)
        sink = _sink_column(sink_ref, kvh, n)
        m = jnp.maximum(sink, jnp.max(s, axis=-1, keepdims=True))
        p = jnp.exp(s - m)
        denom = jnp.exp(sink - m) + jnp.sum(p, axis=-1, keepdims=True)
        o = _dot(p.astype(bf16), qkv_ref[:, vc:vc + HEAD_DIM]) / denom
        for g in range(Q_PER_KV):
            c0 = (kvh * Q_PER_KV + g) * HEAD_DIM
            o_ref[:, c0:c0 + HEAD_DIM] = o[g * n:(g + 1) * n].astype(o_ref.dtype)


def _context_attention(qkv, sink, n_batch, n_lat, n_ctx):
    w = qkv.shape[1]
    blk0 = n_lat // n_ctx
    return pl.pallas_call(
        _ctx_attn_kernel,
        grid=(n_batch,),
        in_specs=[pl.BlockSpec(memory_space=pltpu.SMEM),
                  pl.BlockSpec((n_ctx, w), lambda b: (blk0 + b, 0))],
        out_specs=pl.BlockSpec((n_ctx, 512), lambda b: (b, 0)),
        out_shape=jax.ShapeDtypeStruct((n_batch * n_ctx, 512), bf16),
        compiler_params=_cparams("arbitrary"),
        name="context_attention",
    )(sink, qkv)


def _pool_kernel(main_ref, prev_ref, next_ref, pw_ref, ps_ref, o_ref, xe_ref, *, tiles_per_seq, seq_len):
    i = pl.program_id(0)
    tp = main_ref.shape[0]
    t_in_seq = i % tiles_per_seq
    h = POOL_HALO
    xe_ref[0:h, :] = jnp.where(t_in_seq == 0, 0.0, prev_ref[...])
    xe_ref[h:h + tp, :] = main_ref[...]
    xe_ref[h + tp:2 * h + tp, :] = jnp.where(t_in_seq == tiles_per_seq - 1, 0.0, next_ref[...])
    pos = t_in_seq * tp + lax.broadcasted_iota(jnp.int32, (tp, 1), 0)
    for gi, w in enumerate(POOL_WINDOWS):
        cols = slice(gi * 128, (gi + 1) * 128)
        acc = xe_ref[h - w // 2:h - w // 2 + tp, cols]
        for u in range(-w // 2 + 1, w // 2):
            acc = acc + xe_ref[h + u:h + u + tp, cols]
        lo = jnp.maximum(pos - w // 2, 0)
        hi = jnp.minimum(pos + w // 2, seq_len)
        cnt = (hi - lo).astype(f32)
        pooled = acc / cnt - main_ref[:, cols]
        y = _dot(pooled.astype(bf16), pw_ref[gi])
        o_ref[:, cols] = (y * ps_ref[:, cols]).astype(o_ref.dtype)


def _multi_scale_pool(z, row0, n_rows, seq_len, tp, p_w, p_scale):
    tiles_per_seq = seq_len // tp
    t0 = row0 // tp
    h0 = row0 // POOL_HALO
    hb = tp // POOL_HALO
    n_halo = n_rows // POOL_HALO
    kern = functools.partial(_pool_kernel, tiles_per_seq=tiles_per_seq, seq_len=seq_len)
    return pl.pallas_call(
        kern,
        grid=(n_rows // tp,),
        in_specs=[
            pl.BlockSpec((tp, 512), lambda i: (t0 + i, 0)),
            pl.BlockSpec((POOL_HALO, 512), lambda i: (h0 + jnp.maximum(i * hb - 1, 0), 0)),
            pl.BlockSpec((POOL_HALO, 512), lambda i: (h0 + jnp.minimum((i + 1) * hb, n_halo - 1), 0)),
            pl.BlockSpec((4, 128, 128), lambda i: (0, 0, 0)),
            pl.BlockSpec((1, 512), lambda i: (0, 0)),
        ],
        out_specs=pl.BlockSpec((tp, 512), lambda i: (i, 0)),
        out_shape=jax.ShapeDtypeStruct((n_rows, 512), bf16),
        scratch_shapes=[pltpu.VMEM((tp + 2 * POOL_HALO, 512), f32)],
        compiler_params=_cparams("arbitrary"),
        name="multi_scale_pool",
    )(z, z, z, p_w, p_scale)


def _spatial_gate_kernel(uv_ref, lg_ref, lb_ref, ws_ref, bs_ref, o_ref):
    a = jax.nn.gelu(uv_ref[...])
    u = a[:, 0:512]
    v = a[:, 512:1024]
    mu = jnp.mean(v, axis=-1, keepdims=True)
    var = jnp.mean(jnp.square(v - mu), axis=-1, keepdims=True)
    vn = ((v - mu) * lax.rsqrt(var + EPS) * lg_ref[...] + lb_ref[...]).astype(bf16)
    for c in range(uv_ref.shape[0] // SG_CHUNK):
        rows = slice(c * SG_CHUNK, (c + 1) * SG_CHUNK)
        for g in range(4):
            cols = slice(g * 128, (g + 1) * 128)
            mixed = _dot(ws_ref[g], vn[rows, cols]) + bs_ref[g]
            o_ref[rows, cols] = (u[rows, cols] * mixed).astype(o_ref.dtype)


def _spatial_gate(uv, n_rows, ln_g, ln_b, ws, bs_b):
    tm = ROW_TILE
    return pl.pallas_call(
        _spatial_gate_kernel,
        grid=(n_rows // tm,),
        in_specs=[
            pl.BlockSpec((tm, 1024), lambda i: (i, 0)),
            pl.BlockSpec((1, 512), lambda i: (0, 0)),
            pl.BlockSpec((1, 512), lambda i: (0, 0)),
            pl.BlockSpec((4, 128, 128), lambda i: (0, 0, 0)),
            pl.BlockSpec((4, 128, 128), lambda i: (0, 0, 0)),
        ],
        out_specs=pl.BlockSpec((tm, 512), lambda i: (i, 0)),
        out_shape=jax.ShapeDtypeStruct((n_rows, 512), bf16),
        compiler_params=_cparams("arbitrary"),
        name="spatial_gate",
    )(uv, ln_g, ln_b, ws, bs_b)


def _gla_tables():
    c = GLA_CHUNK
    i = np.arange(c)[:, None]
    j = np.arange(c)[None, :]
    seg = np.zeros((2, 2 * len(GLA_LEVELS) * c, c), np.float32)
    for lv, s in enumerate(GLA_LEVELS):
        same = (i // s) == (j // s)
        seg[0, lv * c:(lv + 1) * c] = same & (j <= i)
        seg[1, lv * c:(lv + 1) * c] = same & (j >= i)
        o = (len(GLA_LEVELS) + lv) * c
        seg[0, o:o + c] = same & (j > i)
        seg[1, o:o + c] = same & (j < i)
    mask = np.zeros((2, 7, c, c), np.float32)
    mask[:, 0] = np.eye(c)
    for lv, s in enumerate((1, 2, 4, 8, 16, 32)):
        m = ((i // (2 * s)) == (j // (2 * s))) & ((i // s) % 2 == 1) & ((j // s) % 2 == 0)
        mask[0, 1 + lv] = m
        mask[1, 1 + lv] = m.T
    return seg, mask


def _log_sigmoid(x):
    return jnp.minimum(x, 0.0) - jnp.log(1.0 + jnp.exp(-jnp.abs(x)))


def _gla_chunk(d, q, k, v, glr, seg_ref, mask_ref, w2_ref, gb_ref, st_ref):
    c = GLA_CHUNK
    nl = len(GLA_LEVELS)
    logit = _dot(glr, w2_ref[d], precision=lax.Precision.HIGHEST) + gb_ref[d]
    g = _log_sigmoid(logit) / GLA_TAU
    g_hi = g.astype(bf16)
    r1 = g - g_hi.astype(f32)
    g_mid = r1.astype(bf16)
    g_lo = (r1 - g_mid.astype(f32)).astype(bf16)
    gs = jnp.concatenate([g_hi, g_mid, g_lo], axis=1)
    tab = _dot(seg_ref[d], gs)
    w = g.shape[1]
    tab = tab[:, 0:w] + tab[:, w:2 * w] + tab[:, 2 * w:3 * w]
    cq = [g] + [tab[lv * c:(lv + 1) * c] for lv in range(nl)]
    ck = [None] + [tab[(nl + lv) * c:(nl + lv + 1) * c] for lv in range(nl)]
    q = q * (GLA_DK ** -0.5)
    kb = k.astype(bf16)
    qs = [q.astype(bf16)] + [(q * jnp.exp(cq[lv])).astype(bf16) for lv in range(nl + 1)]
    ks = [kb, kb] + [(k * jnp.exp(ck[lv])).astype(bf16) for lv in range(1, nl + 1)]
    tot_row = 0 if d == 1 else c - 1
    total = cq[nl][tot_row:tot_row + 1]
    vb = v.astype(bf16)
    outs = []
    for h in range(GLA_HEADS):
        kc = slice(h * GLA_DK, (h + 1) * GLA_DK)
        vc = slice(h * GLA_DV, (h + 1) * GLA_DV)
        att = mask_ref[d, 0] * _dot(qs[0][:, kc], ks[0][:, kc], _NT)
        for lv in range(1, 7):
            att = att + mask_ref[d, lv] * _dot(qs[lv][:, kc], ks[lv][:, kc], _NT)
        st = st_ref[d, h]
        o = _dot(att.astype(bf16), vb[:, vc]) + _dot(qs[7][:, kc], st.astype(bf16), _NT)
        outs.append(o)
        st_ref[d, h] = st * jnp.exp(total[:, kc]) + _dot(vb[:, vc], ks[7][:, kc], _TN)
    return jnp.concatenate(outs, axis=1)


def _gla_kernel(qf_ref, kf_ref, vf_ref, gf_ref, qb_ref, kb_ref, vb_ref, gb_in_ref,
                seg_ref, mask_ref, w2_ref, gbias_ref, of_ref, ob_ref, st_ref):
    @pl.when(pl.program_id(1) == 0)
    def _():
        st_ref[...] = jnp.zeros_like(st_ref)

    n_chunks = qf_ref.shape[0] // GLA_CHUNK

    def body(ci, carry):
        rf = pl.ds(pl.multiple_of(ci * GLA_CHUNK, GLA_CHUNK), GLA_CHUNK)
        of_ref[rf, :] = _gla_chunk(0, qf_ref[rf, :], kf_ref[rf, :], vf_ref[rf, :], gf_ref[rf, :],
                                   seg_ref, mask_ref, w2_ref, gbias_ref, st_ref)
        rb = pl.ds(pl.multiple_of((n_chunks - 1 - ci) * GLA_CHUNK, GLA_CHUNK), GLA_CHUNK)
        ob_ref[rb, :] = _gla_chunk(1, qb_ref[rb, :], kb_ref[rb, :], vb_ref[rb, :], gb_in_ref[rb, :],
                                   seg_ref, mask_ref, w2_ref, gbias_ref, st_ref)
        return carry

    lax.fori_loop(0, n_chunks, body, 0)


def _gla_scan(z_gla, n_batch, seq_len, n_ctx, w2p, gbias):
    rows = z_gla.shape[0]
    tb = GLA_BLOCK
    assert n_ctx == tb
    lat_blocks = seq_len // tb
    ctx_blk0 = n_batch * lat_blocks
    seg, mask = _gla_tables()

    def fwd_row(b, s):
        return jnp.where(s == 0, ctx_blk0 + b, b * lat_blocks + s - 1)

    def bwd_row(b, s):
        return jnp.where(s == 0, ctx_blk0 + b, b * lat_blocks + lat_blocks - s)

    def specs(row):
        return [
            pl.BlockSpec((tb, 256), lambda b, s: (row(b, s), 0)),
            pl.BlockSpec((tb, 256), lambda b, s: (row(b, s), 1)),
            pl.BlockSpec((tb, 512), lambda b, s: (row(b, s), 1)),
            pl.BlockSpec((tb, 128), lambda b, s: (row(b, s), 12)),
        ]

    const = lambda shape: pl.BlockSpec(shape, lambda b, s: (0,) * len(shape))
    return pl.pallas_call(
        _gla_kernel,
        grid=(n_batch, 1 + lat_blocks),
        in_specs=specs(fwd_row) + specs(bwd_row) + [
            const(seg.shape), const(mask.shape), const(w2p.shape), const(gbias.shape)],
        out_specs=[pl.BlockSpec((tb, 512), lambda b, s: (fwd_row(b, s), 0)),
                   pl.BlockSpec((tb, 512), lambda b, s: (bwd_row(b, s), 0))],
        out_shape=[jax.ShapeDtypeStruct((rows, 512), f32)] * 2,
        scratch_shapes=[pltpu.VMEM((2, GLA_HEADS, GLA_DV, GLA_DK), f32)],
        compiler_params=_cparams("arbitrary", "arbitrary"),
        name="gla_scan",
    )(*([z_gla] * 8), jnp.asarray(seg, bf16), jnp.asarray(mask, f32), w2p, gbias)


def _gla_out_kernel(of_ref, ob_ref, r_ref, ng_ref, o_ref):
    o = of_ref[...] + ob_ref[...]
    r = r_ref[...]
    gate = r * jax.nn.sigmoid(r)
    for h in range(GLA_HEADS):
        cols = slice(h * GLA_DV, (h + 1) * GLA_DV)
        oh = o[:, cols]
        oh = oh * lax.rsqrt(jnp.mean(oh * oh, axis=-1, keepdims=True) + EPS)
        o_ref[:, cols] = (oh * ng_ref[:, cols] * gate[:, cols]).astype(o_ref.dtype)


def _gla_output(o_f, o_b, z_gla, n_rows, norm_g):
    tm = ROW_TILE
    return pl.pallas_call(
        _gla_out_kernel,
        grid=(n_rows // tm,),
        in_specs=[
            pl.BlockSpec((tm, 512), lambda i: (i, 0)),
            pl.BlockSpec((tm, 512), lambda i: (i, 0)),
            pl.BlockSpec((tm, 512), lambda i: (i, 2)),
            pl.BlockSpec((1, 512), lambda i: (0, 0)),
        ],
        out_specs=pl.BlockSpec((tm, 512), lambda i: (i, 0)),
        out_shape=jax.ShapeDtypeStruct((n_rows, 512), bf16),
        compiler_params=_cparams("arbitrary"),
        name="gla_output",
    )(o_f, o_b, z_gla, norm_g)


def _merge_kernel(h_ref, a_ref, p_ref, s_ref, g_ref, wg_ref, wb_ref, o_ref):
    hb = h_ref[...]
    acc = None
    for br, b_ref in enumerate((a_ref, p_ref, s_ref, g_ref)):
        gate = jax.nn.sigmoid(_dot(hb, wg_ref[br]))
        term = gate * _dot(b_ref[...], wb_ref[br])
        acc = term if acc is None else acc + term
    o_ref[...] = acc.astype(o_ref.dtype)


def _gated_merge(h, branches, n_rows, layer, wg, wb):
    d = h.shape[1]
    tm, tn = ROW_TILE, 512
    bspec = pl.BlockSpec((tm, 512), lambda i, j: (i, 0))
    return pl.pallas_call(
        _merge_kernel,
        grid=(n_rows // tm, d // tn),
        in_specs=[pl.BlockSpec((tm, d), lambda i, j: (i, 0)), bspec, bspec, bspec, bspec,
                  pl.BlockSpec((None, 4, d, tn), lambda i, j: (layer, 0, 0, j)),
                  pl.BlockSpec((None, 4, 512, tn), lambda i, j: (layer, 0, 0, j))],
        out_specs=pl.BlockSpec((tm, tn), lambda i, j: (i, j)),
        out_shape=jax.ShapeDtypeStruct((n_rows, d), bf16),
        compiler_params=_cparams("arbitrary", "arbitrary"),
        name="gated_merge",
    )(h, *branches, wg, wb)


def _out_proj_kernel(acc_ref, wo_ref, xa_ref, xb_ref, gate_ref, o_ref, *, n_a_tiles):
    x = _two_source_rows(xa_ref, xb_ref, n_a_tiles)
    o_ref[...] = x + gate_ref[...] * _dot(acc_ref[...], wo_ref[...])


def _out_proj_residual(acc, layer, wo, xa, xb, n_a_tiles, b_tile0, mods, n_rows, tiles_per_mod):
    d = acc.shape[1]
    tm = ROW_TILE
    return pl.pallas_call(
        functools.partial(_out_proj_kernel, n_a_tiles=n_a_tiles),
        grid=(n_rows // tm,),
        in_specs=[pl.BlockSpec((tm, d), lambda i: (i, 0)),
                  pl.BlockSpec((None, d, d), lambda i: (layer, 0, 0))]
        + _two_source_specs(tm, d, n_a_tiles, b_tile0)
        + [pl.BlockSpec((None, 1, d), lambda i: (i // tiles_per_mod, 0, 2))],
        out_specs=pl.BlockSpec((tm, d), lambda i: (i, 0)),
        out_shape=jax.ShapeDtypeStruct((n_rows, d), f32),
        compiler_params=_cparams("arbitrary"),
        name="out_proj_residual",
    )(acc, wo, xa, xb, mods)


def _router_kernel(x_ref, g_ref, sh_ref, sc_ref, rw_ref, rb_ref, tri_ref, info_ref, cnt_ref, carry_ref):
    i = pl.program_id(0)

    @pl.when(i == 0)
    def _():
        carry_ref[...] = jnp.zeros_like(carry_ref)

    h2 = _norm_mod(x_ref[...], g_ref[...], sh_ref[...], sc_ref[...])
    tm = h2.shape[0]
    logits = _dot(rw_ref[...], h2, _NT, precision=lax.Precision.HIGHEST)
    score = jax.nn.sigmoid(logits)
    biased = score + rb_ref[...]
    b = [biased[e:e + 1] for e in range(N_EXPERTS)]
    sc = [score[e:e + 1] for e in range(N_EXPERTS)]

    def top2_sum(v):
        hi01, lo01 = jnp.maximum(v[0], v[1]), jnp.minimum(v[0], v[1])
        hi23, lo23 = jnp.maximum(v[2], v[3]), jnp.minimum(v[2], v[3])
        return jnp.maximum(hi01, hi23) + jnp.maximum(jnp.minimum(hi01, hi23), jnp.maximum(lo01, lo23))

    best = jnp.zeros((1, tm), jnp.int32)
    best_score = top2_sum(b[0:4])
    for gidx in range(1, N_GROUPS):
        gs = top2_sum(b[4 * gidx:4 * gidx + 4])
        take = gs > best_score
        best = jnp.where(take, gidx, best)
        best_score = jnp.where(take, gs, best_score)
    cb, cs = [], []
    for kk in range(EXPERTS_PER_GROUP):
        vb, vs = b[kk], sc[kk]
        for gidx in range(1, N_GROUPS):
            vb = jnp.where(best == gidx, b[4 * gidx + kk], vb)
            vs = jnp.where(best == gidx, sc[4 * gidx + kk], vs)
        cb.append(vb)
        cs.append(vs)
    i1 = jnp.zeros((1, tm), jnp.int32)
    m1, w1 = cb[0], cs[0]
    for kk in range(1, EXPERTS_PER_GROUP):
        take = cb[kk] > m1
        i1 = jnp.where(take, kk, i1)
        m1 = jnp.where(take, cb[kk], m1)
        w1 = jnp.where(take, cs[kk], w1)
    i2 = jnp.where(i1 == 0, 1, 0).astype(jnp.int32)
    m2 = jnp.where(i1 == 0, cb[1], cb[0])
    w2 = jnp.where(i1 == 0, cs[1], cs[0])
    for kk in range(1, EXPERTS_PER_GROUP):
        take = (cb[kk] > m2) & (i1 != kk) & (i2 != kk)
        i2 = jnp.where(take, kk, i2)
        m2 = jnp.where(take, cb[kk], m2)
        w2 = jnp.where(take, cs[kk], w2)
    e1 = best * EXPERTS_PER_GROUP + i1
    e2 = best * EXPERTS_PER_GROUP + i2
    wsum = w1 + w2
    erow = lax.broadcasted_iota(jnp.int32, (N_EXPERTS, tm), 0)
    oh1 = (erow == e1).astype(f32)
    oh2 = (erow == e2).astype(f32)
    oh = oh1 + oh2
    rank = _dot(oh.astype(bf16), tri_ref[...]) + carry_ref[:, 0:1]
    carry_ref[...] = carry_ref[...] + jnp.sum(oh, axis=1, keepdims=True)
    r1 = jnp.sum(oh1 * rank, axis=0, keepdims=True)
    r2 = jnp.sum(oh2 * rank, axis=0, keepdims=True)
    info_ref[...] = jnp.concatenate(
        [e1.astype(f32), e2.astype(f32), w1 / wsum, w2 / wsum, r1, r2,
         jnp.zeros((2, tm), f32)], axis=0)
    cnt_ref[...] = carry_ref[...]


def _router(x_all, n_rows, g, mods, rw_t, rb_col, tiles_per_mod):
    d = x_all.shape[1]
    tm = ROW_TILE
    tri = jnp.asarray(np.triu(np.ones((tm, tm), np.float32), k=1), bf16)
    mod_spec = lambda k: pl.BlockSpec((None, 1, d), lambda i: (i // tiles_per_mod, 0, k))
    return pl.pallas_call(
        _router_kernel,
        grid=(n_rows // tm,),
        in_specs=[pl.BlockSpec((tm, d), lambda i: (i, 0)),
                  pl.BlockSpec((1, d), lambda i: (0, 0)),
                  mod_spec(3), mod_spec(4),
                  pl.BlockSpec((N_EXPERTS, d), lambda i: (0, 0)),
                  pl.BlockSpec((N_EXPERTS, 1), lambda i: (0, 0)),
                  pl.BlockSpec((tm, tm), lambda i: (0, 0))],
        out_specs=[pl.BlockSpec((8, tm), lambda i: (0, i)),
                   pl.BlockSpec((N_EXPERTS, 128), lambda i: (0, 0))],
        out_shape=[jax.ShapeDtypeStruct((8, n_rows), f32),
                   jax.ShapeDtypeStruct((N_EXPERTS, 128), f32)],
        scratch_shapes=[pltpu.VMEM((N_EXPERTS, 128), f32)],
        compiler_params=_cparams("arbitrary"),
        name="moe_router",
    )(x_all, g, mods, mods, rw_t, rb_col, tri)


def _dispatch_kernel(pos_ref, zpos_ref, x_ref, g_ref, sh_ref, sc_ref, xs_ref, buf_ref, zero_ref, sem, zsem):
    i = pl.program_id(0)
    tm = x_ref.shape[0]

    @pl.when(i == 0)
    def _():
        zero_ref[...] = jnp.zeros_like(zero_ref)

        def zero_tile(row):
            cp = pltpu.make_async_copy(zero_ref, xs_ref.at[pl.ds(pl.multiple_of(row, 8), tm)], zsem)
            cp.start()
            cp.wait()

        for e in range(N_EXPERTS):
            zero_tile((zpos_ref[e] // 8) * 8)

        def tail(t, c):
            zero_tile(t * tm)
            return c

        lax.fori_loop(zpos_ref[N_EXPERTS], xs_ref.shape[0] // tm, tail, 0)

    buf_ref[...] = _norm_mod(x_ref[...], g_ref[...], sh_ref[...], sc_ref[...])

    def issue(r, c):
        for k in range(2):
            dst = pos_ref[k, i * tm + r]
            pltpu.make_async_copy(buf_ref.at[pl.ds(r, 1)], xs_ref.at[pl.ds(dst, 1)], sem).start()
        return c

    lax.fori_loop(0, tm, issue, 0)
    for k in range(2):
        pltpu.make_async_copy(buf_ref, xs_ref.at[pl.ds(0, tm)], sem).wait()


def _dispatch(pos, zpos, x_all, n_rows, g, mods, tiles_per_mod, p_rows):
    d = x_all.shape[1]
    tm = MOE_TILE
    tpm = tiles_per_mod * (ROW_TILE // tm)
    mod_spec = lambda k: pl.BlockSpec((None, 1, d), lambda i, pos, zpos: (i // tpm, 0, k))
    return pl.pallas_call(
        _dispatch_kernel,
        grid_spec=pltpu.PrefetchScalarGridSpec(
            num_scalar_prefetch=2,
            grid=(n_rows // tm,),
            in_specs=[pl.BlockSpec((tm, d), lambda i, pos, zpos: (i, 0)),
                      pl.BlockSpec((1, d), lambda i, pos, zpos: (0, 0)),
                      mod_spec(3), mod_spec(4)],
            out_specs=pl.BlockSpec(memory_space=pl.ANY),
            scratch_shapes=[pltpu.VMEM((tm, d), f32), pltpu.VMEM((tm, d), f32),
                            pltpu.SemaphoreType.DMA(()), pltpu.SemaphoreType.DMA(())],
        ),
        out_shape=jax.ShapeDtypeStruct((p_rows, d), f32),
        compiler_params=_cparams("arbitrary"),
        name="moe_dispatch",
    )(pos, zpos, x_all, g, mods, mods)


def _expert_kernel(te_ref, nu_ref, xs_ref, wg_ref, wu_ref, wd_ref, y_ref):
    i = pl.program_id(0)

    @pl.when(i < nu_ref[0])
    def _():
        xb = xs_ref[...].astype(bf16)
        a = _dot(xb, wg_ref[...])
        act = (a * jax.nn.sigmoid(a)) * _dot(xb, wu_ref[...])
        y_ref[...] = _dot(act.astype(bf16), wd_ref[...])

    @pl.when(i >= nu_ref[0])
    def _():
        y_ref[...] = jnp.zeros_like(y_ref)


def _expert_ffn(tile_expert, n_used, xs, layer, wg, wu, wd):
    p_rows, d = xs.shape
    de = wg.shape[3]
    tm = MOE_TILE
    return pl.pallas_call(
        _expert_kernel,
        grid_spec=pltpu.PrefetchScalarGridSpec(
            num_scalar_prefetch=2,
            grid=(p_rows // tm,),
            in_specs=[pl.BlockSpec((tm, d), lambda i, te, nu: (jnp.where(i < nu[0], i, 0), 0)),
                      pl.BlockSpec((None, None, d, de), lambda i, te, nu: (layer, te[i], 0, 0)),
                      pl.BlockSpec((None, None, d, de), lambda i, te, nu: (layer, te[i], 0, 0)),
                      pl.BlockSpec((None, None, de, d), lambda i, te, nu: (layer, te[i], 0, 0))],
            out_specs=pl.BlockSpec((tm, d), lambda i, te, nu: (i, 0)),
        ),
        out_shape=jax.ShapeDtypeStruct((p_rows, d), f32),
        compiler_params=_cparams("arbitrary"),
        name="moe_experts",
    )(tile_expert, n_used, xs, wg, wu, wd)


def _combine_kernel(pos_ref, x_ref, w_ref, gate_ref, fg_ref, y_ref, o_ref, y1_ref, y2_ref, sem, *, final_norm):
    i = pl.program_id(0)
    tm = x_ref.shape[0]

    def issue(r, c):
        for k, buf in enumerate((y1_ref, y2_ref)):
            src = pos_ref[k, i * tm + r]
            pltpu.make_async_copy(y_ref.at[pl.ds(src, 1)], buf.at[pl.ds(r, 1)], sem).start()
        return c

    lax.fori_loop(0, tm, issue, 0)
    for buf in (y1_ref, y2_ref):
        pltpu.make_async_copy(y_ref.at[pl.ds(0, tm)], buf, sem).wait()
    w = w_ref[...]
    moe = w[:, 0:1] * y1_ref[...] + w[:, 1:2] * y2_ref[...]
    x = x_ref[...] + gate_ref[...] * moe
    if final_norm:
        x = x * lax.rsqrt(jnp.mean(x * x, axis=-1, keepdims=True) + EPS) * fg_ref[...]
    o_ref[...] = x


def _combine(pos, x_all, n_rows, w_tok, mods, final_g, y, tiles_per_mod, final_norm):
    d = x_all.shape[1]
    tm = MOE_TILE
    tpm = tiles_per_mod * (ROW_TILE // tm)
    kern = functools.partial(_combine_kernel, final_norm=final_norm)
    return pl.pallas_call(
        kern,
        grid_spec=pltpu.PrefetchScalarGridSpec(
            num_scalar_prefetch=1,
            grid=(n_rows // tm,),
            in_specs=[pl.BlockSpec((tm, d), lambda i, pos: (i, 0)),
                      pl.BlockSpec((tm, 2), lambda i, pos: (i, 0)),
                      pl.BlockSpec((None, 1, d), lambda i, pos: (i // tpm, 0, 5)),
                      pl.BlockSpec((1, d), lambda i, pos: (0, 0)),
                      pl.BlockSpec(memory_space=pl.ANY)],
            out_specs=pl.BlockSpec((tm, d), lambda i, pos: (i, 0)),
            scratch_shapes=[pltpu.VMEM((tm, d), f32), pltpu.VMEM((tm, d), f32),
                            pltpu.SemaphoreType.DMA(())],
        ),
        out_shape=jax.ShapeDtypeStruct((n_rows, d), f32),
        compiler_params=_cparams("arbitrary"),
        name="moe_combine",
    )(pos, x_all, w_tok, mods, final_g, y)


def _moe(x_all, n_rows, norm_g, mods, tiles_per_mod, rw_t, rb_col, layer, wg, wu, wd, final_g, final_norm):
    info, counts = _router(x_all, n_rows, norm_g, mods, rw_t, rb_col, tiles_per_mod)
    tm = MOE_TILE
    p_rows = 2 * n_rows + (N_EXPERTS + 1) * tm
    cnt = counts[:, 0].astype(jnp.int32)
    padded = ((cnt + tm - 1) // tm) * tm
    ends = jnp.cumsum(padded)
    starts = ends - padded
    e1 = info[0].astype(jnp.int32)
    e2 = info[1].astype(jnp.int32)
    pos = jnp.stack([starts[e1] + info[4].astype(jnp.int32),
                     starts[e2] + info[5].astype(jnp.int32)])
    n_used = (ends[-1] // tm).astype(jnp.int32)
    tile_start = jnp.arange(p_rows // tm, dtype=jnp.int32) * tm
    tile_expert = jnp.sum((ends[None, :] <= tile_start[:, None]).astype(jnp.int32), axis=1)
    tile_expert = jnp.minimum(tile_expert, N_EXPERTS - 1)
    zpos = jnp.concatenate([starts + cnt, n_used.reshape(1)])
    xs = _dispatch(pos, zpos, x_all, n_rows, norm_g, mods, tiles_per_mod, p_rows)
    y = _expert_ffn(tile_expert, n_used.reshape(1), xs, layer, wg, wu, wd)
    w_tok = jnp.transpose(info[2:4])
    return _combine(pos, x_all, n_rows, w_tok, mods, final_g, y, tiles_per_mod, final_norm)


def _rope_tables(seq_len, n_batch, n_ctx_rows):
    rows = seq_len // GRID_W
    row = jnp.repeat(jnp.arange(rows), GRID_W)
    col = jnp.tile(jnp.arange(GRID_W), rows)
    nf = HEAD_DIM // 4
    inv_freq = ROPE_BASE ** (-jnp.arange(nf, dtype=f32) / nf)
    ang_r = row[:, None].astype(f32) * inv_freq
    ang_c = col[:, None].astype(f32) * inv_freq
    cos64 = jnp.concatenate([jnp.cos(ang_r)] * 2 + [jnp.cos(ang_c)] * 2, axis=-1)
    sin64 = jnp.concatenate([-jnp.sin(ang_r), jnp.sin(ang_r), -jnp.sin(ang_c), jnp.sin(ang_c)], axis=-1)
    cos_t = jnp.tile(jnp.concatenate([cos64, cos64], axis=-1), (n_batch, 1))
    sin_t = jnp.tile(jnp.concatenate([sin64, sin64], axis=-1), (n_batch, 1))
    cos_t = jnp.concatenate([cos_t, jnp.ones((n_ctx_rows, 128), f32)], axis=0)
    sin_t = jnp.concatenate([sin_t, jnp.zeros((n_ctx_rows, 128), f32)], axis=0)
    return cos_t, sin_t


def kernel(x, c, ctx, c_ctx, w_mod, b_mod, norm1_g, norm2_g, final_norm_g, w_in, a_sink,
           p_w, p_scale, c_ln_g, c_ln_b, c_ws, c_bs, g_w2, g_b, g_norm_g,
           w_branch, w_gate, w_out, router_w, router_b, e_gate, e_up, e_down):
    n_batch, seq_len, d = x.shape
    n_ctx = ctx.shape[1]
    depth = w_mod.shape[0]
    n_lat = n_batch * seq_len
    n_all = n_lat + n_batch * n_ctx
    tiles_per_mod = seq_len // ROW_TILE

    cond = jnp.zeros((8, d), f32).at[0:n_batch].set(c).at[n_batch].set(c_ctx)
    mods_all = _modulation(cond, w_mod, b_mod)
    cos_t, sin_t = _rope_tables(seq_len, n_batch, n_batch * n_ctx)
    rw_t = jnp.transpose(router_w)
    rb_col = router_b.reshape(N_EXPERTS, 1)
    final_g = final_norm_g.reshape(1, d)

    w_in_b = jnp.pad(w_in, ((0, 0), (0, 0), (0, 96))).astype(bf16)
    w_gate_b = w_gate.astype(bf16)
    w_branch_b = w_branch.astype(bf16)
    w_out_b = w_out.astype(bf16)
    e_gate_b = e_gate.astype(bf16)
    e_up_b = e_up.astype(bf16)
    e_down_b = e_down.astype(bf16)

    n_lat_tiles = n_lat // ROW_TILE
    xa, xb, b_tile0 = x.reshape(n_lat, d), ctx.reshape(n_batch * n_ctx, d), 0
    for l in range(depth):
        last = l == depth - 1
        n_out = n_lat if last else n_all
        mods = mods_all[l].reshape(8, 1, 6 * d)
        h, qkv, z_pool, z_uv, z_gla = _in_proj(xa, xb, n_lat_tiles, b_tile0, n_all, l,
                                               norm1_g[l].reshape(1, d), mods, cos_t, sin_t,
                                               w_in_b, tiles_per_mod)
        a_br = _window_attention(qkv, a_sink[l], n_batch, seq_len, n_ctx)
        pw = p_w[l].astype(bf16)
        ps = p_scale[l].reshape(1, 512)
        p_br = _multi_scale_pool(z_pool, 0, n_lat, seq_len, ROW_TILE, pw, ps)
        bs_b = jnp.broadcast_to(c_bs[l][:, :, None], (4, SG_CHUNK, 128))
        s_br = _spatial_gate(z_uv, n_out, c_ln_g[l].reshape(1, 512), c_ln_b[l].reshape(1, 512),
                             c_ws[l].astype(bf16), bs_b)
        w2p = jnp.zeros((2, 128, 256), f32)
        w2p = w2p.at[0, 0:GLA_RANK].set(g_w2[l, 0]).at[1, GLA_RANK:2 * GLA_RANK].set(g_w2[l, 1])
        o_f, o_b = _gla_scan(z_gla, n_batch, seq_len, n_ctx, w2p, g_b[l].reshape(2, 1, 256))
        g_br = _gla_output(o_f, o_b, z_gla, n_out, g_norm_g[l].reshape(1, 512))
        if not last:
            a_c = _context_attention(qkv, a_sink[l], n_batch, n_lat, n_ctx)
            p_c = _multi_scale_pool(z_pool, n_lat, n_batch * n_ctx, n_ctx, n_ctx, pw, ps)
            a_br = jnp.concatenate([a_br, a_c], axis=0)
            p_br = jnp.concatenate([p_br, p_c], axis=0)
        acc = _gated_merge(h, (a_br, p_br, s_br, g_br), n_out, l, w_gate_b, w_branch_b)
        x_all = _out_proj_residual(acc, l, w_out_b, xa, xb, n_lat_tiles, b_tile0, mods, n_out,
                                   tiles_per_mod)
        x_all = _moe(x_all, n_out, norm2_g[l].reshape(1, d), mods, tiles_per_mod, rw_t, rb_col,
                     l, e_gate_b, e_up_b, e_down_b, final_g, last)
        xa, xb, b_tile0 = x_all, x_all, n_lat_tiles
    return x_all.reshape(n_batch, seq_len, d)
```

```python
import functools

import numpy as np
import jax
import jax.numpy as jnp
from jax import lax
from jax.experimental import pallas as pl
from jax.experimental.pallas import tpu as pltpu

f32 = jnp.float32
bf16 = jnp.bfloat16

EPS = 1e-6
GRID_W = 64
ROPE_BASE = 10000.0

HEAD_DIM = 64
N_Q_HEADS = 8
N_KV_HEADS = 2
Q_PER_KV = N_Q_HEADS // N_KV_HEADS
ATT_BLOCK = 128
POOL_WINDOWS = (2, 4, 8, 16)
POOL_HALO = 8
SG_CHUNK = 128
GLA_HEADS = 4
GLA_DK = 64
GLA_DV = 128
GLA_RANK = 16
GLA_TAU = 16.0
GLA_CHUNK = 64
N_EXPERTS = 16
EXPERTS_PER_GROUP = 4
N_GROUPS = 4

ROW_TILE = 512
GLA_BLOCK = 256
MOE_TILE = 256
VMEM_LIMIT = 56 * 1024 * 1024


def _cparams(*sem):
    return pltpu.CompilerParams(dimension_semantics=sem, vmem_limit_bytes=VMEM_LIMIT)


def _dot(a, b, dims=(((1,), (0,)), ((), ())), precision=None):
    return lax.dot_general(a, b, dims, precision=precision, preferred_element_type=f32)


_NT = (((1,), (1,)), ((), ()))
_TN = (((0,), (0,)), ((), ()))


def _mod_kernel(c_ref, w_ref, b_ref, o_ref):
    c = c_ref[...]
    a = c * jax.nn.sigmoid(c)
    o_ref[...] = _dot(a, w_ref[...], precision=lax.Precision.HIGHEST) + b_ref[...]


def _modulation(cond, w_mod, b_mod):
    n_layers, d, d6 = w_mod.shape
    tn = 1024
    return pl.pallas_call(
        _mod_kernel,
        grid=(n_layers, d6 // tn),
        in_specs=[
            pl.BlockSpec((8, d), lambda l, j: (0, 0)),
            pl.BlockSpec((None, d, tn), lambda l, j: (l, 0, j)),
            pl.BlockSpec((None, 1, tn), lambda l, j: (l, 0, j)),
        ],
        out_specs=pl.BlockSpec((None, 8, tn), lambda l, j: (l, 0, j)),
        out_shape=jax.ShapeDtypeStruct((n_layers, 8, d6), f32),
        compiler_params=_cparams("arbitrary", "arbitrary"),
        name="modulation",
    )(cond, w_mod, b_mod.reshape(n_layers, 1, d6))


def _norm_mod(x, g, shift, scale):
    y = x * lax.rsqrt(jnp.mean(x * x, axis=-1, keepdims=True) + EPS)
    return (y * g) * (1.0 + scale) + shift


def _two_source_rows(xa_ref, xb_ref, n_a_tiles):
    return jnp.where(pl.program_id(0) < n_a_tiles, xa_ref[...], xb_ref[...])


def _two_source_specs(tm, d, n_a_tiles, b_tile0):
    return [pl.BlockSpec((tm, d), lambda i: (jnp.minimum(i, n_a_tiles - 1), 0)),
            pl.BlockSpec((tm, d), lambda i: (b_tile0 + jnp.maximum(i - n_a_tiles, 0), 0),
                         pipeline_mode=pl.Buffered(1))]


def _in_proj_kernel(xa_ref, xb_ref, g_ref, sh_ref, sc_ref, cos_ref, sin_ref, w_ref,
                    w2_ref, gb_ref, h_ref, qkv_ref, pool_ref, uv_ref, gla_ref, gate_ref, *, n_a_tiles):
    x = _two_source_rows(xa_ref, xb_ref, n_a_tiles)
    hb = _norm_mod(x, g_ref[...], sh_ref[...], sc_ref[...]).astype(bf16)
    h_ref[...] = hb

    cos = cos_ref[...]
    sin = sin_ref[...]
    lane = lax.broadcasted_iota(jnp.int32, cos.shape, 1)
    first_half = (lane % 32) < 16

    def rope(z):
        rot = jnp.where(first_half, pltpu.roll(z, 112, 1), pltpu.roll(z, 16, 1))
        return z * cos + rot * sin

    zq = _dot(hb, w_ref[:, 0:512])
    for c in range(4):
        qkv_ref[:, c * 128:(c + 1) * 128] = (
            rope(zq[:, c * 128:(c + 1) * 128]) * (HEAD_DIM ** -0.5)).astype(bf16)
    zkv = _dot(hb, w_ref[:, 512:768])
    qkv_ref[:, 512:640] = rope(zkv[:, 0:128]).astype(bf16)
    qkv_ref[:, 640:768] = zkv[:, 128:256].astype(bf16)
    pool_ref[...] = _dot(hb, w_ref[:, 768:1280])
    uv_ref[...] = _dot(hb, w_ref[:, 1280:2304])
    gla_ref[...] = _dot(hb, w_ref[:, 2304:3840])
    low_rank = _dot(hb, w_ref[:, 3840:3968])
    logit = _dot(low_rank, w2_ref[...], precision=lax.Precision.HIGHEST) + gb_ref[...]
    gate_ref[...] = _log_sigmoid(logit) / GLA_TAU


def _log_sigmoid(x):
    return jnp.minimum(x, 0.0) - jnp.log(1.0 + jnp.exp(-jnp.abs(x)))


def _in_proj(xa, xb, n_a_tiles, b_tile0, rows, layer, g, mods, cos_t, sin_t, w_pad, w2p, gbias,
             tiles_per_mod):
    d = xa.shape[1]
    n_w = w_pad.shape[2]
    tm = ROW_TILE
    mod_spec = lambda k: pl.BlockSpec((None, 1, d), lambda i: (i // tiles_per_mod, 0, k))
    row_spec = lambda w: pl.BlockSpec((tm, w), lambda i: (i, 0))
    return pl.pallas_call(
        functools.partial(_in_proj_kernel, n_a_tiles=n_a_tiles),
        grid=(rows // tm,),
        in_specs=_two_source_specs(tm, d, n_a_tiles, b_tile0) + [
            pl.BlockSpec((1, d), lambda i: (0, 0)),
            mod_spec(0), mod_spec(1),
            row_spec(128), row_spec(128),
            pl.BlockSpec((None, d, n_w), lambda i: (layer, 0, 0), pipeline_mode=pl.Buffered(1)),
            pl.BlockSpec((128, 512), lambda i: (0, 0)),
            pl.BlockSpec((1, 512), lambda i: (0, 0)),
        ],
        out_specs=[row_spec(d), row_spec(768), row_spec(512), row_spec(1024), row_spec(1536),
                   row_spec(512)],
        out_shape=[
            jax.ShapeDtypeStruct((rows, d), bf16),
            jax.ShapeDtypeStruct((rows, 768), bf16),
            jax.ShapeDtypeStruct((rows, 512), f32),
            jax.ShapeDtypeStruct((rows, 1024), f32),
            jax.ShapeDtypeStruct((rows, 1536), f32),
            jax.ShapeDtypeStruct((rows, 512), f32),
        ],
        compiler_params=_cparams("arbitrary"),
        name="in_proj",
    )(xa, xb, g, mods, mods, cos_t, sin_t, w_pad, w2p, gbias)


def _sink_column(sink_ref, kvh, rows_per_head):
    r = lax.broadcasted_iota(jnp.int32, (Q_PER_KV * rows_per_head, 1), 0) // rows_per_head
    col = jnp.full(r.shape, sink_ref[kvh * Q_PER_KV], f32)
    for g in range(1, Q_PER_KV):
        col = jnp.where(r == g, sink_ref[kvh * Q_PER_KV + g], col)
    return col


def _win_attn_kernel(sink_ref, main_ref, prev_ref, next_ref, ctx_ref, o_ref, *, tiles_per_seq, seq_len):
    i = pl.program_id(0)
    tq = main_ref.shape[0]
    n_sub = tq // ATT_BLOCK
    blk0 = (i % tiles_per_seq) * n_sub
    r = lax.broadcasted_iota(jnp.int32, (Q_PER_KV * ATT_BLOCK, 3 * ATT_BLOCK), 0) % ATT_BLOCK
    j = lax.broadcasted_iota(jnp.int32, (Q_PER_KV * ATT_BLOCK, 3 * ATT_BLOCK), 1)
    rel = j - ATT_BLOCK - r
    in_window = jnp.abs(rel) <= ATT_BLOCK
    n_ctx = ctx_ref.shape[0]
    for kvh in range(N_KV_HEADS):
        kc = 512 + kvh * HEAD_DIM
        vc = 640 + kvh * HEAD_DIM
        k_all = jnp.concatenate([prev_ref[:, kc:kc + HEAD_DIM], main_ref[:, kc:kc + HEAD_DIM],
                                 next_ref[:, kc:kc + HEAD_DIM]], axis=0)
        v_all = jnp.concatenate([prev_ref[:, vc:vc + HEAD_DIM], main_ref[:, vc:vc + HEAD_DIM],
                                 next_ref[:, vc:vc + HEAD_DIM]], axis=0)
        k_ctx = ctx_ref[:, kc:kc + HEAD_DIM]
        v_ctx = ctx_ref[:, vc:vc + HEAD_DIM]
        sink = _sink_column(sink_ref, kvh, ATT_BLOCK)
        for sb in range(n_sub):
            rows = slice(sb * ATT_BLOCK, (sb + 1) * ATT_BLOCK)
            q = jnp.concatenate(
                [main_ref[rows, (kvh * Q_PER_KV + g) * HEAD_DIM:(kvh * Q_PER_KV + g + 1) * HEAD_DIM]
                 for g in range(Q_PER_KV)], axis=0)
            band = slice(sb * ATT_BLOCK, (sb + 3) * ATT_BLOCK)
            s_band = _dot(q, k_all[band], _NT)
            s_ctx = _dot(q, k_ctx, _NT)
            kpos = (blk0 + sb - 1) * ATT_BLOCK + j
            valid = in_window & (kpos >= 0) & (kpos < seq_len)
            s_band = jnp.where(valid, s_band, -1e30)
            m = jnp.maximum(sink, jnp.maximum(jnp.max(s_band, axis=-1, keepdims=True),
                                              jnp.max(s_ctx, axis=-1, keepdims=True)))
            p_band = jnp.exp(s_band - m)
            p_ctx = jnp.exp(s_ctx - m)
            denom = (jnp.exp(sink - m) + jnp.sum(p_band, axis=-1, keepdims=True)
                     + jnp.sum(p_ctx, axis=-1, keepdims=True))
            o = (_dot(p_band.astype(bf16), v_all[band]) + _dot(p_ctx.astype(bf16), v_ctx)) / denom
            for g in range(Q_PER_KV):
                c0 = (kvh * Q_PER_KV + g) * HEAD_DIM
                o_ref[rows, c0:c0 + HEAD_DIM] = o[g * ATT_BLOCK:(g + 1) * ATT_BLOCK].astype(o_ref.dtype)


def _window_attention(qkv, sink, n_batch, seq_len, n_ctx):
    tq = ROW_TILE
    n_lat = n_batch * seq_len
    tiles_per_seq = seq_len // tq
    sub = tq // ATT_BLOCK
    n_blocks = n_lat // ATT_BLOCK
    ctx_blk0 = n_lat // n_ctx
    w = qkv.shape[1]
    kern = functools.partial(_win_attn_kernel, tiles_per_seq=tiles_per_seq, seq_len=seq_len)
    return pl.pallas_call(
        kern,
        grid=(n_lat // tq,),
        in_specs=[
            pl.BlockSpec(memory_space=pltpu.SMEM),
            pl.BlockSpec((tq, w), lambda i: (i, 0)),
            pl.BlockSpec((ATT_BLOCK, w), lambda i: (jnp.maximum(i * sub - 1, 0), 0)),
            pl.BlockSpec((ATT_BLOCK, w), lambda i: (jnp.minimum(i * sub + sub, n_blocks - 1), 0)),
            pl.BlockSpec((n_ctx, w), lambda i: (ctx_blk0 + i // tiles_per_seq, 0)),
        ],
        out_specs=pl.BlockSpec((tq, 512), lambda i: (i, 0)),
        out_shape=jax.ShapeDtypeStruct((n_lat, 512), bf16),
        compiler_params=_cparams("arbitrary"),
        name="window_attention",
    )(sink, qkv, qkv, qkv, qkv)


def _ctx_attn_kernel(sink_ref, qkv_ref, o_ref):
    n = qkv_ref.shape[0]
    for kvh in range(N_KV_HEADS):
        kc = 512 + kvh * HEAD_DIM
        vc = 640 + kvh * HEAD_DIM
        q = jnp.concatenate(
            [qkv_ref[:, (kvh * Q_PER_KV + g) * HEAD_DIM:(kvh * Q_PER_KV + g + 1) * HEAD_DIM]
             for g in range(Q_PER_KV)], axis=0)
        s = _dot(q, qkv_ref[:, kc:kc + HEAD_DIM], _NT)
        sink = _sink_column(sink_ref, kvh, n)
        m = jnp.maximum(sink, jnp.max(s, axis=-1, keepdims=True))
        p = jnp.exp(s - m)
        denom = jnp.exp(sink - m) + jnp.sum(p, axis=-1, keepdims=True)
        o = _dot(p.astype(bf16), qkv_ref[:, vc:vc + HEAD_DIM]) / denom
        for g in range(Q_PER_KV):
            c0 = (kvh * Q_PER_KV + g) * HEAD_DIM
            o_ref[:, c0:c0 + HEAD_DIM] = o[g * n:(g + 1) * n].astype(o_ref.dtype)


def _context_attention(qkv, sink, n_batch, n_lat, n_ctx):
    w = qkv.shape[1]
    blk0 = n_lat // n_ctx
    return pl.pallas_call(
        _ctx_attn_kernel,
        grid=(n_batch,),
        in_specs=[pl.BlockSpec(memory_space=pltpu.SMEM),
                  pl.BlockSpec((n_ctx, w), lambda b: (blk0 + b, 0))],
        out_specs=pl.BlockSpec((n_ctx, 512), lambda b: (b, 0)),
        out_shape=jax.ShapeDtypeStruct((n_batch * n_ctx, 512), bf16),
        compiler_params=_cparams("arbitrary"),
        name="context_attention",
    )(sink, qkv)


def _pool_kernel(main_ref, prev_ref, next_ref, pw_ref, ps_ref, o_ref, xe_ref, *, tiles_per_seq, seq_len):
    i = pl.program_id(0)
    tp = main_ref.shape[0]
    t_in_seq = i % tiles_per_seq
    h = POOL_HALO
    xe_ref[0:h, :] = jnp.where(t_in_seq == 0, 0.0, prev_ref[...])
    xe_ref[h:h + tp, :] = main_ref[...]
    xe_ref[h + tp:2 * h + tp, :] = jnp.where(t_in_seq == tiles_per_seq - 1, 0.0, next_ref[...])
    pos = t_in_seq * tp + lax.broadcasted_iota(jnp.int32, (tp, 1), 0)
    for gi, w in enumerate(POOL_WINDOWS):
        cols = slice(gi * 128, (gi + 1) * 128)
        acc = xe_ref[h - w // 2:h - w // 2 + tp, cols]
        for u in range(-w // 2 + 1, w // 2):
            acc = acc + xe_ref[h + u:h + u + tp, cols]
        lo = jnp.maximum(pos - w // 2, 0)
        hi = jnp.minimum(pos + w // 2, seq_len)
        cnt = (hi - lo).astype(f32)
        pooled = acc / cnt - main_ref[:, cols]
        y = _dot(pooled.astype(bf16), pw_ref[gi])
        o_ref[:, cols] = (y * ps_ref[:, cols]).astype(o_ref.dtype)


def _multi_scale_pool(z, row0, n_rows, seq_len, tp, p_w, p_scale):
    tiles_per_seq = seq_len // tp
    t0 = row0 // tp
    h0 = row0 // POOL_HALO
    hb = tp // POOL_HALO
    n_halo = n_rows // POOL_HALO
    kern = functools.partial(_pool_kernel, tiles_per_seq=tiles_per_seq, seq_len=seq_len)
    return pl.pallas_call(
        kern,
        grid=(n_rows // tp,),
        in_specs=[
            pl.BlockSpec((tp, 512), lambda i: (t0 + i, 0)),
            pl.BlockSpec((POOL_HALO, 512), lambda i: (h0 + jnp.maximum(i * hb - 1, 0), 0)),
            pl.BlockSpec((POOL_HALO, 512), lambda i: (h0 + jnp.minimum((i + 1) * hb, n_halo - 1), 0)),
            pl.BlockSpec((4, 128, 128), lambda i: (0, 0, 0)),
            pl.BlockSpec((1, 512), lambda i: (0, 0)),
        ],
        out_specs=pl.BlockSpec((tp, 512), lambda i: (i, 0)),
        out_shape=jax.ShapeDtypeStruct((n_rows, 512), bf16),
        scratch_shapes=[pltpu.VMEM((tp + 2 * POOL_HALO, 512), f32)],
        compiler_params=_cparams("arbitrary"),
        name="multi_scale_pool",
    )(z, z, z, p_w, p_scale)


def _spatial_gate_kernel(uv_ref, lg_ref, lb_ref, ws_ref, bs_ref, o_ref):
    a = jax.nn.gelu(uv_ref[...])
    u = a[:, 0:512]
    v = a[:, 512:1024]
    mu = jnp.mean(v, axis=-1, keepdims=True)
    var = jnp.mean(jnp.square(v - mu), axis=-1, keepdims=True)
    vn = ((v - mu) * lax.rsqrt(var + EPS) * lg_ref[...] + lb_ref[...]).astype(bf16)
    for c in range(uv_ref.shape[0] // SG_CHUNK):
        rows = slice(c * SG_CHUNK, (c + 1) * SG_CHUNK)
        for g in range(4):
            cols = slice(g * 128, (g + 1) * 128)
            mixed = _dot(ws_ref[g], vn[rows, cols]) + bs_ref[g]
            o_ref[rows, cols] = (u[rows, cols] * mixed).astype(o_ref.dtype)


def _spatial_gate(uv, n_rows, ln_g, ln_b, ws, bs_b):
    tm = ROW_TILE
    return pl.pallas_call(
        _spatial_gate_kernel,
        grid=(n_rows // tm,),
        in_specs=[
            pl.BlockSpec((tm, 1024), lambda i: (i, 0)),
            pl.BlockSpec((1, 512), lambda i: (0, 0)),
            pl.BlockSpec((1, 512), lambda i: (0, 0)),
            pl.BlockSpec((4, 128, 128), lambda i: (0, 0, 0)),
            pl.BlockSpec((4, 128, 128), lambda i: (0, 0, 0)),
        ],
        out_specs=pl.BlockSpec((tm, 512), lambda i: (i, 0)),
        out_shape=jax.ShapeDtypeStruct((n_rows, 512), bf16),
        compiler_params=_cparams("arbitrary"),
        name="spatial_gate",
    )(uv, ln_g, ln_b, ws, bs_b)


GLA_SEGMENTS = (1, 2, 4, 8, 16, 32, 64)


def _gla_tables():
    c = GLA_CHUNK
    i = np.arange(c)[:, None]
    j = np.arange(c)[None, :]
    tri = (j <= i).astype(np.float32)
    mask = np.zeros((2, 7, c, c), np.float32)
    mask[:, 0] = np.eye(c)
    for lv, s in enumerate(GLA_SEGMENTS[:-1]):
        m = ((i // (2 * s)) == (j // (2 * s))) & ((i // s) % 2 == 1) & ((j // s) % 2 == 0)
        mask[0, 1 + lv] = m
        mask[1, 1 + lv] = m.T
    mask = np.tile(mask, (1, 1, 1, GLA_HEADS))
    r = np.arange(GLA_HEADS * GLA_DV)[:, None] // GLA_DV
    cc = np.arange(GLA_HEADS * GLA_DK)[None, :] // GLA_DK
    bd = (r == cc).astype(np.float32)
    return tri, mask, bd


def _segment_sums(g, cum):
    c = GLA_CHUNK
    row = lax.broadcasted_iota(jnp.int32, (c, 1), 0)
    zero = jnp.zeros_like(g)
    before = [None] + [pltpu.roll(g, kk, 0) for kk in (1, 2, 3)]
    after = [None] + [pltpu.roll(g, c - kk, 0) for kk in (1, 2, 3)]
    a, r = {1: g}, {1: zero}
    for s in (2, 4):
        pos = row % s
        a_s, r_s = g, zero
        for kk in range(1, s):
            a_s = a_s + jnp.where(pos >= kk, before[kk], 0.0)
            r_s = r_s + jnp.where(pos < s - kk, after[kk], 0.0)
        a[s], r[s] = a_s, r_s
    blocks = [cum[8 * b:8 * b + 8] for b in range(c // 8)]
    last = [cum[8 * b + 7:8 * b + 8] for b in range(c // 8)]
    for s in (8, 16, 32, 64):
        nb = s // 8
        a_blk, r_blk = [], []
        for b in range(c // 8):
            prev_end = (b // nb) * nb - 1
            a_blk.append(blocks[b] - last[prev_end] if prev_end >= 0 else blocks[b])
            r_blk.append(last[(b // nb + 1) * nb - 1] - blocks[b])
        a[s] = jnp.concatenate(a_blk, axis=0)
        r[s] = jnp.concatenate(r_blk, axis=0)
    return a, r


def _gla_chunk(d, q, k, v, g, tri_ref, mask_ref, bd_ref, st_ref):
    c = GLA_CHUNK
    w = g.shape[1]
    g_hi = g.astype(bf16)
    r1 = g - g_hi.astype(f32)
    g_mid = r1.astype(bf16)
    g_lo = (r1 - g_mid.astype(f32)).astype(bf16)
    cum3 = _dot(tri_ref[...], jnp.concatenate([g_hi, g_mid, g_lo], axis=1))
    cum = cum3[:, 0:w] + cum3[:, w:2 * w] + cum3[:, 2 * w:3 * w]
    a, r = _segment_sums(g, cum)
    if d == 0:
        cq = [a[s] for s in GLA_SEGMENTS]
        ck = [r[s] for s in GLA_SEGMENTS]
    else:
        cq = [r[s] + g for s in GLA_SEGMENTS]
        ck = [a[s] - g for s in GLA_SEGMENTS]
    total = cum[c - 1:c]

    q = q * (GLA_DK ** -0.5)
    head_of_lane = lax.broadcasted_iota(jnp.int32, (1, w), 1) // GLA_DK
    vhead_of_lane = lax.broadcasted_iota(jnp.int32, (1, v.shape[1]), 1) // GLA_DV

    def stack_heads(x, lane_head):
        return jnp.concatenate([jnp.where(lane_head == h, x, jnp.zeros_like(x))
                                for h in range(GLA_HEADS)], axis=0)

    kb = k.astype(bf16)
    k_plain = stack_heads(kb, head_of_lane)
    q01 = jnp.concatenate([q.astype(bf16), (q * jnp.exp(cq[0])).astype(bf16)], axis=0)
    s01 = _dot(q01, k_plain, _NT)
    att = mask_ref[d, 0] * s01[0:c] + mask_ref[d, 1] * s01[c:2 * c]
    for lv in range(1, 6):
        qs = (q * jnp.exp(cq[lv])).astype(bf16)
        ks = stack_heads((k * jnp.exp(ck[lv])).astype(bf16), head_of_lane)
        att = att + mask_ref[d, 1 + lv] * _dot(qs, ks, _NT)
    vb = v.astype(bf16)
    st = st_ref[d]
    q_state = (q * jnp.exp(cq[6])).astype(bf16)
    o = _dot(att.astype(bf16), stack_heads(vb, vhead_of_lane)) + _dot(q_state, st.astype(bf16), _NT)
    k_state = (k * jnp.exp(ck[6])).astype(bf16)
    st_ref[d] = st * jnp.exp(total) + bd_ref[...] * _dot(vb, k_state, _TN)
    return o


def _gla_kernel(qf_ref, kf_ref, vf_ref, gf_ref, qb_ref, kb_ref, vb_ref, gb_ref,
                tri_ref, mask_ref, bd_ref, of_ref, ob_ref, st_ref):
    @pl.when(pl.program_id(1) == 0)
    def _():
        st_ref[...] = jnp.zeros_like(st_ref)

    n_chunks = qf_ref.shape[0] // GLA_CHUNK
    for ci in range(n_chunks):
        rf = slice(ci * GLA_CHUNK, (ci + 1) * GLA_CHUNK)
        of_ref[rf, :] = _gla_chunk(0, qf_ref[rf, :], kf_ref[rf, :], vf_ref[rf, :], gf_ref[rf, :],
                                   tri_ref, mask_ref, bd_ref, st_ref)
        cb = n_chunks - 1 - ci
        rb = slice(cb * GLA_CHUNK, (cb + 1) * GLA_CHUNK)
        ob_ref[rb, :] = _gla_chunk(1, qb_ref[rb, :], kb_ref[rb, :], vb_ref[rb, :], gb_ref[rb, :],
                                   tri_ref, mask_ref, bd_ref, st_ref)


def _gla_scan(z_gla, gates, n_batch, seq_len, n_ctx):
    rows = z_gla.shape[0]
    tb = GLA_BLOCK
    assert n_ctx == tb
    lat_blocks = seq_len // tb
    ctx_blk0 = n_batch * lat_blocks
    tri, mask, bd = _gla_tables()

    def fwd_row(b, s):
        return jnp.where(s == 0, ctx_blk0 + b, b * lat_blocks + s - 1)

    def bwd_row(b, s):
        return jnp.where(s == 0, ctx_blk0 + b, b * lat_blocks + lat_blocks - s)

    def specs(row, direction):
        return [
            pl.BlockSpec((tb, 256), lambda b, s: (row(b, s), 0)),
            pl.BlockSpec((tb, 256), lambda b, s: (row(b, s), 1)),
            pl.BlockSpec((tb, 512), lambda b, s: (row(b, s), 1)),
            pl.BlockSpec((tb, 256), lambda b, s: (row(b, s), direction)),
        ]

    const = lambda shape: pl.BlockSpec(shape, lambda b, s: (0,) * len(shape))
    return pl.pallas_call(
        _gla_kernel,
        grid=(n_batch, 1 + lat_blocks),
        in_specs=specs(fwd_row, 0) + specs(bwd_row, 1) + [
            const(tri.shape), const(mask.shape), const(bd.shape)],
        out_specs=[pl.BlockSpec((tb, 512), lambda b, s: (fwd_row(b, s), 0)),
                   pl.BlockSpec((tb, 512), lambda b, s: (bwd_row(b, s), 0))],
        out_shape=[jax.ShapeDtypeStruct((rows, 512), f32)] * 2,
        scratch_shapes=[pltpu.VMEM((2, GLA_HEADS * GLA_DV, GLA_HEADS * GLA_DK), f32)],
        compiler_params=_cparams("arbitrary", "arbitrary"),
        name="gla_scan",
    )(z_gla, z_gla, z_gla, gates, z_gla, z_gla, z_gla, gates,
      jnp.asarray(tri, bf16), jnp.asarray(mask, f32), jnp.asarray(bd, f32))


def _gla_out_kernel(of_ref, ob_ref, r_ref, ng_ref, o_ref):
    o = of_ref[...] + ob_ref[...]
    r = r_ref[...]
    gate = r * jax.nn.sigmoid(r)
    for h in range(GLA_HEADS):
        cols = slice(h * GLA_DV, (h + 1) * GLA_DV)
        oh = o[:, cols]
        oh = oh * lax.rsqrt(jnp.mean(oh * oh, axis=-1, keepdims=True) + EPS)
        o_ref[:, cols] = (oh * ng_ref[:, cols] * gate[:, cols]).astype(o_ref.dtype)


def _gla_output(o_f, o_b, z_gla, n_rows, norm_g):
    tm = ROW_TILE
    return pl.pallas_call(
        _gla_out_kernel,
        grid=(n_rows // tm,),
        in_specs=[
            pl.BlockSpec((tm, 512), lambda i: (i, 0)),
            pl.BlockSpec((tm, 512), lambda i: (i, 0)),
            pl.BlockSpec((tm, 512), lambda i: (i, 2)),
            pl.BlockSpec((1, 512), lambda i: (0, 0)),
        ],
        out_specs=pl.BlockSpec((tm, 512), lambda i: (i, 0)),
        out_shape=jax.ShapeDtypeStruct((n_rows, 512), bf16),
        compiler_params=_cparams("arbitrary"),
        name="gla_output",
    )(o_f, o_b, z_gla, norm_g)


def _merge_kernel(h_ref, a_ref, p_ref, s_ref, g_ref, wg_ref, wb_ref, o_ref):
    hb = h_ref[...]
    acc = None
    for br, b_ref in enumerate((a_ref, p_ref, s_ref, g_ref)):
        gate = jax.nn.sigmoid(_dot(hb, wg_ref[br]))
        term = gate * _dot(b_ref[...], wb_ref[br])
        acc = term if acc is None else acc + term
    o_ref[...] = acc.astype(o_ref.dtype)


def _gated_merge(h, branches, n_rows, layer, wg, wb):
    d = h.shape[1]
    tm, tn = ROW_TILE, 512
    bspec = pl.BlockSpec((tm, 512), lambda i, j: (i, 0))
    return pl.pallas_call(
        _merge_kernel,
        grid=(n_rows // tm, d // tn),
        in_specs=[pl.BlockSpec((tm, d), lambda i, j: (i, 0)), bspec, bspec, bspec, bspec,
                  pl.BlockSpec((None, 4, d, tn), lambda i, j: (layer, 0, 0, j)),
                  pl.BlockSpec((None, 4, 512, tn), lambda i, j: (layer, 0, 0, j))],
        out_specs=pl.BlockSpec((tm, tn), lambda i, j: (i, j)),
        out_shape=jax.ShapeDtypeStruct((n_rows, d), bf16),
        compiler_params=_cparams("arbitrary", "arbitrary"),
        name="gated_merge",
    )(h, *branches, wg, wb)


def _out_proj_kernel(acc_ref, wo_ref, xa_ref, xb_ref, gate_ref, o_ref, *, n_a_tiles):
    x = _two_source_rows(xa_ref, xb_ref, n_a_tiles)
    o_ref[...] = x + gate_ref[...] * _dot(acc_ref[...], wo_ref[...])


def _out_proj_residual(acc, layer, wo, xa, xb, n_a_tiles, b_tile0, mods, n_rows, tiles_per_mod):
    d = acc.shape[1]
    tm = ROW_TILE
    return pl.pallas_call(
        functools.partial(_out_proj_kernel, n_a_tiles=n_a_tiles),
        grid=(n_rows // tm,),
        in_specs=[pl.BlockSpec((tm, d), lambda i: (i, 0)),
                  pl.BlockSpec((None, d, d), lambda i: (layer, 0, 0))]
        + _two_source_specs(tm, d, n_a_tiles, b_tile0)
        + [pl.BlockSpec((None, 1, d), lambda i: (i // tiles_per_mod, 0, 2))],
        out_specs=pl.BlockSpec((tm, d), lambda i: (i, 0)),
        out_shape=jax.ShapeDtypeStruct((n_rows, d), f32),
        compiler_params=_cparams("arbitrary"),
        name="out_proj_residual",
    )(acc, wo, xa, xb, mods)


def _router_kernel(x_ref, g_ref, sh_ref, sc_ref, rw_ref, rb_ref, tri_ref, info_ref, cnt_ref, carry_ref):
    i = pl.program_id(0)

    @pl.when(i == 0)
    def _():
        carry_ref[...] = jnp.zeros_like(carry_ref)

    h2 = _norm_mod(x_ref[...], g_ref[...], sh_ref[...], sc_ref[...])
    tm = h2.shape[0]
    h_hi = h2.astype(bf16)
    h_lo = (h2 - h_hi.astype(f32)).astype(bf16)
    part = _dot(rw_ref[...], h_hi, _NT)
    logits = (part[0:N_EXPERTS] + part[N_EXPERTS:2 * N_EXPERTS]
              + _dot(rw_ref[0:N_EXPERTS, :], h_lo, _NT))
    score = jax.nn.sigmoid(logits)
    biased = score + rb_ref[...]
    b = [biased[e:e + 1] for e in range(N_EXPERTS)]
    sc = [score[e:e + 1] for e in range(N_EXPERTS)]

    def top2_sum(v):
        hi01, lo01 = jnp.maximum(v[0], v[1]), jnp.minimum(v[0], v[1])
        hi23, lo23 = jnp.maximum(v[2], v[3]), jnp.minimum(v[2], v[3])
        return jnp.maximum(hi01, hi23) + jnp.maximum(jnp.minimum(hi01, hi23), jnp.maximum(lo01, lo23))

    best = jnp.zeros((1, tm), jnp.int32)
    best_score = top2_sum(b[0:4])
    for gidx in range(1, N_GROUPS):
        gs = top2_sum(b[4 * gidx:4 * gidx + 4])
        take = gs > best_score
        best = jnp.where(take, gidx, best)
        best_score = jnp.where(take, gs, best_score)
    cb, cs = [], []
    for kk in range(EXPERTS_PER_GROUP):
        vb, vs = b[kk], sc[kk]
        for gidx in range(1, N_GROUPS):
            vb = jnp.where(best == gidx, b[4 * gidx + kk], vb)
            vs = jnp.where(best == gidx, sc[4 * gidx + kk], vs)
        cb.append(vb)
        cs.append(vs)
    i1 = jnp.zeros((1, tm), jnp.int32)
    m1, w1 = cb[0], cs[0]
    for kk in range(1, EXPERTS_PER_GROUP):
        take = cb[kk] > m1
        i1 = jnp.where(take, kk, i1)
        m1 = jnp.where(take, cb[kk], m1)
        w1 = jnp.where(take, cs[kk], w1)
    i2 = jnp.where(i1 == 0, 1, 0).astype(jnp.int32)
    m2 = jnp.where(i1 == 0, cb[1], cb[0])
    w2 = jnp.where(i1 == 0, cs[1], cs[0])
    for kk in range(1, EXPERTS_PER_GROUP):
        take = (cb[kk] > m2) & (i1 != kk) & (i2 != kk)
        i2 = jnp.where(take, kk, i2)
        m2 = jnp.where(take, cb[kk], m2)
        w2 = jnp.where(take, cs[kk], w2)
    e1 = best * EXPERTS_PER_GROUP + i1
    e2 = best * EXPERTS_PER_GROUP + i2
    wsum = w1 + w2
    erow = lax.broadcasted_iota(jnp.int32, (N_EXPERTS, tm), 0)
    oh1 = (erow == e1).astype(f32)
    oh2 = (erow == e2).astype(f32)
    oh = oh1 + oh2
    rank = _dot(oh.astype(bf16), tri_ref[...]) + carry_ref[:, 0:1]
    carry_ref[...] = carry_ref[...] + jnp.sum(oh, axis=1, keepdims=True)
    r1 = jnp.sum(oh1 * rank, axis=0, keepdims=True)
    r2 = jnp.sum(oh2 * rank, axis=0, keepdims=True)
    info_ref[...] = jnp.concatenate(
        [e1.astype(f32), e2.astype(f32), w1 / wsum, w2 / wsum, r1, r2,
         jnp.zeros((2, tm), f32)], axis=0)
    cnt_ref[...] = carry_ref[...]


def _router(x_all, n_rows, g, mods, rw_t, rb_col, tiles_per_mod):
    d = x_all.shape[1]
    tm = ROW_TILE
    tri = jnp.asarray(np.triu(np.ones((tm, tm), np.float32), k=1), bf16)
    mod_spec = lambda k: pl.BlockSpec((None, 1, d), lambda i: (i // tiles_per_mod, 0, k))
    return pl.pallas_call(
        _router_kernel,
        grid=(n_rows // tm,),
        in_specs=[pl.BlockSpec((tm, d), lambda i: (i, 0)),
                  pl.BlockSpec((1, d), lambda i: (0, 0)),
                  mod_spec(3), mod_spec(4),
                  pl.BlockSpec((2 * N_EXPERTS, d), lambda i: (0, 0)),
                  pl.BlockSpec((N_EXPERTS, 1), lambda i: (0, 0)),
                  pl.BlockSpec((tm, tm), lambda i: (0, 0))],
        out_specs=[pl.BlockSpec((8, tm), lambda i: (0, i)),
                   pl.BlockSpec((N_EXPERTS, 128), lambda i: (0, 0))],
        out_shape=[jax.ShapeDtypeStruct((8, n_rows), f32),
                   jax.ShapeDtypeStruct((N_EXPERTS, 128), f32)],
        scratch_shapes=[pltpu.VMEM((N_EXPERTS, 128), f32)],
        compiler_params=_cparams("arbitrary"),
        name="moe_router",
    )(x_all, g, mods, mods, rw_t, rb_col, tri)


def _dispatch_kernel(pos_ref, zpos_ref, x_ref, g_ref, sh_ref, sc_ref, xs_ref, buf_ref, zero_ref, sem, zsem):
    i = pl.program_id(0)
    tm = x_ref.shape[0]

    @pl.when(i == 0)
    def _():
        zero_ref[...] = jnp.zeros_like(zero_ref)

        def zero_tile(row):
            cp = pltpu.make_async_copy(zero_ref, xs_ref.at[pl.ds(pl.multiple_of(row, 8), tm)], zsem)
            cp.start()
            cp.wait()

        for e in range(N_EXPERTS):
            zero_tile((zpos_ref[e] // 8) * 8)

        def tail(t, c):
            zero_tile(t * tm)
            return c

        lax.fori_loop(zpos_ref[N_EXPERTS], xs_ref.shape[0] // tm, tail, 0)

    buf_ref[...] = _norm_mod(x_ref[...], g_ref[...], sh_ref[...], sc_ref[...])

    def issue(r, c):
        for k in range(2):
            dst = pos_ref[k, i * tm + r]
            pltpu.make_async_copy(buf_ref.at[pl.ds(r, 1)], xs_ref.at[pl.ds(dst, 1)], sem).start()
        return c

    lax.fori_loop(0, tm, issue, 0, unroll=8)
    for k in range(2):
        pltpu.make_async_copy(buf_ref, xs_ref.at[pl.ds(0, tm)], sem).wait()


def _dispatch(pos, zpos, x_all, n_rows, g, mods, tiles_per_mod, p_rows):
    d = x_all.shape[1]
    tm = MOE_TILE
    tpm = tiles_per_mod * (ROW_TILE // tm)
    mod_spec = lambda k: pl.BlockSpec((None, 1, d), lambda i, pos, zpos: (i // tpm, 0, k))
    return pl.pallas_call(
        _dispatch_kernel,
        grid_spec=pltpu.PrefetchScalarGridSpec(
            num_scalar_prefetch=2,
            grid=(n_rows // tm,),
            in_specs=[pl.BlockSpec((tm, d), lambda i, pos, zpos: (i, 0)),
                      pl.BlockSpec((1, d), lambda i, pos, zpos: (0, 0)),
                      mod_spec(3), mod_spec(4)],
            out_specs=pl.BlockSpec(memory_space=pl.ANY),
            scratch_shapes=[pltpu.VMEM((tm, d), f32), pltpu.VMEM((tm, d), f32),
                            pltpu.SemaphoreType.DMA(()), pltpu.SemaphoreType.DMA(())],
        ),
        out_shape=jax.ShapeDtypeStruct((p_rows, d), f32),
        compiler_params=_cparams("arbitrary"),
        name="moe_dispatch",
    )(pos, zpos, x_all, g, mods, mods)


def _expert_kernel(te_ref, nu_ref, xs_ref, wg_ref, wu_ref, wd_ref, y_ref):
    i = pl.program_id(0)

    @pl.when(i < nu_ref[0])
    def _():
        xb = xs_ref[...].astype(bf16)
        a = _dot(xb, wg_ref[...])
        act = (a * jax.nn.sigmoid(a)) * _dot(xb, wu_ref[...])
        y_ref[...] = _dot(act.astype(bf16), wd_ref[...])

    @pl.when(i >= nu_ref[0])
    def _():
        y_ref[...] = jnp.zeros_like(y_ref)


def _expert_ffn(tile_expert, n_used, xs, layer, wg, wu, wd):
    p_rows, d = xs.shape
    de = wg.shape[3]
    tm = MOE_TILE
    return pl.pallas_call(
        _expert_kernel,
        grid_spec=pltpu.PrefetchScalarGridSpec(
            num_scalar_prefetch=2,
            grid=(p_rows // tm,),
            in_specs=[pl.BlockSpec((tm, d), lambda i, te, nu: (jnp.where(i < nu[0], i, 0), 0)),
                      pl.BlockSpec((None, None, d, de), lambda i, te, nu: (layer, te[i], 0, 0)),
                      pl.BlockSpec((None, None, d, de), lambda i, te, nu: (layer, te[i], 0, 0)),
                      pl.BlockSpec((None, None, de, d), lambda i, te, nu: (layer, te[i], 0, 0))],
            out_specs=pl.BlockSpec((tm, d), lambda i, te, nu: (i, 0)),
        ),
        out_shape=jax.ShapeDtypeStruct((p_rows, d), f32),
        compiler_params=_cparams("arbitrary"),
        name="moe_experts",
    )(tile_expert, n_used, xs, wg, wu, wd)


def _combine_kernel(pos_ref, x_ref, w_ref, gate_ref, fg_ref, y_ref, o_ref, y1_ref, y2_ref, sem, *, final_norm):
    i = pl.program_id(0)
    tm = x_ref.shape[0]

    def issue(r, c):
        for k, buf in enumerate((y1_ref, y2_ref)):
            src = pos_ref[k, i * tm + r]
            pltpu.make_async_copy(y_ref.at[pl.ds(src, 1)], buf.at[pl.ds(r, 1)], sem).start()
        return c

    lax.fori_loop(0, tm, issue, 0, unroll=8)
    for buf in (y1_ref, y2_ref):
        pltpu.make_async_copy(y_ref.at[pl.ds(0, tm)], buf, sem).wait()
    w = w_ref[...]
    moe = w[:, 0:1] * y1_ref[...] + w[:, 1:2] * y2_ref[...]
    x = x_ref[...] + gate_ref[...] * moe
    if final_norm:
        x = x * lax.rsqrt(jnp.mean(x * x, axis=-1, keepdims=True) + EPS) * fg_ref[...]
    o_ref[...] = x


def _combine(pos, x_all, n_rows, w_tok, mods, final_g, y, tiles_per_mod, final_norm):
    d = x_all.shape[1]
    tm = MOE_TILE
    tpm = tiles_per_mod * (ROW_TILE // tm)
    kern = functools.partial(_combine_kernel, final_norm=final_norm)
    return pl.pallas_call(
        kern,
        grid_spec=pltpu.PrefetchScalarGridSpec(
            num_scalar_prefetch=1,
            grid=(n_rows // tm,),
            in_specs=[pl.BlockSpec((tm, d), lambda i, pos: (i, 0)),
                      pl.BlockSpec((tm, 2), lambda i, pos: (i, 0)),
                      pl.BlockSpec((None, 1, d), lambda i, pos: (i // tpm, 0, 5)),
                      pl.BlockSpec((1, d), lambda i, pos: (0, 0)),
                      pl.BlockSpec(memory_space=pl.ANY)],
            out_specs=pl.BlockSpec((tm, d), lambda i, pos: (i, 0)),
            scratch_shapes=[pltpu.VMEM((tm, d), f32), pltpu.VMEM((tm, d), f32),
                            pltpu.SemaphoreType.DMA(())],
        ),
        out_shape=jax.ShapeDtypeStruct((n_rows, d), f32),
        compiler_params=_cparams("arbitrary"),
        name="moe_combine",
    )(pos, x_all, w_tok, mods, final_g, y)


def _moe(x_all, n_rows, norm_g, mods, tiles_per_mod, rw_t, rb_col, layer, wg, wu, wd, final_g, final_norm):
    info, counts = _router(x_all, n_rows, norm_g, mods, rw_t, rb_col, tiles_per_mod)
    tm = MOE_TILE
    p_rows = 2 * n_rows + (N_EXPERTS + 1) * tm
    cnt = counts[:, 0].astype(jnp.int32)
    padded = ((cnt + tm - 1) // tm) * tm
    ends = jnp.cumsum(padded)
    starts = ends - padded
    e1 = info[0].astype(jnp.int32)
    e2 = info[1].astype(jnp.int32)
    pos = jnp.stack([starts[e1] + info[4].astype(jnp.int32),
                     starts[e2] + info[5].astype(jnp.int32)])
    n_used = (ends[-1] // tm).astype(jnp.int32)
    tile_start = jnp.arange(p_rows // tm, dtype=jnp.int32) * tm
    tile_expert = jnp.sum((ends[None, :] <= tile_start[:, None]).astype(jnp.int32), axis=1)
    tile_expert = jnp.minimum(tile_expert, N_EXPERTS - 1)
    zpos = jnp.concatenate([starts + cnt, n_used.reshape(1)])
    xs = _dispatch(pos, zpos, x_all, n_rows, norm_g, mods, tiles_per_mod, p_rows)
    y = _expert_ffn(tile_expert, n_used.reshape(1), xs, layer, wg, wu, wd)
    w_tok = jnp.transpose(info[2:4])
    return _combine(pos, x_all, n_rows, w_tok, mods, final_g, y, tiles_per_mod, final_norm)


def _rope_tables(seq_len, n_batch, n_ctx_rows):
    rows = seq_len // GRID_W
    row = jnp.repeat(jnp.arange(rows), GRID_W)
    col = jnp.tile(jnp.arange(GRID_W), rows)
    nf = HEAD_DIM // 4
    inv_freq = ROPE_BASE ** (-jnp.arange(nf, dtype=f32) / nf)
    ang_r = row[:, None].astype(f32) * inv_freq
    ang_c = col[:, None].astype(f32) * inv_freq
    cos64 = jnp.concatenate([jnp.cos(ang_r)] * 2 + [jnp.cos(ang_c)] * 2, axis=-1)
    sin64 = jnp.concatenate([-jnp.sin(ang_r), jnp.sin(ang_r), -jnp.sin(ang_c), jnp.sin(ang_c)], axis=-1)
    cos_t = jnp.tile(jnp.concatenate([cos64, cos64], axis=-1), (n_batch, 1))
    sin_t = jnp.tile(jnp.concatenate([sin64, sin64], axis=-1), (n_batch, 1))
    cos_t = jnp.concatenate([cos_t, jnp.ones((n_ctx_rows, 128), f32)], axis=0)
    sin_t = jnp.concatenate([sin_t, jnp.zeros((n_ctx_rows, 128), f32)], axis=0)
    return cos_t, sin_t


def kernel(x, c, ctx, c_ctx, w_mod, b_mod, norm1_g, norm2_g, final_norm_g, w_in, a_sink,
           p_w, p_scale, c_ln_g, c_ln_b, c_ws, c_bs, g_w2, g_b, g_norm_g,
           w_branch, w_gate, w_out, router_w, router_b, e_gate, e_up, e_down):
    n_batch, seq_len, d = x.shape
    n_ctx = ctx.shape[1]
    depth = w_mod.shape[0]
    n_lat = n_batch * seq_len
    n_all = n_lat + n_batch * n_ctx
    tiles_per_mod = seq_len // ROW_TILE

    cond = jnp.zeros((8, d), f32).at[0:n_batch].set(c).at[n_batch].set(c_ctx)
    mods_all = _modulation(cond, w_mod, b_mod)
    cos_t, sin_t = _rope_tables(seq_len, n_batch, n_batch * n_ctx)
    rw_f = jnp.transpose(router_w)
    rw_hi = rw_f.astype(bf16)
    rw_t = jnp.concatenate([rw_hi, (rw_f - rw_hi.astype(f32)).astype(bf16)], axis=0)
    rb_col = router_b.reshape(N_EXPERTS, 1)
    final_g = final_norm_g.reshape(1, d)

    w_in_b = jnp.pad(w_in, ((0, 0), (0, 0), (0, 96))).astype(bf16)
    w_gate_b = w_gate.astype(bf16)
    w_branch_b = w_branch.astype(bf16)
    w_out_b = w_out.astype(bf16)
    e_gate_b = e_gate.astype(bf16)
    e_up_b = e_up.astype(bf16)
    e_down_b = e_down.astype(bf16)

    n_lat_tiles = n_lat // ROW_TILE
    xa, xb, b_tile0 = x.reshape(n_lat, d), ctx.reshape(n_batch * n_ctx, d), 0
    for l in range(depth):
        last = l == depth - 1
        n_out = n_lat if last else n_all
        mods = mods_all[l].reshape(8, 1, 6 * d)
        w2p = jnp.zeros((128, 512), f32)
        w2p = w2p.at[0:GLA_RANK, 0:256].set(g_w2[l, 0]).at[GLA_RANK:2 * GLA_RANK, 256:512].set(g_w2[l, 1])
        h, qkv, z_pool, z_uv, z_gla, gates = _in_proj(
            xa, xb, n_lat_tiles, b_tile0, n_all, l, norm1_g[l].reshape(1, d), mods, cos_t, sin_t,
            w_in_b, w2p, g_b[l].reshape(1, 512), tiles_per_mod)
        a_br = _window_attention(qkv, a_sink[l], n_batch, seq_len, n_ctx)
        pw = p_w[l].astype(bf16)
        ps = p_scale[l].reshape(1, 512)
        p_br = _multi_scale_pool(z_pool, 0, n_lat, seq_len, ROW_TILE, pw, ps)
        bs_b = jnp.broadcast_to(c_bs[l][:, :, None], (4, SG_CHUNK, 128))
        s_br = _spatial_gate(z_uv, n_out, c_ln_g[l].reshape(1, 512), c_ln_b[l].reshape(1, 512),
                             c_ws[l].astype(bf16), bs_b)
        o_f, o_b = _gla_scan(z_gla, gates, n_batch, seq_len, n_ctx)
        g_br = _gla_output(o_f, o_b, z_gla, n_out, g_norm_g[l].reshape(1, 512))
        if not last:
            a_c = _context_attention(qkv, a_sink[l], n_batch, n_lat, n_ctx)
            p_c = _multi_scale_pool(z_pool, n_lat, n_batch * n_ctx, n_ctx, n_ctx, pw, ps)
            a_br = jnp.concatenate([a_br, a_c], axis=0)
            p_br = jnp.concatenate([p_br, p_c], axis=0)
        acc = _gated_merge(h, (a_br, p_br, s_br, g_br), n_out, l, w_gate_b, w_branch_b)
        x_all = _out_proj_residual(acc, l, w_out_b, xa, xb, n_lat_tiles, b_tile0, mods, n_out,
                                   tiles_per_mod)
        x_all = _moe(x_all, n_out, norm2_g[l].reshape(1, d), mods, tiles_per_mod, rw_t, rb_col,
                     l, e_gate_b, e_up_b, e_down_b, final_g, last)
        xa, xb, b_tile0 = x_all, x_all, n_lat_tiles
    return x_all.reshape(n_batch, seq_len, d)
```

```python
import functools

import numpy as np
import jax
import jax.numpy as jnp
from jax import lax
from jax.experimental import pallas as pl
from jax.experimental.pallas import tpu as pltpu

f32 = jnp.float32
bf16 = jnp.bfloat16

EPS = 1e-6
GRID_W = 64
ROPE_BASE = 10000.0

HEAD_DIM = 64
N_Q_HEADS = 8
N_KV_HEADS = 2
Q_PER_KV = N_Q_HEADS // N_KV_HEADS
ATT_BLOCK = 128
POOL_WINDOWS = (2, 4, 8, 16)
POOL_HALO = 8
SG_CHUNK = 128
GLA_HEADS = 4
GLA_DK = 64
GLA_DV = 128
GLA_RANK = 16
GLA_TAU = 16.0
GLA_CHUNK = 64
N_EXPERTS = 16
EXPERTS_PER_GROUP = 4
N_GROUPS = 4

ROW_TILE = 512
GLA_BLOCK = 256
MOE_TILE = 256
VMEM_LIMIT = 56 * 1024 * 1024


def _cparams(*sem):
    return pltpu.CompilerParams(dimension_semantics=sem, vmem_limit_bytes=VMEM_LIMIT)


def _dot(a, b, dims=(((1,), (0,)), ((), ())), precision=None):
    return lax.dot_general(a, b, dims, precision=precision, preferred_element_type=f32)


_NT = (((1,), (1,)), ((), ()))
_TN = (((0,), (0,)), ((), ()))


def _mod_kernel(c_ref, w_ref, b_ref, o_ref):
    c = c_ref[...]
    a = c * jax.nn.sigmoid(c)
    o_ref[...] = _dot(a, w_ref[...], precision=lax.Precision.HIGHEST) + b_ref[...]


def _modulation(cond, w_mod, b_mod):
    n_layers, d, d6 = w_mod.shape
    tn = 1024
    return pl.pallas_call(
        _mod_kernel,
        grid=(n_layers, d6 // tn),
        in_specs=[
            pl.BlockSpec((8, d), lambda l, j: (0, 0)),
            pl.BlockSpec((None, d, tn), lambda l, j: (l, 0, j)),
            pl.BlockSpec((None, 1, tn), lambda l, j: (l, 0, j)),
        ],
        out_specs=pl.BlockSpec((None, 8, tn), lambda l, j: (l, 0, j)),
        out_shape=jax.ShapeDtypeStruct((n_layers, 8, d6), f32),
        compiler_params=_cparams("arbitrary", "arbitrary"),
        name="modulation",
    )(cond, w_mod, b_mod.reshape(n_layers, 1, d6))


def _norm_mod(x, g, shift, scale):
    y = x * lax.rsqrt(jnp.mean(x * x, axis=-1, keepdims=True) + EPS)
    return (y * g) * (1.0 + scale) + shift


def _two_source_rows(xa_ref, xb_ref, n_a_tiles):
    return jnp.where(pl.program_id(0) < n_a_tiles, xa_ref[...], xb_ref[...])


def _two_source_specs(tm, d, n_a_tiles, b_tile0):
    return [pl.BlockSpec((tm, d), lambda i: (jnp.minimum(i, n_a_tiles - 1), 0)),
            pl.BlockSpec((tm, d), lambda i: (b_tile0 + jnp.maximum(i - n_a_tiles, 0), 0),
                         pipeline_mode=pl.Buffered(1))]


def _spatial_gate_rows(uv, lg_ref, lb_ref, ws_ref, bs_ref, o_ref):
    a = jax.nn.gelu(uv)
    u = a[:, 0:512]
    v = a[:, 512:1024]
    mu = jnp.mean(v, axis=-1, keepdims=True)
    var = jnp.mean(jnp.square(v - mu), axis=-1, keepdims=True)
    vn = ((v - mu) * lax.rsqrt(var + EPS) * lg_ref[...] + lb_ref[...]).astype(bf16)
    for c in range(uv.shape[0] // SG_CHUNK):
        rows = slice(c * SG_CHUNK, (c + 1) * SG_CHUNK)
        for g in range(4):
            cols = slice(g * 128, (g + 1) * 128)
            mixed = _dot(ws_ref[g], vn[rows, cols]) + bs_ref[g]
            o_ref[rows, cols] = (u[rows, cols] * mixed).astype(o_ref.dtype)


def _in_proj_kernel(xa_ref, xb_ref, g_ref, sh_ref, sc_ref, cos_ref, sin_ref, w_ref,
                    w2_ref, gb_ref, lg_ref, lb_ref, ws_ref, bs_ref,
                    h_ref, qkv_ref, pool_ref, sg_ref, gla_ref, gate_ref, *, n_a_tiles):
    x = _two_source_rows(xa_ref, xb_ref, n_a_tiles)
    hb = _norm_mod(x, g_ref[...], sh_ref[...], sc_ref[...]).astype(bf16)
    h_ref[...] = hb

    cos = cos_ref[...]
    sin = sin_ref[...]
    lane = lax.broadcasted_iota(jnp.int32, cos.shape, 1)
    first_half = (lane % 32) < 16

    def rope(z):
        rot = jnp.where(first_half, pltpu.roll(z, 112, 1), pltpu.roll(z, 16, 1))
        return z * cos + rot * sin

    zq = _dot(hb, w_ref[:, 0:512])
    for c in range(4):
        qkv_ref[:, c * 128:(c + 1) * 128] = (
            rope(zq[:, c * 128:(c + 1) * 128]) * (HEAD_DIM ** -0.5)).astype(bf16)
    zkv = _dot(hb, w_ref[:, 512:768])
    qkv_ref[:, 512:640] = rope(zkv[:, 0:128]).astype(bf16)
    qkv_ref[:, 640:768] = zkv[:, 128:256].astype(bf16)
    pool_ref[...] = _dot(hb, w_ref[:, 768:1280])
    _spatial_gate_rows(_dot(hb, w_ref[:, 1280:2304]), lg_ref, lb_ref, ws_ref, bs_ref, sg_ref)
    gla_ref[...] = _dot(hb, w_ref[:, 2304:3840])
    low_rank = _dot(hb, w_ref[:, 3840:3968])
    lr_hi = low_rank.astype(bf16)
    lr_lo = (low_rank - lr_hi.astype(f32)).astype(bf16)
    logit = _dot(jnp.concatenate([lr_hi, lr_lo, lr_hi], axis=1), w2_ref[...]) + gb_ref[...]
    gate_ref[...] = _log_sigmoid(logit) / GLA_TAU


def _log_sigmoid(x):
    return jnp.minimum(x, 0.0) - jnp.log(1.0 + jnp.exp(-jnp.abs(x)))


def _in_proj(xa, xb, n_a_tiles, b_tile0, rows, layer, g, mods, cos_t, sin_t, w_pad, w2p, gbias,
             ln_g, ln_b, ws, bs_b, tiles_per_mod):
    d = xa.shape[1]
    n_w = w_pad.shape[2]
    tm = ROW_TILE
    mod_spec = lambda k: pl.BlockSpec((None, 1, d), lambda i: (i // tiles_per_mod, 0, k))
    row_spec = lambda w: pl.BlockSpec((tm, w), lambda i: (i, 0))
    return pl.pallas_call(
        functools.partial(_in_proj_kernel, n_a_tiles=n_a_tiles),
        grid=(rows // tm,),
        in_specs=_two_source_specs(tm, d, n_a_tiles, b_tile0) + [
            pl.BlockSpec((1, d), lambda i: (0, 0)),
            mod_spec(0), mod_spec(1),
            row_spec(128), row_spec(128),
            pl.BlockSpec((None, d, n_w), lambda i: (layer, 0, 0), pipeline_mode=pl.Buffered(1)),
            pl.BlockSpec((384, 512), lambda i: (0, 0)),
            pl.BlockSpec((1, 512), lambda i: (0, 0)),
            pl.BlockSpec((1, 512), lambda i: (0, 0)),
            pl.BlockSpec((1, 512), lambda i: (0, 0)),
            pl.BlockSpec((4, 128, 128), lambda i: (0, 0, 0)),
            pl.BlockSpec((4, 128, 128), lambda i: (0, 0, 0)),
        ],
        out_specs=[row_spec(d), row_spec(768), row_spec(512), row_spec(512), row_spec(1536),
                   row_spec(512)],
        out_shape=[
            jax.ShapeDtypeStruct((rows, d), bf16),
            jax.ShapeDtypeStruct((rows, 768), bf16),
            jax.ShapeDtypeStruct((rows, 512), f32),
            jax.ShapeDtypeStruct((rows, 512), bf16),
            jax.ShapeDtypeStruct((rows, 1536), f32),
            jax.ShapeDtypeStruct((rows, 512), f32),
        ],
        compiler_params=_cparams("arbitrary"),
        name="in_proj",
    )(xa, xb, g, mods, mods, cos_t, sin_t, w_pad, w2p, gbias, ln_g, ln_b, ws, bs_b)


def _sink_column(sink_ref, kvh, rows_per_head):
    r = lax.broadcasted_iota(jnp.int32, (Q_PER_KV * rows_per_head, 1), 0) // rows_per_head
    col = jnp.full(r.shape, sink_ref[kvh * Q_PER_KV], f32)
    for g in range(1, Q_PER_KV):
        col = jnp.where(r == g, sink_ref[kvh * Q_PER_KV + g], col)
    return col


def _win_attn_kernel(sink_ref, main_ref, prev_ref, next_ref, ctx_ref, o_ref, *, tiles_per_seq, seq_len):
    i = pl.program_id(0)
    tq = main_ref.shape[0]
    n_sub = tq // ATT_BLOCK
    blk0 = (i % tiles_per_seq) * n_sub
    r = lax.broadcasted_iota(jnp.int32, (Q_PER_KV * ATT_BLOCK, 3 * ATT_BLOCK), 0) % ATT_BLOCK
    j = lax.broadcasted_iota(jnp.int32, (Q_PER_KV * ATT_BLOCK, 3 * ATT_BLOCK), 1)
    rel = j - ATT_BLOCK - r
    in_window = jnp.abs(rel) <= ATT_BLOCK
    n_ctx = ctx_ref.shape[0]
    for kvh in range(N_KV_HEADS):
        kc = 512 + kvh * HEAD_DIM
        vc = 640 + kvh * HEAD_DIM
        k_all = jnp.concatenate([prev_ref[:, kc:kc + HEAD_DIM], main_ref[:, kc:kc + HEAD_DIM],
                                 next_ref[:, kc:kc + HEAD_DIM]], axis=0)
        v_all = jnp.concatenate([prev_ref[:, vc:vc + HEAD_DIM], main_ref[:, vc:vc + HEAD_DIM],
                                 next_ref[:, vc:vc + HEAD_DIM]], axis=0)
        k_ctx = ctx_ref[:, kc:kc + HEAD_DIM]
        v_ctx = ctx_ref[:, vc:vc + HEAD_DIM]
        sink = _sink_column(sink_ref, kvh, ATT_BLOCK)
        for sb in range(n_sub):
            rows = slice(sb * ATT_BLOCK, (sb + 1) * ATT_BLOCK)
            q = jnp.concatenate(
                [main_ref[rows, (kvh * Q_PER_KV + g) * HEAD_DIM:(kvh * Q_PER_KV + g + 1) * HEAD_DIM]
                 for g in range(Q_PER_KV)], axis=0)
            band = slice(sb * ATT_BLOCK, (sb + 3) * ATT_BLOCK)
            s_band = _dot(q, k_all[band], _NT)
            s_ctx = _dot(q, k_ctx, _NT)
            kpos = (blk0 + sb - 1) * ATT_BLOCK + j
            valid = in_window & (kpos >= 0) & (kpos < seq_len)
            s_band = jnp.where(valid, s_band, -1e30)
            m = jnp.maximum(sink, jnp.maximum(jnp.max(s_band, axis=-1, keepdims=True),
                                              jnp.max(s_ctx, axis=-1, keepdims=True)))
            p_band = jnp.exp(s_band - m)
            p_ctx = jnp.exp(s_ctx - m)
            denom = (jnp.exp(sink - m) + jnp.sum(p_band, axis=-1, keepdims=True)
                     + jnp.sum(p_ctx, axis=-1, keepdims=True))
            o = (_dot(p_band.astype(bf16), v_all[band]) + _dot(p_ctx.astype(bf16), v_ctx)) / denom
            for g in range(Q_PER_KV):
                c0 = (kvh * Q_PER_KV + g) * HEAD_DIM
                o_ref[rows, c0:c0 + HEAD_DIM] = o[g * ATT_BLOCK:(g + 1) * ATT_BLOCK].astype(o_ref.dtype)


def _window_attention(qkv, sink, n_batch, seq_len, n_ctx):
    tq = ROW_TILE
    n_lat = n_batch * seq_len
    tiles_per_seq = seq_len // tq
    sub = tq // ATT_BLOCK
    n_blocks = n_lat // ATT_BLOCK
    ctx_blk0 = n_lat // n_ctx
    w = qkv.shape[1]
    kern = functools.partial(_win_attn_kernel, tiles_per_seq=tiles_per_seq, seq_len=seq_len)
    return pl.pallas_call(
        kern,
        grid=(n_lat // tq,),
        in_specs=[
            pl.BlockSpec(memory_space=pltpu.SMEM),
            pl.BlockSpec((tq, w), lambda i: (i, 0)),
            pl.BlockSpec((ATT_BLOCK, w), lambda i: (jnp.maximum(i * sub - 1, 0), 0)),
            pl.BlockSpec((ATT_BLOCK, w), lambda i: (jnp.minimum(i * sub + sub, n_blocks - 1), 0)),
            pl.BlockSpec((n_ctx, w), lambda i: (ctx_blk0 + i // tiles_per_seq, 0)),
        ],
        out_specs=pl.BlockSpec((tq, 512), lambda i: (i, 0)),
        out_shape=jax.ShapeDtypeStruct((n_lat, 512), bf16),
        compiler_params=_cparams("arbitrary"),
        name="window_attention",
    )(sink, qkv, qkv, qkv, qkv)


def _ctx_attn_kernel(sink_ref, qkv_ref, o_ref):
    n = qkv_ref.shape[0]
    for kvh in range(N_KV_HEADS):
        kc = 512 + kvh * HEAD_DIM
        vc = 640 + kvh * HEAD_DIM
        q = jnp.concatenate(
            [qkv_ref[:, (kvh * Q_PER_KV + g) * HEAD_DIM:(kvh * Q_PER_KV + g + 1) * HEAD_DIM]
             for g in range(Q_PER_KV)], axis=0)
        s = _dot(q, qkv_ref[:, kc:kc + HEAD_DIM], _NT)
        sink = _sink_column(sink_ref, kvh, n)
        m = jnp.maximum(sink, jnp.max(s, axis=-1, keepdims=True))
        p = jnp.exp(s - m)
        denom = jnp.exp(sink - m) + jnp.sum(p, axis=-1, keepdims=True)
        o = _dot(p.astype(bf16), qkv_ref[:, vc:vc + HEAD_DIM]) / denom
        for g in range(Q_PER_KV):
            c0 = (kvh * Q_PER_KV + g) * HEAD_DIM
            o_ref[:, c0:c0 + HEAD_DIM] = o[g * n:(g + 1) * n].astype(o_ref.dtype)


def _context_attention(qkv, sink, n_batch, n_lat, n_ctx):
    w = qkv.shape[1]
    blk0 = n_lat // n_ctx
    return pl.pallas_call(
        _ctx_attn_kernel,
        grid=(n_batch,),
        in_specs=[pl.BlockSpec(memory_space=pltpu.SMEM),
                  pl.BlockSpec((n_ctx, w), lambda b: (blk0 + b, 0))],
        out_specs=pl.BlockSpec((n_ctx, 512), lambda b: (b, 0)),
        out_shape=jax.ShapeDtypeStruct((n_batch * n_ctx, 512), bf16),
        compiler_params=_cparams("arbitrary"),
        name="context_attention",
    )(sink, qkv)


def _pool_kernel(main_ref, prev_ref, next_ref, pw_ref, ps_ref, o_ref, xe_ref, *, tiles_per_seq, seq_len):
    i = pl.program_id(0)
    tp = main_ref.shape[0]
    t_in_seq = i % tiles_per_seq
    h = POOL_HALO
    xe_ref[0:h, :] = jnp.where(t_in_seq == 0, 0.0, prev_ref[...])
    xe_ref[h:h + tp, :] = main_ref[...]
    xe_ref[h + tp:2 * h + tp, :] = jnp.where(t_in_seq == tiles_per_seq - 1, 0.0, next_ref[...])
    pos = t_in_seq * tp + lax.broadcasted_iota(jnp.int32, (tp, 1), 0)
    for gi, w in enumerate(POOL_WINDOWS):
        cols = slice(gi * 128, (gi + 1) * 128)
        acc = xe_ref[h - w // 2:h - w // 2 + tp, cols]
        for u in range(-w // 2 + 1, w // 2):
            acc = acc + xe_ref[h + u:h + u + tp, cols]
        lo = jnp.maximum(pos - w // 2, 0)
        hi = jnp.minimum(pos + w // 2, seq_len)
        cnt = (hi - lo).astype(f32)
        pooled = acc / cnt - main_ref[:, cols]
        y = _dot(pooled.astype(bf16), pw_ref[gi])
        o_ref[:, cols] = (y * ps_ref[:, cols]).astype(o_ref.dtype)


def _multi_scale_pool(z, row0, n_rows, seq_len, tp, p_w, p_scale):
    tiles_per_seq = seq_len // tp
    t0 = row0 // tp
    h0 = row0 // POOL_HALO
    hb = tp // POOL_HALO
    n_halo = n_rows // POOL_HALO
    kern = functools.partial(_pool_kernel, tiles_per_seq=tiles_per_seq, seq_len=seq_len)
    return pl.pallas_call(
        kern,
        grid=(n_rows // tp,),
        in_specs=[
            pl.BlockSpec((tp, 512), lambda i: (t0 + i, 0)),
            pl.BlockSpec((POOL_HALO, 512), lambda i: (h0 + jnp.maximum(i * hb - 1, 0), 0)),
            pl.BlockSpec((POOL_HALO, 512), lambda i: (h0 + jnp.minimum((i + 1) * hb, n_halo - 1), 0)),
            pl.BlockSpec((4, 128, 128), lambda i: (0, 0, 0)),
            pl.BlockSpec((1, 512), lambda i: (0, 0)),
        ],
        out_specs=pl.BlockSpec((tp, 512), lambda i: (i, 0)),
        out_shape=jax.ShapeDtypeStruct((n_rows, 512), bf16),
        scratch_shapes=[pltpu.VMEM((tp + 2 * POOL_HALO, 512), f32)],
        compiler_params=_cparams("arbitrary"),
        name="multi_scale_pool",
    )(z, z, z, p_w, p_scale)


GLA_SEGMENTS = (1, 2, 4, 8, 16, 32, 64)


def _gla_tables():
    c = GLA_CHUNK
    i = np.arange(c)[:, None]
    j = np.arange(c)[None, :]
    tri = (j <= i).astype(np.float32)
    mask = np.zeros((2, 7, c, c), np.float32)
    mask[:, 0] = np.eye(c)
    for lv, s in enumerate(GLA_SEGMENTS[:-1]):
        m = ((i // (2 * s)) == (j // (2 * s))) & ((i // s) % 2 == 1) & ((j // s) % 2 == 0)
        mask[0, 1 + lv] = m
        mask[1, 1 + lv] = m.T
    mask = np.tile(mask, (1, 1, 1, GLA_HEADS))
    r = np.arange(GLA_HEADS * GLA_DV)[:, None] // GLA_DV
    cc = np.arange(GLA_HEADS * GLA_DK)[None, :] // GLA_DK
    bd = (r == cc).astype(np.float32)
    return tri, mask, bd


def _segment_sums(g, cum):
    c = GLA_CHUNK
    row = lax.broadcasted_iota(jnp.int32, (c, 1), 0)
    zero = jnp.zeros_like(g)
    before = [None] + [pltpu.roll(g, kk, 0) for kk in (1, 2, 3)]
    after = [None] + [pltpu.roll(g, c - kk, 0) for kk in (1, 2, 3)]
    a, r = {1: g}, {1: zero}
    for s in (2, 4):
        pos = row % s
        a_s, r_s = g, zero
        for kk in range(1, s):
            a_s = a_s + jnp.where(pos >= kk, before[kk], 0.0)
            r_s = r_s + jnp.where(pos < s - kk, after[kk], 0.0)
        a[s], r[s] = a_s, r_s
    blocks = [cum[8 * b:8 * b + 8] for b in range(c // 8)]
    last = [cum[8 * b + 7:8 * b + 8] for b in range(c // 8)]
    for s in (8, 16, 32, 64):
        nb = s // 8
        a_blk, r_blk = [], []
        for b in range(c // 8):
            prev_end = (b // nb) * nb - 1
            a_blk.append(blocks[b] - last[prev_end] if prev_end >= 0 else blocks[b])
            r_blk.append(last[(b // nb + 1) * nb - 1] - blocks[b])
        a[s] = jnp.concatenate(a_blk, axis=0)
        r[s] = jnp.concatenate(r_blk, axis=0)
    return a, r


def _gla_chunk(d, q, k, v, g, tri_ref, mask_ref, bd_ref, st_ref):
    c = GLA_CHUNK
    w = g.shape[1]
    g_hi = g.astype(bf16)
    r1 = g - g_hi.astype(f32)
    g_mid = r1.astype(bf16)
    g_lo = (r1 - g_mid.astype(f32)).astype(bf16)
    cum3 = _dot(tri_ref[...], jnp.concatenate([g_hi, g_mid, g_lo], axis=1))
    cum = cum3[:, 0:w] + cum3[:, w:2 * w] + cum3[:, 2 * w:3 * w]
    a, r = _segment_sums(g, cum)
    if d == 0:
        cq = [a[s] for s in GLA_SEGMENTS]
        ck = [r[s] for s in GLA_SEGMENTS]
    else:
        cq = [r[s] + g for s in GLA_SEGMENTS]
        ck = [a[s] - g for s in GLA_SEGMENTS]
    total = cum[c - 1:c]

    q = q * (GLA_DK ** -0.5)
    head_of_lane = lax.broadcasted_iota(jnp.int32, (1, w), 1) // GLA_DK
    vhead_of_lane = lax.broadcasted_iota(jnp.int32, (1, v.shape[1]), 1) // GLA_DV

    def stack_heads(x, lane_head):
        return jnp.concatenate([jnp.where(lane_head == h, x, jnp.zeros_like(x))
                                for h in range(GLA_HEADS)], axis=0)

    kb = k.astype(bf16)
    k_plain = stack_heads(kb, head_of_lane)
    q01 = jnp.concatenate([q.astype(bf16), (q * jnp.exp(cq[0])).astype(bf16)], axis=0)
    s01 = _dot(q01, k_plain, _NT)
    att = mask_ref[d, 0] * s01[0:c] + mask_ref[d, 1] * s01[c:2 * c]
    for lv in range(1, 6):
        qs = (q * jnp.exp(cq[lv])).astype(bf16)
        ks = stack_heads((k * jnp.exp(ck[lv])).astype(bf16), head_of_lane)
        att = att + mask_ref[d, 1 + lv] * _dot(qs, ks, _NT)
    vb = v.astype(bf16)
    st = st_ref[d]
    q_state = (q * jnp.exp(cq[6])).astype(bf16)
    o = _dot(att.astype(bf16), stack_heads(vb, vhead_of_lane)) + _dot(q_state, st.astype(bf16), _NT)
    k_state = (k * jnp.exp(ck[6])).astype(bf16)
    st_ref[d] = st * jnp.exp(total) + bd_ref[...] * _dot(vb, k_state, _TN)
    return o


def _gla_kernel(qf_ref, kf_ref, vf_ref, gf_ref, qb_ref, kb_ref, vb_ref, gb_ref,
                tri_ref, mask_ref, bd_ref, of_ref, ob_ref, st_ref):
    @pl.when(pl.program_id(1) == 0)
    def _():
        st_ref[...] = jnp.zeros_like(st_ref)

    n_chunks = qf_ref.shape[0] // GLA_CHUNK
    for ci in range(n_chunks):
        rf = slice(ci * GLA_CHUNK, (ci + 1) * GLA_CHUNK)
        of_ref[rf, :] = _gla_chunk(0, qf_ref[rf, :], kf_ref[rf, :], vf_ref[rf, :], gf_ref[rf, :],
                                   tri_ref, mask_ref, bd_ref, st_ref)
        cb = n_chunks - 1 - ci
        rb = slice(cb * GLA_CHUNK, (cb + 1) * GLA_CHUNK)
        ob_ref[rb, :] = _gla_chunk(1, qb_ref[rb, :], kb_ref[rb, :], vb_ref[rb, :], gb_ref[rb, :],
                                   tri_ref, mask_ref, bd_ref, st_ref)


def _gla_scan(z_gla, gates, n_batch, seq_len, n_ctx):
    rows = z_gla.shape[0]
    tb = GLA_BLOCK
    assert n_ctx == tb
    lat_blocks = seq_len // tb
    ctx_blk0 = n_batch * lat_blocks
    tri, mask, bd = _gla_tables()

    def fwd_row(b, s):
        return jnp.where(s == 0, ctx_blk0 + b, b * lat_blocks + s - 1)

    def bwd_row(b, s):
        return jnp.where(s == 0, ctx_blk0 + b, b * lat_blocks + lat_blocks - s)

    def specs(row, direction):
        return [
            pl.BlockSpec((tb, 256), lambda b, s: (row(b, s), 0)),
            pl.BlockSpec((tb, 256), lambda b, s: (row(b, s), 1)),
            pl.BlockSpec((tb, 512), lambda b, s: (row(b, s), 1)),
            pl.BlockSpec((tb, 256), lambda b, s: (row(b, s), direction)),
        ]

    const = lambda shape: pl.BlockSpec(shape, lambda b, s: (0,) * len(shape))
    return pl.pallas_call(
        _gla_kernel,
        grid=(n_batch, 1 + lat_blocks),
        in_specs=specs(fwd_row, 0) + specs(bwd_row, 1) + [
            const(tri.shape), const(mask.shape), const(bd.shape)],
        out_specs=[pl.BlockSpec((tb, 512), lambda b, s: (fwd_row(b, s), 0)),
                   pl.BlockSpec((tb, 512), lambda b, s: (bwd_row(b, s), 0))],
        out_shape=[jax.ShapeDtypeStruct((rows, 512), f32)] * 2,
        scratch_shapes=[pltpu.VMEM((2, GLA_HEADS * GLA_DV, GLA_HEADS * GLA_DK), f32)],
        compiler_params=_cparams("arbitrary", "arbitrary"),
        name="gla_scan",
    )(z_gla, z_gla, z_gla, gates, z_gla, z_gla, z_gla, gates,
      jnp.asarray(tri, bf16), jnp.asarray(mask, f32), jnp.asarray(bd, f32))


def _merge_kernel(h_ref, a_ref, p_ref, s_ref, of_ref, ob_ref, r_ref, ng_ref, wg_ref, wb_ref, o_ref, g_ref):
    @pl.when(pl.program_id(1) == 0)
    def _():
        o = of_ref[...] + ob_ref[...]
        r = r_ref[...]
        gate = r * jax.nn.sigmoid(r)
        for h in range(GLA_HEADS):
            cols = slice(h * GLA_DV, (h + 1) * GLA_DV)
            oh = o[:, cols]
            oh = oh * lax.rsqrt(jnp.mean(oh * oh, axis=-1, keepdims=True) + EPS)
            g_ref[:, cols] = (oh * ng_ref[:, cols] * gate[:, cols]).astype(g_ref.dtype)

    hb = h_ref[...]
    acc = None
    for br, b_ref in enumerate((a_ref, p_ref, s_ref, g_ref)):
        gate = jax.nn.sigmoid(_dot(hb, wg_ref[br]))
        term = gate * _dot(b_ref[...], wb_ref[br])
        acc = term if acc is None else acc + term
    o_ref[...] = acc.astype(o_ref.dtype)


def _gated_merge(h, branches, o_f, o_b, z_gla, norm_g, n_rows, layer, wg, wb):
    d = h.shape[1]
    tm, tn = ROW_TILE, 512
    bspec = pl.BlockSpec((tm, 512), lambda i, j: (i, 0))
    return pl.pallas_call(
        _merge_kernel,
        grid=(n_rows // tm, d // tn),
        in_specs=[pl.BlockSpec((tm, d), lambda i, j: (i, 0)), bspec, bspec, bspec,
                  bspec, bspec,
                  pl.BlockSpec((tm, 512), lambda i, j: (i, 2)),
                  pl.BlockSpec((1, 512), lambda i, j: (0, 0)),
                  pl.BlockSpec((None, 4, d, tn), lambda i, j: (layer, 0, 0, j)),
                  pl.BlockSpec((None, 4, 512, tn), lambda i, j: (layer, 0, 0, j))],
        out_specs=pl.BlockSpec((tm, tn), lambda i, j: (i, j)),
        out_shape=jax.ShapeDtypeStruct((n_rows, d), bf16),
        scratch_shapes=[pltpu.VMEM((tm, 512), bf16)],
        compiler_params=_cparams("arbitrary", "arbitrary"),
        name="gated_merge",
    )(h, *branches, o_f, o_b, z_gla, norm_g, wg, wb)


def _out_proj_kernel(acc_ref, wo_ref, xa_ref, xb_ref, gate_ref, o_ref, *, n_a_tiles):
    x = _two_source_rows(xa_ref, xb_ref, n_a_tiles)
    o_ref[...] = x + gate_ref[...] * _dot(acc_ref[...], wo_ref[...])


def _out_proj_residual(acc, layer, wo, xa, xb, n_a_tiles, b_tile0, mods, n_rows, tiles_per_mod):
    d = acc.shape[1]
    tm = ROW_TILE
    return pl.pallas_call(
        functools.partial(_out_proj_kernel, n_a_tiles=n_a_tiles),
        grid=(n_rows // tm,),
        in_specs=[pl.BlockSpec((tm, d), lambda i: (i, 0)),
                  pl.BlockSpec((None, d, d), lambda i: (layer, 0, 0))]
        + _two_source_specs(tm, d, n_a_tiles, b_tile0)
        + [pl.BlockSpec((None, 1, d), lambda i: (i // tiles_per_mod, 0, 2))],
        out_specs=pl.BlockSpec((tm, d), lambda i: (i, 0)),
        out_shape=jax.ShapeDtypeStruct((n_rows, d), f32),
        compiler_params=_cparams("arbitrary"),
        name="out_proj_residual",
    )(acc, wo, xa, xb, mods)


def _router_kernel(x_ref, g_ref, sh_ref, sc_ref, rw_ref, rb_ref, tri_ref, info_ref, cnt_ref, carry_ref):
    i = pl.program_id(0)

    @pl.when(i == 0)
    def _():
        carry_ref[...] = jnp.zeros_like(carry_ref)

    h2 = _norm_mod(x_ref[...], g_ref[...], sh_ref[...], sc_ref[...])
    tm = h2.shape[0]
    h_hi = h2.astype(bf16)
    h_lo = (h2 - h_hi.astype(f32)).astype(bf16)
    part = _dot(rw_ref[...], h_hi, _NT)
    logits = (part[0:N_EXPERTS] + part[N_EXPERTS:2 * N_EXPERTS]
              + _dot(rw_ref[0:N_EXPERTS, :], h_lo, _NT))
    score = jax.nn.sigmoid(logits)
    biased = score + rb_ref[...]
    b = [biased[e:e + 1] for e in range(N_EXPERTS)]
    sc = [score[e:e + 1] for e in range(N_EXPERTS)]

    def top2_sum(v):
        hi01, lo01 = jnp.maximum(v[0], v[1]), jnp.minimum(v[0], v[1])
        hi23, lo23 = jnp.maximum(v[2], v[3]), jnp.minimum(v[2], v[3])
        return jnp.maximum(hi01, hi23) + jnp.maximum(jnp.minimum(hi01, hi23), jnp.maximum(lo01, lo23))

    best = jnp.zeros((1, tm), jnp.int32)
    best_score = top2_sum(b[0:4])
    for gidx in range(1, N_GROUPS):
        gs = top2_sum(b[4 * gidx:4 * gidx + 4])
        take = gs > best_score
        best = jnp.where(take, gidx, best)
        best_score = jnp.where(take, gs, best_score)
    cb, cs = [], []
    for kk in range(EXPERTS_PER_GROUP):
        vb, vs = b[kk], sc[kk]
        for gidx in range(1, N_GROUPS):
            vb = jnp.where(best == gidx, b[4 * gidx + kk], vb)
            vs = jnp.where(best == gidx, sc[4 * gidx + kk], vs)
        cb.append(vb)
        cs.append(vs)
    i1 = jnp.zeros((1, tm), jnp.int32)
    m1, w1 = cb[0], cs[0]
    for kk in range(1, EXPERTS_PER_GROUP):
        take = cb[kk] > m1
        i1 = jnp.where(take, kk, i1)
        m1 = jnp.where(take, cb[kk], m1)
        w1 = jnp.where(take, cs[kk], w1)
    i2 = jnp.where(i1 == 0, 1, 0).astype(jnp.int32)
    m2 = jnp.where(i1 == 0, cb[1], cb[0])
    w2 = jnp.where(i1 == 0, cs[1], cs[0])
    for kk in range(1, EXPERTS_PER_GROUP):
        take = (cb[kk] > m2) & (i1 != kk) & (i2 != kk)
        i2 = jnp.where(take, kk, i2)
        m2 = jnp.where(take, cb[kk], m2)
        w2 = jnp.where(take, cs[kk], w2)
    e1 = best * EXPERTS_PER_GROUP + i1
    e2 = best * EXPERTS_PER_GROUP + i2
    wsum = w1 + w2
    erow = lax.broadcasted_iota(jnp.int32, (N_EXPERTS, tm), 0)
    oh1 = (erow == e1).astype(f32)
    oh2 = (erow == e2).astype(f32)
    oh = oh1 + oh2
    rank = _dot(oh.astype(bf16), tri_ref[...]) + carry_ref[:, 0:1]
    carry_ref[...] = carry_ref[...] + jnp.sum(oh, axis=1, keepdims=True)
    r1 = jnp.sum(oh1 * rank, axis=0, keepdims=True)
    r2 = jnp.sum(oh2 * rank, axis=0, keepdims=True)
    info_ref[...] = jnp.concatenate(
        [e1.astype(f32), e2.astype(f32), w1 / wsum, w2 / wsum, r1, r2,
         jnp.zeros((2, tm), f32)], axis=0)
    cnt_ref[...] = carry_ref[...]


def _router(x_all, n_rows, g, mods, rw_t, rb_col, tiles_per_mod):
    d = x_all.shape[1]
    tm = ROW_TILE
    tri = jnp.asarray(np.triu(np.ones((tm, tm), np.float32), k=1), bf16)
    mod_spec = lambda k: pl.BlockSpec((None, 1, d), lambda i: (i // tiles_per_mod, 0, k))
    return pl.pallas_call(
        _router_kernel,
        grid=(n_rows // tm,),
        in_specs=[pl.BlockSpec((tm, d), lambda i: (i, 0)),
                  pl.BlockSpec((1, d), lambda i: (0, 0)),
                  mod_spec(3), mod_spec(4),
                  pl.BlockSpec((2 * N_EXPERTS, d), lambda i: (0, 0)),
                  pl.BlockSpec((N_EXPERTS, 1), lambda i: (0, 0)),
                  pl.BlockSpec((tm, tm), lambda i: (0, 0))],
        out_specs=[pl.BlockSpec((8, tm), lambda i: (0, i)),
                   pl.BlockSpec((N_EXPERTS, 128), lambda i: (0, 0))],
        out_shape=[jax.ShapeDtypeStruct((8, n_rows), f32),
                   jax.ShapeDtypeStruct((N_EXPERTS, 128), f32)],
        scratch_shapes=[pltpu.VMEM((N_EXPERTS, 128), f32)],
        compiler_params=_cparams("arbitrary"),
        name="moe_router",
    )(x_all, g, mods, mods, rw_t, rb_col, tri)


def _dispatch_kernel(pos_ref, zpos_ref, x_ref, g_ref, sh_ref, sc_ref, xs_ref, buf_ref, zero_ref, sem, zsem):
    i = pl.program_id(0)
    tm = x_ref.shape[0]

    @pl.when(i == 0)
    def _():
        zero_ref[...] = jnp.zeros_like(zero_ref)

        def zero_tile(row):
            cp = pltpu.make_async_copy(zero_ref, xs_ref.at[pl.ds(pl.multiple_of(row, 8), tm)], zsem)
            cp.start()
            cp.wait()

        for e in range(N_EXPERTS):
            zero_tile((zpos_ref[e] // 8) * 8)

        def tail(t, c):
            zero_tile(t * tm)
            return c

        lax.fori_loop(zpos_ref[N_EXPERTS], xs_ref.shape[0] // tm, tail, 0)

    slot = i % 2

    def wait_slot(s):
        for k in range(2):
            pltpu.make_async_copy(buf_ref.at[s], xs_ref.at[pl.ds(0, tm)], sem.at[s]).wait()

    @pl.when(i >= 2)
    def _():
        wait_slot(slot)

    buf_ref[slot] = _norm_mod(x_ref[...], g_ref[...], sh_ref[...], sc_ref[...])
    src = buf_ref.at[slot]
    for r in range(tm):
        for k in range(2):
            dst = pos_ref[k, i * tm + r]
            pltpu.make_async_copy(src.at[pl.ds(r, 1)], xs_ref.at[pl.ds(dst, 1)], sem.at[slot]).start()

    @pl.when(i == pl.num_programs(0) - 1)
    def _():
        wait_slot(slot)
        wait_slot(1 - slot)


def _dispatch(pos, zpos, x_all, n_rows, g, mods, tiles_per_mod, p_rows):
    d = x_all.shape[1]
    tm = MOE_TILE
    tpm = tiles_per_mod * (ROW_TILE // tm)
    mod_spec = lambda k: pl.BlockSpec((None, 1, d), lambda i, pos, zpos: (i // tpm, 0, k))
    return pl.pallas_call(
        _dispatch_kernel,
        grid_spec=pltpu.PrefetchScalarGridSpec(
            num_scalar_prefetch=2,
            grid=(n_rows // tm,),
            in_specs=[pl.BlockSpec((tm, d), lambda i, pos, zpos: (i, 0)),
                      pl.BlockSpec((1, d), lambda i, pos, zpos: (0, 0)),
                      mod_spec(3), mod_spec(4)],
            out_specs=pl.BlockSpec(memory_space=pl.ANY),
            scratch_shapes=[pltpu.VMEM((2, tm, d), f32), pltpu.VMEM((tm, d), f32),
                            pltpu.SemaphoreType.DMA((2,)), pltpu.SemaphoreType.DMA(())],
        ),
        out_shape=jax.ShapeDtypeStruct((p_rows, d), f32),
        compiler_params=_cparams("arbitrary"),
        name="moe_dispatch",
    )(pos, zpos, x_all, g, mods, mods)


def _expert_kernel(te_ref, nu_ref, xs_ref, wg_ref, wu_ref, wd_ref, y_ref):
    i = pl.program_id(0)

    @pl.when(i < nu_ref[0])
    def _():
        xb = xs_ref[...].astype(bf16)
        a = _dot(xb, wg_ref[...])
        act = (a * jax.nn.sigmoid(a)) * _dot(xb, wu_ref[...])
        y_ref[...] = _dot(act.astype(bf16), wd_ref[...])

    @pl.when(i >= nu_ref[0])
    def _():
        y_ref[...] = jnp.zeros_like(y_ref)


def _expert_ffn(tile_expert, n_used, xs, layer, wg, wu, wd):
    p_rows, d = xs.shape
    de = wg.shape[3]
    tm = MOE_TILE
    return pl.pallas_call(
        _expert_kernel,
        grid_spec=pltpu.PrefetchScalarGridSpec(
            num_scalar_prefetch=2,
            grid=(p_rows // tm,),
            in_specs=[pl.BlockSpec((tm, d), lambda i, te, nu: (jnp.where(i < nu[0], i, 0), 0)),
                      pl.BlockSpec((None, None, d, de), lambda i, te, nu: (layer, te[i], 0, 0)),
                      pl.BlockSpec((None, None, d, de), lambda i, te, nu: (layer, te[i], 0, 0)),
                      pl.BlockSpec((None, None, de, d), lambda i, te, nu: (layer, te[i], 0, 0))],
            out_specs=pl.BlockSpec((tm, d), lambda i, te, nu: (i, 0)),
        ),
        out_shape=jax.ShapeDtypeStruct((p_rows, d), f32),
        compiler_params=_cparams("arbitrary"),
        name="moe_experts",
    )(tile_expert, n_used, xs, wg, wu, wd)


def _combine_kernel(pos_ref, x_ref, w_ref, gate_ref, fg_ref, y_ref, o_ref, ybuf_ref, sem, *, final_norm):
    i = pl.program_id(0)
    tm = x_ref.shape[0]
    slot = i % 2

    def gather(step, s):
        for r in range(tm):
            for k in range(2):
                src = pos_ref[k, step * tm + r]
                pltpu.make_async_copy(y_ref.at[pl.ds(src, 1)], ybuf_ref.at[s, k].at[pl.ds(r, 1)],
                                      sem.at[s]).start()

    @pl.when(i == 0)
    def _():
        gather(0, 0)

    @pl.when(i + 1 < pl.num_programs(0))
    def _():
        gather(i + 1, 1 - slot)

    for k in range(2):
        pltpu.make_async_copy(y_ref.at[pl.ds(0, tm)], ybuf_ref.at[slot, k], sem.at[slot]).wait()
    w = w_ref[...]
    moe = w[:, 0:1] * ybuf_ref[slot, 0] + w[:, 1:2] * ybuf_ref[slot, 1]
    x = x_ref[...] + gate_ref[...] * moe
    if final_norm:
        x = x * lax.rsqrt(jnp.mean(x * x, axis=-1, keepdims=True) + EPS) * fg_ref[...]
    o_ref[...] = x


def _combine(pos, x_all, n_rows, w_tok, mods, final_g, y, tiles_per_mod, final_norm):
    d = x_all.shape[1]
    tm = MOE_TILE
    tpm = tiles_per_mod * (ROW_TILE // tm)
    kern = functools.partial(_combine_kernel, final_norm=final_norm)
    return pl.pallas_call(
        kern,
        grid_spec=pltpu.PrefetchScalarGridSpec(
            num_scalar_prefetch=1,
            grid=(n_rows // tm,),
            in_specs=[pl.BlockSpec((tm, d), lambda i, pos: (i, 0)),
                      pl.BlockSpec((tm, 2), lambda i, pos: (i, 0)),
                      pl.BlockSpec((None, 1, d), lambda i, pos: (i // tpm, 0, 5)),
                      pl.BlockSpec((1, d), lambda i, pos: (0, 0)),
                      pl.BlockSpec(memory_space=pl.ANY)],
            out_specs=pl.BlockSpec((tm, d), lambda i, pos: (i, 0)),
            scratch_shapes=[pltpu.VMEM((2, 2, tm, d), f32), pltpu.SemaphoreType.DMA((2,))],
        ),
        out_shape=jax.ShapeDtypeStruct((n_rows, d), f32),
        compiler_params=_cparams("arbitrary"),
        name="moe_combine",
    )(pos, x_all, w_tok, mods, final_g, y)


def _moe(x_all, n_rows, norm_g, mods, tiles_per_mod, rw_t, rb_col, layer, wg, wu, wd, final_g, final_norm):
    info, counts = _router(x_all, n_rows, norm_g, mods, rw_t, rb_col, tiles_per_mod)
    tm = MOE_TILE
    p_rows = 2 * n_rows + (N_EXPERTS + 1) * tm
    cnt = counts[:, 0].astype(jnp.int32)
    padded = ((cnt + tm - 1) // tm) * tm
    ends = jnp.cumsum(padded)
    starts = ends - padded
    e1 = info[0].astype(jnp.int32)
    e2 = info[1].astype(jnp.int32)
    pos = jnp.stack([starts[e1] + info[4].astype(jnp.int32),
                     starts[e2] + info[5].astype(jnp.int32)])
    n_used = (ends[-1] // tm).astype(jnp.int32)
    tile_start = jnp.arange(p_rows // tm, dtype=jnp.int32) * tm
    tile_expert = jnp.sum((ends[None, :] <= tile_start[:, None]).astype(jnp.int32), axis=1)
    tile_expert = jnp.minimum(tile_expert, N_EXPERTS - 1)
    zpos = jnp.concatenate([starts + cnt, n_used.reshape(1)])
    xs = _dispatch(pos, zpos, x_all, n_rows, norm_g, mods, tiles_per_mod, p_rows)
    y = _expert_ffn(tile_expert, n_used.reshape(1), xs, layer, wg, wu, wd)
    w_tok = jnp.transpose(info[2:4])
    return _combine(pos, x_all, n_rows, w_tok, mods, final_g, y, tiles_per_mod, final_norm)


def _rope_tables(seq_len, n_batch, n_ctx_rows):
    rows = seq_len // GRID_W
    row = jnp.repeat(jnp.arange(rows), GRID_W)
    col = jnp.tile(jnp.arange(GRID_W), rows)
    nf = HEAD_DIM // 4
    inv_freq = ROPE_BASE ** (-jnp.arange(nf, dtype=f32) / nf)
    ang_r = row[:, None].astype(f32) * inv_freq
    ang_c = col[:, None].astype(f32) * inv_freq
    cos64 = jnp.concatenate([jnp.cos(ang_r)] * 2 + [jnp.cos(ang_c)] * 2, axis=-1)
    sin64 = jnp.concatenate([-jnp.sin(ang_r), jnp.sin(ang_r), -jnp.sin(ang_c), jnp.sin(ang_c)], axis=-1)
    cos_t = jnp.tile(jnp.concatenate([cos64, cos64], axis=-1), (n_batch, 1))
    sin_t = jnp.tile(jnp.concatenate([sin64, sin64], axis=-1), (n_batch, 1))
    cos_t = jnp.concatenate([cos_t, jnp.ones((n_ctx_rows, 128), f32)], axis=0)
    sin_t = jnp.concatenate([sin_t, jnp.zeros((n_ctx_rows, 128), f32)], axis=0)
    return cos_t, sin_t


def kernel(x, c, ctx, c_ctx, w_mod, b_mod, norm1_g, norm2_g, final_norm_g, w_in, a_sink,
           p_w, p_scale, c_ln_g, c_ln_b, c_ws, c_bs, g_w2, g_b, g_norm_g,
           w_branch, w_gate, w_out, router_w, router_b, e_gate, e_up, e_down):
    n_batch, seq_len, d = x.shape
    n_ctx = ctx.shape[1]
    depth = w_mod.shape[0]
    n_lat = n_batch * seq_len
    n_all = n_lat + n_batch * n_ctx
    tiles_per_mod = seq_len // ROW_TILE

    cond = jnp.zeros((8, d), f32).at[0:n_batch].set(c).at[n_batch].set(c_ctx)
    mods_all = _modulation(cond, w_mod, b_mod)
    cos_t, sin_t = _rope_tables(seq_len, n_batch, n_batch * n_ctx)
    rw_f = jnp.transpose(router_w)
    rw_hi = rw_f.astype(bf16)
    rw_t = jnp.concatenate([rw_hi, (rw_f - rw_hi.astype(f32)).astype(bf16)], axis=0)
    rb_col = router_b.reshape(N_EXPERTS, 1)
    final_g = final_norm_g.reshape(1, d)

    w_in_b = jnp.pad(w_in, ((0, 0), (0, 0), (0, 96))).astype(bf16)
    w_gate_b = w_gate.astype(bf16)
    w_branch_b = w_branch.astype(bf16)
    w_out_b = w_out.astype(bf16)
    e_gate_b = e_gate.astype(bf16)
    e_up_b = e_up.astype(bf16)
    e_down_b = e_down.astype(bf16)

    n_lat_tiles = n_lat // ROW_TILE
    xa, xb, b_tile0 = x.reshape(n_lat, d), ctx.reshape(n_batch * n_ctx, d), 0
    for l in range(depth):
        last = l == depth - 1
        n_out = n_lat if last else n_all
        mods = mods_all[l].reshape(8, 1, 6 * d)
        w2p = jnp.zeros((128, 512), f32)
        w2p = w2p.at[0:GLA_RANK, 0:256].set(g_w2[l, 0]).at[GLA_RANK:2 * GLA_RANK, 256:512].set(g_w2[l, 1])
        w2_hi = w2p.astype(bf16)
        w2_split = jnp.concatenate([w2_hi, w2_hi, (w2p - w2_hi.astype(f32)).astype(bf16)], axis=0)
        bs_b = jnp.broadcast_to(c_bs[l][:, :, None], (4, SG_CHUNK, 128))
        h, qkv, z_pool, s_br, z_gla, gates = _in_proj(
            xa, xb, n_lat_tiles, b_tile0, n_all, l, norm1_g[l].reshape(1, d), mods, cos_t, sin_t,
            w_in_b, w2_split, g_b[l].reshape(1, 512), c_ln_g[l].reshape(1, 512),
            c_ln_b[l].reshape(1, 512), c_ws[l].astype(bf16), bs_b, tiles_per_mod)
        a_br = _window_attention(qkv, a_sink[l], n_batch, seq_len, n_ctx)
        pw = p_w[l].astype(bf16)
        ps = p_scale[l].reshape(1, 512)
        p_br = _multi_scale_pool(z_pool, 0, n_lat, seq_len, ROW_TILE, pw, ps)
        o_f, o_b = _gla_scan(z_gla, gates, n_batch, seq_len, n_ctx)
        if not last:
            a_c = _context_attention(qkv, a_sink[l], n_batch, n_lat, n_ctx)
            p_c = _multi_scale_pool(z_pool, n_lat, n_batch * n_ctx, n_ctx, n_ctx, pw, ps)
            a_br = jnp.concatenate([a_br, a_c], axis=0)
            p_br = jnp.concatenate([p_br, p_c], axis=0)
        acc = _gated_merge(h, (a_br, p_br, s_br), o_f, o_b, z_gla, g_norm_g[l].reshape(1, 512),
                           n_out, l, w_gate_b, w_branch_b)
        x_all = _out_proj_residual(acc, l, w_out_b, xa, xb, n_lat_tiles, b_tile0, mods, n_out,
                                   tiles_per_mod)
        x_all = _moe(x_all, n_out, norm2_g[l].reshape(1, d), mods, tiles_per_mod, rw_t, rb_col,
                     l, e_gate_b, e_up_b, e_down_b, final_g, last)
        xa, xb, b_tile0 = x_all, x_all, n_lat_tiles
    return x_all.reshape(n_batch, seq_len, d)
```

```python
import functools

import numpy as np
import jax
import jax.numpy as jnp
from jax import lax
from jax.experimental import pallas as pl
from jax.experimental.pallas import tpu as pltpu

f32 = jnp.float32
bf16 = jnp.bfloat16

EPS = 1e-6
GRID_W = 64
ROPE_BASE = 10000.0

HEAD_DIM = 64
N_Q_HEADS = 8
N_KV_HEADS = 2
Q_PER_KV = N_Q_HEADS // N_KV_HEADS
ATT_BLOCK = 128
POOL_WINDOWS = (2, 4, 8, 16)
POOL_HALO = 8
SG_CHUNK = 128
GLA_HEADS = 4
GLA_DK = 64
GLA_DV = 128
GLA_RANK = 16
GLA_TAU = 16.0
GLA_CHUNK = 64
N_EXPERTS = 16
EXPERTS_PER_GROUP = 4
N_GROUPS = 4
PAIRS_PER_GROUP = 6
N_CLASSES = N_GROUPS * PAIRS_PER_GROUP
N_CLASS_ROWS = 32
PAIR_LO = (0, 0, 0, 1, 1, 2)
PAIR_HI = (1, 2, 3, 2, 3, 3)

ROW_TILE = 512
GLA_BLOCK = 256
MOE_TILE = 256
VMEM_LIMIT = 56 * 1024 * 1024


def _cparams(*sem):
    return pltpu.CompilerParams(dimension_semantics=sem, vmem_limit_bytes=VMEM_LIMIT)


def _dot(a, b, dims=(((1,), (0,)), ((), ())), precision=None):
    return lax.dot_general(a, b, dims, precision=precision, preferred_element_type=f32)


_NT = (((1,), (1,)), ((), ()))
_TN = (((0,), (0,)), ((), ()))


def _mod_kernel(c_ref, w_ref, b_ref, o_ref):
    c = c_ref[...]
    a = c * jax.nn.sigmoid(c)
    w = w_ref[...]
    a_hi = a.astype(bf16)
    a_lo = (a - a_hi.astype(f32)).astype(bf16)
    w_hi = w.astype(bf16)
    w_lo = (w - w_hi.astype(f32)).astype(bf16)
    n = a.shape[0]
    both = _dot(jnp.concatenate([a_hi, a_lo], axis=0), w_hi)
    o_ref[...] = both[0:n] + both[n:2 * n] + _dot(a_hi, w_lo) + b_ref[...]


def _modulation(cond, w_mod, b_mod):
    n_layers, d, d6 = w_mod.shape
    tn = 1024
    return pl.pallas_call(
        _mod_kernel,
        grid=(n_layers, d6 // tn),
        in_specs=[
            pl.BlockSpec((8, d), lambda l, j: (0, 0)),
            pl.BlockSpec((None, d, tn), lambda l, j: (l, 0, j)),
            pl.BlockSpec((None, 1, tn), lambda l, j: (l, 0, j)),
        ],
        out_specs=pl.BlockSpec((None, 8, tn), lambda l, j: (l, 0, j)),
        out_shape=jax.ShapeDtypeStruct((n_layers, 8, d6), f32),
        compiler_params=_cparams("arbitrary", "arbitrary"),
        name="modulation",
    )(cond, w_mod, b_mod.reshape(n_layers, 1, d6))


def _norm_mod(x, g, shift, scale):
    y = x * lax.rsqrt(jnp.mean(x * x, axis=-1, keepdims=True) + EPS)
    return (y * g) * (1.0 + scale) + shift


def _two_source_rows(xa_ref, xb_ref, n_a_tiles):
    return jnp.where(pl.program_id(0) < n_a_tiles, xa_ref[...], xb_ref[...])


def _two_source_specs(tm, d, n_a_tiles, b_tile0):
    return [pl.BlockSpec((tm, d), lambda i: (jnp.minimum(i, n_a_tiles - 1), 0)),
            pl.BlockSpec((tm, d), lambda i: (b_tile0 + jnp.maximum(i - n_a_tiles, 0), 0),
                         pipeline_mode=pl.Buffered(1))]


def _spatial_gate_rows(uv, lg_ref, lb_ref, ws_ref, bs_ref, o_ref):
    a = jax.nn.gelu(uv)
    u = a[:, 0:512]
    v = a[:, 512:1024]
    mu = jnp.mean(v, axis=-1, keepdims=True)
    var = jnp.mean(jnp.square(v - mu), axis=-1, keepdims=True)
    vn = ((v - mu) * lax.rsqrt(var + EPS) * lg_ref[...] + lb_ref[...]).astype(bf16)
    for c in range(uv.shape[0] // SG_CHUNK):
        rows = slice(c * SG_CHUNK, (c + 1) * SG_CHUNK)
        for g in range(4):
            cols = slice(g * 128, (g + 1) * 128)
            mixed = _dot(ws_ref[g], vn[rows, cols]) + bs_ref[g]
            o_ref[rows, cols] = (u[rows, cols] * mixed).astype(o_ref.dtype)


def _in_proj_kernel(xa_ref, xb_ref, g_ref, sh_ref, sc_ref, cos_ref, sin_ref, w_ref,
                    w2_ref, gb_ref, lg_ref, lb_ref, ws_ref, bs_ref,
                    h_ref, qkv_ref, pool_ref, sg_ref, gla_ref, gate_ref, *, n_a_tiles):
    x = _two_source_rows(xa_ref, xb_ref, n_a_tiles)
    hb = _norm_mod(x, g_ref[...], sh_ref[...], sc_ref[...]).astype(bf16)
    h_ref[...] = hb

    cos = cos_ref[...]
    sin = sin_ref[...]
    lane = lax.broadcasted_iota(jnp.int32, cos.shape, 1)
    first_half = (lane % 32) < 16

    def rope(z):
        rot = jnp.where(first_half, pltpu.roll(z, 112, 1), pltpu.roll(z, 16, 1))
        return z * cos + rot * sin

    zq = _dot(hb, w_ref[:, 0:512])
    for c in range(4):
        qkv_ref[:, c * 128:(c + 1) * 128] = (
            rope(zq[:, c * 128:(c + 1) * 128]) * (HEAD_DIM ** -0.5)).astype(bf16)
    zkv = _dot(hb, w_ref[:, 512:768])
    qkv_ref[:, 512:640] = rope(zkv[:, 0:128]).astype(bf16)
    qkv_ref[:, 640:768] = zkv[:, 128:256].astype(bf16)
    pool_ref[...] = _dot(hb, w_ref[:, 768:1280])
    _spatial_gate_rows(_dot(hb, w_ref[:, 1280:2304]), lg_ref, lb_ref, ws_ref, bs_ref, sg_ref)
    gla_ref[...] = _dot(hb, w_ref[:, 2304:3840])
    low_rank = _dot(hb, w_ref[:, 3840:3968])
    lr_hi = low_rank.astype(bf16)
    lr_lo = (low_rank - lr_hi.astype(f32)).astype(bf16)
    logit = _dot(jnp.concatenate([lr_hi, lr_lo, lr_hi], axis=1), w2_ref[...]) + gb_ref[...]
    gate_ref[...] = _log_sigmoid(logit) / GLA_TAU


def _log_sigmoid(x):
    return jnp.minimum(x, 0.0) - jnp.log(1.0 + jnp.exp(-jnp.abs(x)))


def _in_proj(xa, xb, n_a_tiles, b_tile0, rows, layer, g, mods, cos_t, sin_t, w_pad, w2p, gbias,
             ln_g, ln_b, ws, bs_b, tiles_per_mod):
    d = xa.shape[1]
    n_w = w_pad.shape[2]
    tm = ROW_TILE
    mod_spec = lambda k: pl.BlockSpec((None, 1, d), lambda i: (i // tiles_per_mod, 0, k))
    row_spec = lambda w: pl.BlockSpec((tm, w), lambda i: (i, 0))
    return pl.pallas_call(
        functools.partial(_in_proj_kernel, n_a_tiles=n_a_tiles),
        grid=(rows // tm,),
        in_specs=_two_source_specs(tm, d, n_a_tiles, b_tile0) + [
            pl.BlockSpec((1, d), lambda i: (0, 0)),
            mod_spec(0), mod_spec(1),
            row_spec(128), row_spec(128),
            pl.BlockSpec((None, d, n_w), lambda i: (layer, 0, 0), pipeline_mode=pl.Buffered(1)),
            pl.BlockSpec((384, 512), lambda i: (0, 0)),
            pl.BlockSpec((1, 512), lambda i: (0, 0)),
            pl.BlockSpec((1, 512), lambda i: (0, 0)),
            pl.BlockSpec((1, 512), lambda i: (0, 0)),
            pl.BlockSpec((4, 128, 128), lambda i: (0, 0, 0)),
            pl.BlockSpec((4, 128, 128), lambda i: (0, 0, 0)),
        ],
        out_specs=[row_spec(d), row_spec(768), row_spec(512), row_spec(512), row_spec(1536),
                   row_spec(512)],
        out_shape=[
            jax.ShapeDtypeStruct((rows, d), bf16),
            jax.ShapeDtypeStruct((rows, 768), bf16),
            jax.ShapeDtypeStruct((rows, 512), f32),
            jax.ShapeDtypeStruct((rows, 512), bf16),
            jax.ShapeDtypeStruct((rows, 1536), f32),
            jax.ShapeDtypeStruct((rows, 512), f32),
        ],
        compiler_params=_cparams("arbitrary"),
        name="in_proj",
    )(xa, xb, g, mods, mods, cos_t, sin_t, w_pad, w2p, gbias, ln_g, ln_b, ws, bs_b)


def _sink_column(sink_ref, kvh, rows_per_head):
    r = lax.broadcasted_iota(jnp.int32, (Q_PER_KV * rows_per_head, 1), 0) // rows_per_head
    col = jnp.full(r.shape, sink_ref[kvh * Q_PER_KV], f32)
    for g in range(1, Q_PER_KV):
        col = jnp.where(r == g, sink_ref[kvh * Q_PER_KV + g], col)
    return col


def _win_attn_kernel(sink_ref, main_ref, prev_ref, next_ref, ctx_ref, o_ref, *, tiles_per_seq, seq_len):
    i = pl.program_id(0)
    tq = main_ref.shape[0]
    n_sub = tq // ATT_BLOCK
    blk0 = (i % tiles_per_seq) * n_sub
    blocks_per_seq = seq_len // ATT_BLOCK
    r = lax.broadcasted_iota(jnp.int32, (Q_PER_KV * ATT_BLOCK, ATT_BLOCK), 0) % ATT_BLOCK
    j = lax.broadcasted_iota(jnp.int32, (Q_PER_KV * ATT_BLOCK, ATT_BLOCK), 1)
    keep_prev = j >= r
    keep_next = j <= r
    n_ctx = ctx_ref.shape[0]

    def with_ones(v):
        return jnp.concatenate([v, jnp.ones_like(v)], axis=1)

    for kvh in range(N_KV_HEADS):
        kc = 512 + kvh * HEAD_DIM
        vc = 640 + kvh * HEAD_DIM
        k_all = jnp.concatenate([prev_ref[:, kc:kc + HEAD_DIM], main_ref[:, kc:kc + HEAD_DIM],
                                 next_ref[:, kc:kc + HEAD_DIM]], axis=0)
        v_all = with_ones(jnp.concatenate([prev_ref[:, vc:vc + HEAD_DIM], main_ref[:, vc:vc + HEAD_DIM],
                                           next_ref[:, vc:vc + HEAD_DIM]], axis=0))
        k_ctx = ctx_ref[:, kc:kc + HEAD_DIM]
        v_ctx = with_ones(ctx_ref[:, vc:vc + HEAD_DIM])
        sink = _sink_column(sink_ref, kvh, ATT_BLOCK)
        for sb in range(n_sub):
            rows = slice(sb * ATT_BLOCK, (sb + 1) * ATT_BLOCK)
            q = jnp.concatenate(
                [main_ref[rows, (kvh * Q_PER_KV + g) * HEAD_DIM:(kvh * Q_PER_KV + g + 1) * HEAD_DIM]
                 for g in range(Q_PER_KV)], axis=0)
            band = slice(sb * ATT_BLOCK, (sb + 3) * ATT_BLOCK)
            s_band = _dot(q, k_all[band], _NT)
            s_ctx = _dot(q, k_ctx, _NT)
            blk = blk0 + sb
            parts = [jnp.where(keep_prev & (blk >= 1), s_band[:, 0:ATT_BLOCK], -1e30),
                     s_band[:, ATT_BLOCK:2 * ATT_BLOCK],
                     jnp.where(keep_next & (blk <= blocks_per_seq - 2), s_band[:, 2 * ATT_BLOCK:], -1e30)]
            parts += [s_ctx[:, c:c + 128] for c in range(0, n_ctx, 128)]
            m = jnp.maximum(sink, jnp.max(functools.reduce(jnp.maximum, parts), axis=-1, keepdims=True))
            p = jnp.concatenate([jnp.exp(x - m).astype(bf16) for x in parts], axis=1)
            o_sum = (_dot(p[:, 0:3 * ATT_BLOCK], v_all[band]) + _dot(p[:, 3 * ATT_BLOCK:], v_ctx))
            denom = jnp.exp(sink - m) + o_sum[:, HEAD_DIM:HEAD_DIM + 1]
            o = o_sum[:, 0:HEAD_DIM] / denom
            for g in range(Q_PER_KV):
                c0 = (kvh * Q_PER_KV + g) * HEAD_DIM
                o_ref[rows, c0:c0 + HEAD_DIM] = o[g * ATT_BLOCK:(g + 1) * ATT_BLOCK].astype(o_ref.dtype)


def _window_attention(qkv, sink, n_batch, seq_len, n_ctx):
    tq = ROW_TILE
    n_lat = n_batch * seq_len
    tiles_per_seq = seq_len // tq
    sub = tq // ATT_BLOCK
    n_blocks = n_lat // ATT_BLOCK
    ctx_blk0 = n_lat // n_ctx
    w = qkv.shape[1]
    kern = functools.partial(_win_attn_kernel, tiles_per_seq=tiles_per_seq, seq_len=seq_len)
    return pl.pallas_call(
        kern,
        grid=(n_lat // tq,),
        in_specs=[
            pl.BlockSpec(memory_space=pltpu.SMEM),
            pl.BlockSpec((tq, w), lambda i: (i, 0)),
            pl.BlockSpec((ATT_BLOCK, w), lambda i: (jnp.maximum(i * sub - 1, 0), 0)),
            pl.BlockSpec((ATT_BLOCK, w), lambda i: (jnp.minimum(i * sub + sub, n_blocks - 1), 0)),
            pl.BlockSpec((n_ctx, w), lambda i: (ctx_blk0 + i // tiles_per_seq, 0)),
        ],
        out_specs=pl.BlockSpec((tq, 512), lambda i: (i, 0)),
        out_shape=jax.ShapeDtypeStruct((n_lat, 512), bf16),
        compiler_params=_cparams("arbitrary"),
        name="window_attention",
    )(sink, qkv, qkv, qkv, qkv)


def _ctx_attn_kernel(sink_ref, qkv_ref, o_ref):
    n = qkv_ref.shape[0]
    for kvh in range(N_KV_HEADS):
        kc = 512 + kvh * HEAD_DIM
        vc = 640 + kvh * HEAD_DIM
        q = jnp.concatenate(
            [qkv_ref[:, (kvh * Q_PER_KV + g) * HEAD_DIM:(kvh * Q_PER_KV + g + 1) * HEAD_DIM]
             for g in range(Q_PER_KV)], axis=0)
        s = _dot(q, qkv_ref[:, kc:kc + HEAD_DIM], _NT)
        sink = _sink_column(sink_ref, kvh, n)
        m = jnp.maximum(sink, jnp.max(s, axis=-1, keepdims=True))
        p = jnp.exp(s - m)
        denom = jnp.exp(sink - m) + jnp.sum(p, axis=-1, keepdims=True)
        o = _dot(p.astype(bf16), qkv_ref[:, vc:vc + HEAD_DIM]) / denom
        for g in range(Q_PER_KV):
            c0 = (kvh * Q_PER_KV + g) * HEAD_DIM
            o_ref[:, c0:c0 + HEAD_DIM] = o[g * n:(g + 1) * n].astype(o_ref.dtype)


def _context_attention(qkv, sink, n_batch, n_lat, n_ctx):
    w = qkv.shape[1]
    blk0 = n_lat // n_ctx
    return pl.pallas_call(
        _ctx_attn_kernel,
        grid=(n_batch,),
        in_specs=[pl.BlockSpec(memory_space=pltpu.SMEM),
                  pl.BlockSpec((n_ctx, w), lambda b: (blk0 + b, 0))],
        out_specs=pl.BlockSpec((n_ctx, 512), lambda b: (b, 0)),
        out_shape=jax.ShapeDtypeStruct((n_batch * n_ctx, 512), bf16),
        compiler_params=_cparams("arbitrary"),
        name="context_attention",
    )(sink, qkv)


def _pool_kernel(main_ref, prev_ref, next_ref, pw_ref, ps_ref, o_ref, xe_ref, *, tiles_per_seq, seq_len):
    i = pl.program_id(0)
    tp = main_ref.shape[0]
    t_in_seq = i % tiles_per_seq
    h = POOL_HALO
    xe_ref[0:h, :] = jnp.where(t_in_seq == 0, 0.0, prev_ref[...])
    xe_ref[h:h + tp, :] = main_ref[...]
    xe_ref[h + tp:2 * h + tp, :] = jnp.where(t_in_seq == tiles_per_seq - 1, 0.0, next_ref[...])
    pos = t_in_seq * tp + lax.broadcasted_iota(jnp.int32, (tp, 1), 0)
    for gi, w in enumerate(POOL_WINDOWS):
        cols = slice(gi * 128, (gi + 1) * 128)
        acc = xe_ref[h - w // 2:h - w // 2 + tp, cols]
        for u in range(-w // 2 + 1, w // 2):
            acc = acc + xe_ref[h + u:h + u + tp, cols]
        lo = jnp.maximum(pos - w // 2, 0)
        hi = jnp.minimum(pos + w // 2, seq_len)
        cnt = (hi - lo).astype(f32)
        pooled = acc / cnt - main_ref[:, cols]
        y = _dot(pooled.astype(bf16), pw_ref[gi])
        o_ref[:, cols] = (y * ps_ref[:, cols]).astype(o_ref.dtype)


def _multi_scale_pool(z, row0, n_rows, seq_len, tp, p_w, p_scale):
    tiles_per_seq = seq_len // tp
    t0 = row0 // tp
    h0 = row0 // POOL_HALO
    hb = tp // POOL_HALO
    n_halo = n_rows // POOL_HALO
    kern = functools.partial(_pool_kernel, tiles_per_seq=tiles_per_seq, seq_len=seq_len)
    return pl.pallas_call(
        kern,
        grid=(n_rows // tp,),
        in_specs=[
            pl.BlockSpec((tp, 512), lambda i: (t0 + i, 0)),
            pl.BlockSpec((POOL_HALO, 512), lambda i: (h0 + jnp.maximum(i * hb - 1, 0), 0)),
            pl.BlockSpec((POOL_HALO, 512), lambda i: (h0 + jnp.minimum((i + 1) * hb, n_halo - 1), 0)),
            pl.BlockSpec((4, 128, 128), lambda i: (0, 0, 0)),
            pl.BlockSpec((1, 512), lambda i: (0, 0)),
        ],
        out_specs=pl.BlockSpec((tp, 512), lambda i: (i, 0)),
        out_shape=jax.ShapeDtypeStruct((n_rows, 512), bf16),
        scratch_shapes=[pltpu.VMEM((tp + 2 * POOL_HALO, 512), f32)],
        compiler_params=_cparams("arbitrary"),
        name="multi_scale_pool",
    )(z, z, z, p_w, p_scale)


GLA_SEGMENTS = (1, 2, 4, 8, 16, 32, 64)


def _gla_tables():
    c = GLA_CHUNK
    i = np.arange(c)[:, None]
    j = np.arange(c)[None, :]
    tri = (j <= i).astype(np.float32)
    mask = np.zeros((2, 7, c, c), np.float32)
    mask[:, 0] = np.eye(c)
    for lv, s in enumerate(GLA_SEGMENTS[:-1]):
        m = ((i // (2 * s)) == (j // (2 * s))) & ((i // s) % 2 == 1) & ((j // s) % 2 == 0)
        mask[0, 1 + lv] = m
        mask[1, 1 + lv] = m.T
    mask = np.tile(mask, (1, 1, 1, GLA_HEADS))
    return tri, mask


def _segment_sums(g, cum):
    c = GLA_CHUNK
    row = lax.broadcasted_iota(jnp.int32, (c, 1), 0)
    zero = jnp.zeros_like(g)
    before = [None] + [pltpu.roll(g, kk, 0) for kk in (1, 2, 3)]
    after = [None] + [pltpu.roll(g, c - kk, 0) for kk in (1, 2, 3)]
    a, r = {1: g}, {1: zero}
    for s in (2, 4):
        pos = row % s
        a_s, r_s = g, zero
        for kk in range(1, s):
            a_s = a_s + jnp.where(pos >= kk, before[kk], 0.0)
            r_s = r_s + jnp.where(pos < s - kk, after[kk], 0.0)
        a[s], r[s] = a_s, r_s
    blocks = [cum[8 * b:8 * b + 8] for b in range(c // 8)]
    last = [cum[8 * b + 7:8 * b + 8] for b in range(c // 8)]
    for s in (8, 16, 32, 64):
        nb = s // 8
        a_blk, r_blk = [], []
        for b in range(c // 8):
            prev_end = (b // nb) * nb - 1
            a_blk.append(blocks[b] - last[prev_end] if prev_end >= 0 else blocks[b])
            r_blk.append(last[(b // nb + 1) * nb - 1] - blocks[b])
        a[s] = jnp.concatenate(a_blk, axis=0)
        r[s] = jnp.concatenate(r_blk, axis=0)
    return a, r


def _gla_chunk(d, q, k, v, g, tri_ref, mask_ref, st_ref):
    c = GLA_CHUNK
    w = g.shape[1]
    g_hi = g.astype(bf16)
    r1 = g - g_hi.astype(f32)
    g_mid = r1.astype(bf16)
    g_lo = (r1 - g_mid.astype(f32)).astype(bf16)
    cum3 = _dot(tri_ref[...], jnp.concatenate([g_hi, g_mid, g_lo], axis=1))
    cum = cum3[:, 0:w] + cum3[:, w:2 * w] + cum3[:, 2 * w:3 * w]
    a, r = _segment_sums(g, cum)
    if d == 0:
        cq = [a[s] for s in GLA_SEGMENTS]
        ck = [r[s] for s in GLA_SEGMENTS]
    else:
        cq = [r[s] + g for s in GLA_SEGMENTS]
        ck = [a[s] - g for s in GLA_SEGMENTS]
    total = cum[c - 1:c]

    q = q * (GLA_DK ** -0.5)
    head_of_lane = lax.broadcasted_iota(jnp.int32, (1, w), 1) // GLA_DK
    vhead_of_lane = lax.broadcasted_iota(jnp.int32, (1, v.shape[1]), 1) // GLA_DV

    def stack_heads(x, lane_head):
        return jnp.concatenate([jnp.where(lane_head == h, x, jnp.zeros_like(x))
                                for h in range(GLA_HEADS)], axis=0)

    kb = k.astype(bf16)
    k_plain = stack_heads(kb, head_of_lane)
    q01 = jnp.concatenate([q.astype(bf16), (q * jnp.exp(cq[0])).astype(bf16)], axis=0)
    s01 = _dot(q01, k_plain, _NT)
    att = mask_ref[d, 0] * s01[0:c] + mask_ref[d, 1] * s01[c:2 * c]
    for lv in range(1, 6):
        qs = (q * jnp.exp(cq[lv])).astype(bf16)
        ks = stack_heads((k * jnp.exp(ck[lv])).astype(bf16), head_of_lane)
        att = att + mask_ref[d, 1 + lv] * _dot(qs, ks, _NT)
    v_heads = stack_heads(v.astype(bf16), vhead_of_lane)
    st = st_ref[d]
    q_state = (q * jnp.exp(cq[6])).astype(bf16)
    o = _dot(att.astype(bf16), v_heads) + _dot(q_state, st.astype(bf16), _NT)
    k_state = stack_heads((k * jnp.exp(ck[6])).astype(bf16), head_of_lane)
    st_ref[d] = st * jnp.exp(total) + _dot(v_heads, k_state, _TN)
    return o


def _gla_kernel(qf_ref, kf_ref, vf_ref, gf_ref, qb_ref, kb_ref, vb_ref, gb_ref,
                tri_ref, mask_ref, of_ref, ob_ref, st_ref):
    @pl.when(pl.program_id(1) == 0)
    def _():
        st_ref[...] = jnp.zeros_like(st_ref)

    n_chunks = qf_ref.shape[0] // GLA_CHUNK
    for ci in range(n_chunks):
        rf = slice(ci * GLA_CHUNK, (ci + 1) * GLA_CHUNK)
        of_ref[rf, :] = _gla_chunk(0, qf_ref[rf, :], kf_ref[rf, :], vf_ref[rf, :], gf_ref[rf, :],
                                   tri_ref, mask_ref, st_ref)
        cb = n_chunks - 1 - ci
        rb = slice(cb * GLA_CHUNK, (cb + 1) * GLA_CHUNK)
        ob_ref[rb, :] = _gla_chunk(1, qb_ref[rb, :], kb_ref[rb, :], vb_ref[rb, :], gb_ref[rb, :],
                                   tri_ref, mask_ref, st_ref)


def _gla_scan(z_gla, gates, n_batch, seq_len, n_ctx):
    rows = z_gla.shape[0]
    tb = GLA_BLOCK
    assert n_ctx == tb
    lat_blocks = seq_len // tb
    ctx_blk0 = n_batch * lat_blocks
    tri, mask = _gla_tables()

    def fwd_row(b, s):
        return jnp.where(s == 0, ctx_blk0 + b, b * lat_blocks + s - 1)

    def bwd_row(b, s):
        return jnp.where(s == 0, ctx_blk0 + b, b * lat_blocks + lat_blocks - s)

    def specs(row, direction):
        return [
            pl.BlockSpec((tb, 256), lambda b, s: (row(b, s), 0)),
            pl.BlockSpec((tb, 256), lambda b, s: (row(b, s), 1)),
            pl.BlockSpec((tb, 512), lambda b, s: (row(b, s), 1)),
            pl.BlockSpec((tb, 256), lambda b, s: (row(b, s), direction)),
        ]

    const = lambda shape: pl.BlockSpec(shape, lambda b, s: (0,) * len(shape))
    return pl.pallas_call(
        _gla_kernel,
        grid=(n_batch, 1 + lat_blocks),
        in_specs=specs(fwd_row, 0) + specs(bwd_row, 1) + [
            const(tri.shape), const(mask.shape)],
        out_specs=[pl.BlockSpec((tb, 512), lambda b, s: (fwd_row(b, s), 0)),
                   pl.BlockSpec((tb, 512), lambda b, s: (bwd_row(b, s), 0))],
        out_shape=[jax.ShapeDtypeStruct((rows, 512), f32)] * 2,
        scratch_shapes=[pltpu.VMEM((2, GLA_HEADS * GLA_DV, GLA_HEADS * GLA_DK), f32)],
        compiler_params=_cparams("arbitrary", "arbitrary"),
        name="gla_scan",
    )(z_gla, z_gla, z_gla, gates, z_gla, z_gla, z_gla, gates,
      jnp.asarray(tri, bf16), jnp.asarray(mask, f32))


def _merge_kernel(h_ref, a_ref, p_ref, s_ref, of_ref, ob_ref, r_ref, ng_ref, wg_ref, wb_ref, o_ref, g_ref):
    @pl.when(pl.program_id(1) == 0)
    def _():
        o = of_ref[...] + ob_ref[...]
        r = r_ref[...]
        gate = r * jax.nn.sigmoid(r)
        for h in range(GLA_HEADS):
            cols = slice(h * GLA_DV, (h + 1) * GLA_DV)
            oh = o[:, cols]
            oh = oh * lax.rsqrt(jnp.mean(oh * oh, axis=-1, keepdims=True) + EPS)
            g_ref[:, cols] = (oh * ng_ref[:, cols] * gate[:, cols]).astype(g_ref.dtype)

    hb = h_ref[...]
    acc = None
    for br, b_ref in enumerate((a_ref, p_ref, s_ref, g_ref)):
        gate = jax.nn.sigmoid(_dot(hb, wg_ref[br]))
        term = gate * _dot(b_ref[...], wb_ref[br])
        acc = term if acc is None else acc + term
    o_ref[...] = acc.astype(o_ref.dtype)


def _gated_merge(h, branches, o_f, o_b, z_gla, norm_g, n_rows, layer, wg, wb):
    d = h.shape[1]
    tm, tn = ROW_TILE, 512
    bspec = pl.BlockSpec((tm, 512), lambda i, j: (i, 0))
    return pl.pallas_call(
        _merge_kernel,
        grid=(n_rows // tm, d // tn),
        in_specs=[pl.BlockSpec((tm, d), lambda i, j: (i, 0)), bspec, bspec, bspec,
                  bspec, bspec,
                  pl.BlockSpec((tm, 512), lambda i, j: (i, 2)),
                  pl.BlockSpec((1, 512), lambda i, j: (0, 0)),
                  pl.BlockSpec((None, 4, d, tn), lambda i, j: (layer, 0, 0, j)),
                  pl.BlockSpec((None, 4, 512, tn), lambda i, j: (layer, 0, 0, j))],
        out_specs=pl.BlockSpec((tm, tn), lambda i, j: (i, j)),
        out_shape=jax.ShapeDtypeStruct((n_rows, d), bf16),
        scratch_shapes=[pltpu.VMEM((tm, 512), bf16)],
        compiler_params=_cparams("arbitrary", "arbitrary"),
        name="gated_merge",
    )(h, *branches, o_f, o_b, z_gla, norm_g, wg, wb)


def _out_proj_kernel(acc_ref, wo_ref, xa_ref, xb_ref, gate_ref, o_ref, *, n_a_tiles):
    x = _two_source_rows(xa_ref, xb_ref, n_a_tiles)
    o_ref[...] = x + gate_ref[...] * _dot(acc_ref[...], wo_ref[...])


def _out_proj_residual(acc, layer, wo, xa, xb, n_a_tiles, b_tile0, mods, n_rows, tiles_per_mod):
    d = acc.shape[1]
    tm = ROW_TILE
    return pl.pallas_call(
        functools.partial(_out_proj_kernel, n_a_tiles=n_a_tiles),
        grid=(n_rows // tm,),
        in_specs=[pl.BlockSpec((tm, d), lambda i: (i, 0)),
                  pl.BlockSpec((None, d, d), lambda i: (layer, 0, 0))]
        + _two_source_specs(tm, d, n_a_tiles, b_tile0)
        + [pl.BlockSpec((None, 1, d), lambda i: (i // tiles_per_mod, 0, 2))],
        out_specs=pl.BlockSpec((tm, d), lambda i: (i, 0)),
        out_shape=jax.ShapeDtypeStruct((n_rows, d), f32),
        compiler_params=_cparams("arbitrary"),
        name="out_proj_residual",
    )(acc, wo, xa, xb, mods)


def _router_kernel(x_ref, g_ref, sh_ref, sc_ref, rw_ref, rb_ref, tri_ref, info_ref, cnt_ref, carry_ref):
    i = pl.program_id(0)

    @pl.when(i == 0)
    def _():
        carry_ref[...] = jnp.zeros_like(carry_ref)

    h2 = _norm_mod(x_ref[...], g_ref[...], sh_ref[...], sc_ref[...])
    tm = h2.shape[0]
    h_hi = h2.astype(bf16)
    h_lo = (h2 - h_hi.astype(f32)).astype(bf16)
    part = _dot(rw_ref[...], h_hi, _NT)
    logits = (part[0:N_EXPERTS] + part[N_EXPERTS:2 * N_EXPERTS]
              + _dot(rw_ref[0:N_EXPERTS, :], h_lo, _NT))
    score = jax.nn.sigmoid(logits)
    biased = score + rb_ref[...]
    b = [biased[e:e + 1] for e in range(N_EXPERTS)]
    sc = [score[e:e + 1] for e in range(N_EXPERTS)]

    def top2_sum(v):
        hi01, lo01 = jnp.maximum(v[0], v[1]), jnp.minimum(v[0], v[1])
        hi23, lo23 = jnp.maximum(v[2], v[3]), jnp.minimum(v[2], v[3])
        return jnp.maximum(hi01, hi23) + jnp.maximum(jnp.minimum(hi01, hi23), jnp.maximum(lo01, lo23))

    best = jnp.zeros((1, tm), jnp.int32)
    best_score = top2_sum(b[0:4])
    for gidx in range(1, N_GROUPS):
        gs = top2_sum(b[4 * gidx:4 * gidx + 4])
        take = gs > best_score
        best = jnp.where(take, gidx, best)
        best_score = jnp.where(take, gs, best_score)
    cb, cs = [], []
    for kk in range(EXPERTS_PER_GROUP):
        vb, vs = b[kk], sc[kk]
        for gidx in range(1, N_GROUPS):
            vb = jnp.where(best == gidx, b[4 * gidx + kk], vb)
            vs = jnp.where(best == gidx, sc[4 * gidx + kk], vs)
        cb.append(vb)
        cs.append(vs)
    i1 = jnp.zeros((1, tm), jnp.int32)
    m1, w1 = cb[0], cs[0]
    for kk in range(1, EXPERTS_PER_GROUP):
        take = cb[kk] > m1
        i1 = jnp.where(take, kk, i1)
        m1 = jnp.where(take, cb[kk], m1)
        w1 = jnp.where(take, cs[kk], w1)
    i2 = jnp.where(i1 == 0, 1, 0).astype(jnp.int32)
    m2 = jnp.where(i1 == 0, cb[1], cb[0])
    w2 = jnp.where(i1 == 0, cs[1], cs[0])
    for kk in range(1, EXPERTS_PER_GROUP):
        take = (cb[kk] > m2) & (i1 != kk) & (i2 != kk)
        i2 = jnp.where(take, kk, i2)
        m2 = jnp.where(take, cb[kk], m2)
        w2 = jnp.where(take, cs[kk], w2)
    wsum = w1 + w2
    first_is_lo = i1 < i2
    lo = jnp.minimum(i1, i2)
    hi = jnp.maximum(i1, i2)
    pair = jnp.where(lo == 0, hi - 1, jnp.where(lo == 1, hi + 1, 5))
    cls = best * PAIRS_PER_GROUP + pair
    w_lo = jnp.where(first_is_lo, w1, w2) / wsum
    w_hi = jnp.where(first_is_lo, w2, w1) / wsum
    crow = lax.broadcasted_iota(jnp.int32, (N_CLASS_ROWS, tm), 0)
    oh = (crow == cls).astype(f32)
    rank = _dot(oh.astype(bf16), tri_ref[...]) + carry_ref[:, 0:1]
    carry_ref[...] = carry_ref[...] + jnp.sum(oh, axis=1, keepdims=True)
    info_ref[...] = jnp.concatenate(
        [cls.astype(f32), w_lo, w_hi, jnp.sum(oh * rank, axis=0, keepdims=True),
         jnp.zeros((4, tm), f32)], axis=0)
    cnt_ref[...] = carry_ref[...]


def _router(x_all, n_rows, g, mods, rw_t, rb_col, tiles_per_mod):
    d = x_all.shape[1]
    tm = ROW_TILE
    tri = jnp.asarray(np.triu(np.ones((tm, tm), np.float32), k=1), bf16)
    mod_spec = lambda k: pl.BlockSpec((None, 1, d), lambda i: (i // tiles_per_mod, 0, k))
    return pl.pallas_call(
        _router_kernel,
        grid=(n_rows // tm,),
        in_specs=[pl.BlockSpec((tm, d), lambda i: (i, 0)),
                  pl.BlockSpec((1, d), lambda i: (0, 0)),
                  mod_spec(3), mod_spec(4),
                  pl.BlockSpec((2 * N_EXPERTS, d), lambda i: (0, 0)),
                  pl.BlockSpec((N_EXPERTS, 1), lambda i: (0, 0)),
                  pl.BlockSpec((tm, tm), lambda i: (0, 0))],
        out_specs=[pl.BlockSpec((8, tm), lambda i: (0, i)),
                   pl.BlockSpec((N_CLASS_ROWS, 128), lambda i: (0, 0))],
        out_shape=[jax.ShapeDtypeStruct((8, n_rows), f32),
                   jax.ShapeDtypeStruct((N_CLASS_ROWS, 128), f32)],
        scratch_shapes=[pltpu.VMEM((N_CLASS_ROWS, 128), f32)],
        compiler_params=_cparams("arbitrary"),
        name="moe_router",
    )(x_all, g, mods, mods, rw_t, rb_col, tri)


def _dispatch_kernel(pos_ref, zpos_ref, x_ref, w_ref, g_ref, sh_ref, sc_ref, xs_ref, buf_ref, zero_ref,
                     sem, zsem):
    i = pl.program_id(0)
    tm, d = x_ref.shape

    @pl.when(i == 0)
    def _():
        zero_ref[...] = jnp.zeros_like(zero_ref)

        def zero_tile(row):
            cp = pltpu.make_async_copy(zero_ref, xs_ref.at[pl.ds(pl.multiple_of(row, 8), tm)], zsem)
            cp.start()
            cp.wait()

        for c in range(N_CLASSES):
            zero_tile((zpos_ref[c] // 8) * 8)

        def tail(t, c):
            zero_tile(t * tm)
            return c

        lax.fori_loop(zpos_ref[N_CLASSES], xs_ref.shape[0] // tm, tail, 0)

    slot = i % 2

    def wait_slot(s):
        pltpu.make_async_copy(buf_ref.at[s], xs_ref.at[pl.ds(0, tm)], sem.at[s]).wait()

    @pl.when(i >= 2)
    def _():
        wait_slot(slot)

    buf_ref[slot, :, 0:d] = _norm_mod(x_ref[...], g_ref[...], sh_ref[...], sc_ref[...])
    w = w_ref[...]
    lane = lax.broadcasted_iota(jnp.int32, (tm, 128), 1)
    buf_ref[slot, :, d:d + 128] = jnp.where(lane == 0, w[:, 0:1], jnp.where(lane == 1, w[:, 1:2], 0.0))
    src = buf_ref.at[slot]
    for r in range(tm):
        dst = pos_ref[i * tm + r]
        pltpu.make_async_copy(src.at[pl.ds(r, 1)], xs_ref.at[pl.ds(dst, 1)], sem.at[slot]).start()

    @pl.when(i == pl.num_programs(0) - 1)
    def _():
        wait_slot(slot)
        wait_slot(1 - slot)


def _dispatch(pos, zpos, x_all, w_tok, n_rows, g, mods, tiles_per_mod, p_rows):
    d = x_all.shape[1]
    dx = d + 128
    tm = MOE_TILE
    tpm = tiles_per_mod * (ROW_TILE // tm)
    mod_spec = lambda k: pl.BlockSpec((None, 1, d), lambda i, pos, zpos: (i // tpm, 0, k))
    return pl.pallas_call(
        _dispatch_kernel,
        grid_spec=pltpu.PrefetchScalarGridSpec(
            num_scalar_prefetch=2,
            grid=(n_rows // tm,),
            in_specs=[pl.BlockSpec((tm, d), lambda i, pos, zpos: (i, 0)),
                      pl.BlockSpec((tm, 2), lambda i, pos, zpos: (i, 0)),
                      pl.BlockSpec((1, d), lambda i, pos, zpos: (0, 0)),
                      mod_spec(3), mod_spec(4)],
            out_specs=pl.BlockSpec(memory_space=pl.ANY),
            scratch_shapes=[pltpu.VMEM((2, tm, dx), f32), pltpu.VMEM((tm, dx), f32),
                            pltpu.SemaphoreType.DMA((2,)), pltpu.SemaphoreType.DMA(())],
        ),
        out_shape=jax.ShapeDtypeStruct((p_rows, dx), f32),
        compiler_params=_cparams("arbitrary"),
        name="moe_dispatch",
    )(pos, zpos, x_all, w_tok, g, mods, mods)


def _expert_kernel(te_ref, ws_ref, nu_ref, xs_ref, wg_ref, wu_ref, wd_ref, y_ref):
    i = pl.program_id(0)
    j = pl.program_id(1)
    d = y_ref.shape[1]

    @pl.when(i < nu_ref[0])
    def _():
        xb = xs_ref[:, 0:d].astype(bf16)
        a = _dot(xb, wg_ref[...])
        act = (a * jax.nn.sigmoid(a)) * _dot(xb, wu_ref[...])
        wcol = jnp.where(ws_ref[2 * i + j] == 0, xs_ref[:, d:d + 1], xs_ref[:, d + 1:d + 2])
        part = wcol * _dot(act.astype(bf16), wd_ref[...])

        @pl.when(j == 0)
        def _():
            y_ref[...] = part

        @pl.when(j == 1)
        def _():
            y_ref[...] = y_ref[...] + part

    @pl.when(i >= nu_ref[0])
    def _():
        y_ref[...] = jnp.zeros_like(y_ref)


def _expert_ffn(step_expert, step_wsel, n_used, xs, layer, wg, wu, wd):
    p_rows, dx = xs.shape
    d, de = wg.shape[2], wg.shape[3]
    tm = MOE_TILE
    wspec = lambda a, b: pl.BlockSpec((None, None, a, b), lambda i, j, te, ws, nu: (layer, te[2 * i + j], 0, 0))
    return pl.pallas_call(
        _expert_kernel,
        grid_spec=pltpu.PrefetchScalarGridSpec(
            num_scalar_prefetch=3,
            grid=(p_rows // tm, 2),
            in_specs=[pl.BlockSpec((tm, dx), lambda i, j, te, ws, nu: (jnp.where(i < nu[0], i, 0), 0)),
                      wspec(d, de), wspec(d, de), wspec(de, d)],
            out_specs=pl.BlockSpec((tm, d), lambda i, j, te, ws, nu: (i, 0)),
        ),
        out_shape=jax.ShapeDtypeStruct((p_rows, d), f32),
        compiler_params=_cparams("arbitrary", "arbitrary"),
        name="moe_experts",
    )(step_expert, step_wsel, n_used, xs, wg, wu, wd)


def _combine_kernel(pos_ref, x_ref, gate_ref, fg_ref, y_ref, o_ref, ybuf_ref, sem, *, final_norm):
    i = pl.program_id(0)
    tm = x_ref.shape[0]
    slot = i % 2

    def gather(step, s):
        for r in range(tm):
            src = pos_ref[step * tm + r]
            pltpu.make_async_copy(y_ref.at[pl.ds(src, 1)], ybuf_ref.at[s].at[pl.ds(r, 1)],
                                  sem.at[s]).start()

    @pl.when(i == 0)
    def _():
        gather(0, 0)

    @pl.when(i + 1 < pl.num_programs(0))
    def _():
        gather(i + 1, 1 - slot)

    pltpu.make_async_copy(y_ref.at[pl.ds(0, tm)], ybuf_ref.at[slot], sem.at[slot]).wait()
    x = x_ref[...] + gate_ref[...] * ybuf_ref[slot]
    if final_norm:
        x = x * lax.rsqrt(jnp.mean(x * x, axis=-1, keepdims=True) + EPS) * fg_ref[...]
    o_ref[...] = x


def _combine(pos, x_all, n_rows, mods, final_g, y, tiles_per_mod, final_norm):
    d = x_all.shape[1]
    tm = MOE_TILE
    tpm = tiles_per_mod * (ROW_TILE // tm)
    kern = functools.partial(_combine_kernel, final_norm=final_norm)
    return pl.pallas_call(
        kern,
        grid_spec=pltpu.PrefetchScalarGridSpec(
            num_scalar_prefetch=1,
            grid=(n_rows // tm,),
            in_specs=[pl.BlockSpec((tm, d), lambda i, pos: (i, 0)),
                      pl.BlockSpec((None, 1, d), lambda i, pos: (i // tpm, 0, 5)),
                      pl.BlockSpec((1, d), lambda i, pos: (0, 0)),
                      pl.BlockSpec(memory_space=pl.ANY)],
            out_specs=pl.BlockSpec((tm, d), lambda i, pos: (i, 0)),
            scratch_shapes=[pltpu.VMEM((2, tm, d), f32), pltpu.SemaphoreType.DMA((2,))],
        ),
        out_shape=jax.ShapeDtypeStruct((n_rows, d), f32),
        compiler_params=_cparams("arbitrary"),
        name="moe_combine",
    )(pos, x_all, mods, final_g, y)


def _moe(x_all, n_rows, norm_g, mods, tiles_per_mod, rw_t, rb_col, layer, wg, wu, wd, final_g, final_norm):
    info, counts = _router(x_all, n_rows, norm_g, mods, rw_t, rb_col, tiles_per_mod)
    tm = MOE_TILE
    p_rows = n_rows + (N_CLASSES + 1) * tm
    n_tiles = p_rows // tm
    cnt = counts[0:N_CLASSES, 0].astype(jnp.int32)
    padded = ((cnt + tm - 1) // tm) * tm
    ends = jnp.cumsum(padded)
    starts = ends - padded
    cls = info[0].astype(jnp.int32)
    pos = starts[cls] + info[3].astype(jnp.int32)
    n_used = (ends[-1] // tm).astype(jnp.int32)
    tile = jnp.arange(n_tiles, dtype=jnp.int32)
    tile_cls = jnp.sum((ends[None, :] <= (tile * tm)[:, None]).astype(jnp.int32), axis=1)
    tile_cls = jnp.minimum(tile_cls, N_CLASSES - 1)
    pair = tile_cls % PAIRS_PER_GROUP
    e_lo = (tile_cls // PAIRS_PER_GROUP) * EXPERTS_PER_GROUP + jnp.asarray(PAIR_LO, jnp.int32)[pair]
    e_hi = (tile_cls // PAIRS_PER_GROUP) * EXPERTS_PER_GROUP + jnp.asarray(PAIR_HI, jnp.int32)[pair]
    flip = (tile % 2) == 1
    step_expert = jnp.stack([jnp.where(flip, e_hi, e_lo), jnp.where(flip, e_lo, e_hi)], axis=1)
    step_wsel = jnp.stack([flip, ~flip], axis=1).astype(jnp.int32)
    last_e = step_expert.reshape(-1)[jnp.maximum(2 * n_used - 1, 0)]
    step_expert = jnp.where((tile < n_used)[:, None], step_expert, last_e).reshape(-1)
    zpos = jnp.concatenate([starts + cnt, n_used.reshape(1)])
    w_tok = jnp.transpose(info[1:3])
    xs = _dispatch(pos, zpos, x_all, w_tok, n_rows, norm_g, mods, tiles_per_mod, p_rows)
    y = _expert_ffn(step_expert, step_wsel.reshape(-1), n_used.reshape(1), xs, layer, wg, wu, wd)
    return _combine(pos, x_all, n_rows, mods, final_g, y, tiles_per_mod, final_norm)


def _rope_tables(seq_len, n_batch, n_ctx_rows):
    rows = seq_len // GRID_W
    row = jnp.repeat(jnp.arange(rows), GRID_W)
    col = jnp.tile(jnp.arange(GRID_W), rows)
    nf = HEAD_DIM // 4
    inv_freq = ROPE_BASE ** (-jnp.arange(nf, dtype=f32) / nf)
    ang_r = row[:, None].astype(f32) * inv_freq
    ang_c = col[:, None].astype(f32) * inv_freq
    cos64 = jnp.concatenate([jnp.cos(ang_r)] * 2 + [jnp.cos(ang_c)] * 2, axis=-1)
    sin64 = jnp.concatenate([-jnp.sin(ang_r), jnp.sin(ang_r), -jnp.sin(ang_c), jnp.sin(ang_c)], axis=-1)
    cos_t = jnp.tile(jnp.concatenate([cos64, cos64], axis=-1), (n_batch, 1))
    sin_t = jnp.tile(jnp.concatenate([sin64, sin64], axis=-1), (n_batch, 1))
    cos_t = jnp.concatenate([cos_t, jnp.ones((n_ctx_rows, 128), f32)], axis=0)
    sin_t = jnp.concatenate([sin_t, jnp.zeros((n_ctx_rows, 128), f32)], axis=0)
    return cos_t, sin_t


def kernel(x, c, ctx, c_ctx, w_mod, b_mod, norm1_g, norm2_g, final_norm_g, w_in, a_sink,
           p_w, p_scale, c_ln_g, c_ln_b, c_ws, c_bs, g_w2, g_b, g_norm_g,
           w_branch, w_gate, w_out, router_w, router_b, e_gate, e_up, e_down):
    n_batch, seq_len, d = x.shape
    n_ctx = ctx.shape[1]
    depth = w_mod.shape[0]
    n_lat = n_batch * seq_len
    n_all = n_lat + n_batch * n_ctx
    tiles_per_mod = seq_len // ROW_TILE

    cond = jnp.zeros((8, d), f32).at[0:n_batch].set(c).at[n_batch].set(c_ctx)
    mods_all = _modulation(cond, w_mod, b_mod)
    cos_t, sin_t = _rope_tables(seq_len, n_batch, n_batch * n_ctx)
    rw_f = jnp.transpose(router_w)
    rw_hi = rw_f.astype(bf16)
    rw_t = jnp.concatenate([rw_hi, (rw_f - rw_hi.astype(f32)).astype(bf16)], axis=0)
    rb_col = router_b.reshape(N_EXPERTS, 1)
    final_g = final_norm_g.reshape(1, d)

    w_in_b = jnp.pad(w_in, ((0, 0), (0, 0), (0, 96))).astype(bf16)
    w_gate_b = w_gate.astype(bf16)
    w_branch_b = w_branch.astype(bf16)
    w_out_b = w_out.astype(bf16)
    e_gate_b = e_gate.astype(bf16)
    e_up_b = e_up.astype(bf16)
    e_down_b = e_down.astype(bf16)

    n_lat_tiles = n_lat // ROW_TILE
    xa, xb, b_tile0 = x.reshape(n_lat, d), ctx.reshape(n_batch * n_ctx, d), 0
    for l in range(depth):
        last = l == depth - 1
        n_out = n_lat if last else n_all
        mods = mods_all[l].reshape(8, 1, 6 * d)
        w2p = jnp.zeros((128, 512), f32)
        w2p = w2p.at[0:GLA_RANK, 0:256].set(g_w2[l, 0]).at[GLA_RANK:2 * GLA_RANK, 256:512].set(g_w2[l, 1])
        w2_hi = w2p.astype(bf16)
        w2_split = jnp.concatenate([w2_hi, w2_hi, (w2p - w2_hi.astype(f32)).astype(bf16)], axis=0)
        bs_b = jnp.broadcast_to(c_bs[l][:, :, None], (4, SG_CHUNK, 128))
        h, qkv, z_pool, s_br, z_gla, gates = _in_proj(
            xa, xb, n_lat_tiles, b_tile0, n_all, l, norm1_g[l].reshape(1, d), mods, cos_t, sin_t,
            w_in_b, w2_split, g_b[l].reshape(1, 512), c_ln_g[l].reshape(1, 512),
            c_ln_b[l].reshape(1, 512), c_ws[l].astype(bf16), bs_b, tiles_per_mod)
        a_br = _window_attention(qkv, a_sink[l], n_batch, seq_len, n_ctx)
        pw = p_w[l].astype(bf16)
        ps = p_scale[l].reshape(1, 512)
        p_br = _multi_scale_pool(z_pool, 0, n_lat, seq_len, ROW_TILE, pw, ps)
        o_f, o_b = _gla_scan(z_gla, gates, n_batch, seq_len, n_ctx)
        if not last:
            a_c = _context_attention(qkv, a_sink[l], n_batch, n_lat, n_ctx)
            p_c = _multi_scale_pool(z_pool, n_lat, n_batch * n_ctx, n_ctx, n_ctx, pw, ps)
            a_br = jnp.concatenate([a_br, a_c], axis=0)
            p_br = jnp.concatenate([p_br, p_c], axis=0)
        acc = _gated_merge(h, (a_br, p_br, s_br), o_f, o_b, z_gla, g_norm_g[l].reshape(1, 512),
                           n_out, l, w_gate_b, w_branch_b)
        x_all = _out_proj_residual(acc, l, w_out_b, xa, xb, n_lat_tiles, b_tile0, mods, n_out,
                                   tiles_per_mod)
        x_all = _moe(x_all, n_out, norm2_g[l].reshape(1, d), mods, tiles_per_mod, rw_t, rb_col,
                     l, e_gate_b, e_up_b, e_down_b, final_g, last)
        xa, xb, b_tile0 = x_all, x_all, n_lat_tiles
    return x_all.reshape(n_batch, seq_len, d)
```

```python
import functools

import numpy as np
import jax
import jax.numpy as jnp
from jax import lax
from jax.experimental import pallas as pl
from jax.experimental.pallas import tpu as pltpu

f32 = jnp.float32
bf16 = jnp.bfloat16

EPS = 1e-6
GRID_W = 64
ROPE_BASE = 10000.0

HEAD_DIM = 64
N_Q_HEADS = 8
N_KV_HEADS = 2
Q_PER_KV = N_Q_HEADS // N_KV_HEADS
ATT_BLOCK = 128
POOL_WINDOWS = (2, 4, 8, 16)
POOL_HALO = 8
SG_CHUNK = 128
GLA_HEADS = 4
GLA_DK = 64
GLA_DV = 128
GLA_RANK = 16
GLA_TAU = 16.0
GLA_CHUNK = 64
N_EXPERTS = 16
EXPERTS_PER_GROUP = 4
N_GROUPS = 4

ROW_TILE = 512
GLA_BLOCK = 256
MOE_TILE = 256
VMEM_LIMIT = 56 * 1024 * 1024


def _cparams(*sem):
    return pltpu.CompilerParams(dimension_semantics=sem, vmem_limit_bytes=VMEM_LIMIT)


def _dot(a, b, dims=(((1,), (0,)), ((), ())), precision=None):
    return lax.dot_general(a, b, dims, precision=precision, preferred_element_type=f32)


_NT = (((1,), (1,)), ((), ()))
_TN = (((0,), (0,)), ((), ()))


def _mod_kernel(c_ref, w_ref, b_ref, o_ref):
    c = c_ref[...]
    a = c * jax.nn.sigmoid(c)
    w = w_ref[...]
    a_hi = a.astype(bf16)
    a_lo = (a - a_hi.astype(f32)).astype(bf16)
    w_hi = w.astype(bf16)
    w_lo = (w - w_hi.astype(f32)).astype(bf16)
    n = a.shape[0]
    both = _dot(jnp.concatenate([a_hi, a_lo], axis=0), w_hi)
    o_ref[...] = both[0:n] + both[n:2 * n] + _dot(a_hi, w_lo) + b_ref[...]


def _modulation(cond, w_mod, b_mod):
    n_layers, d, d6 = w_mod.shape
    tn = 1024
    return pl.pallas_call(
        _mod_kernel,
        grid=(n_layers, d6 // tn),
        in_specs=[
            pl.BlockSpec((8, d), lambda l, j: (0, 0)),
            pl.BlockSpec((None, d, tn), lambda l, j: (l, 0, j)),
            pl.BlockSpec((None, 1, tn), lambda l, j: (l, 0, j)),
        ],
        out_specs=pl.BlockSpec((None, 8, tn), lambda l, j: (l, 0, j)),
        out_shape=jax.ShapeDtypeStruct((n_layers, 8, d6), f32),
        compiler_params=_cparams("arbitrary", "arbitrary"),
        name="modulation",
    )(cond, w_mod, b_mod.reshape(n_layers, 1, d6))


def _norm_mod(x, g, shift, scale):
    y = x * lax.rsqrt(jnp.mean(x * x, axis=-1, keepdims=True) + EPS)
    return (y * g) * (1.0 + scale) + shift


def _two_source_rows(xa_ref, xb_ref, n_a_tiles):
    return jnp.where(pl.program_id(0) < n_a_tiles, xa_ref[...], xb_ref[...])


def _two_source_specs(tm, d, n_a_tiles, b_tile0):
    return [pl.BlockSpec((tm, d), lambda i: (jnp.minimum(i, n_a_tiles - 1), 0)),
            pl.BlockSpec((tm, d), lambda i: (b_tile0 + jnp.maximum(i - n_a_tiles, 0), 0),
                         pipeline_mode=pl.Buffered(1))]


def _spatial_gate_rows(uv, lg_ref, lb_ref, ws_ref, bs_ref, o_ref):
    a = jax.nn.gelu(uv)
    u = a[:, 0:512]
    v = a[:, 512:1024]
    mu = jnp.mean(v, axis=-1, keepdims=True)
    var = jnp.mean(jnp.square(v - mu), axis=-1, keepdims=True)
    vn = ((v - mu) * lax.rsqrt(var + EPS) * lg_ref[...] + lb_ref[...]).astype(bf16)
    for c in range(uv.shape[0] // SG_CHUNK):
        rows = slice(c * SG_CHUNK, (c + 1) * SG_CHUNK)
        for g in range(4):
            cols = slice(g * 128, (g + 1) * 128)
            mixed = _dot(ws_ref[g], vn[rows, cols]) + bs_ref[g]
            o_ref[rows, cols] = (u[rows, cols] * mixed).astype(o_ref.dtype)


def _in_proj_kernel(xa_ref, xb_ref, g_ref, sh_ref, sc_ref, cos_ref, sin_ref, w_ref,
                    w2_ref, gb_ref, lg_ref, lb_ref, ws_ref, bs_ref,
                    h_ref, qkv_ref, pool_ref, sg_ref, gla_ref, gate_ref, *, n_a_tiles):
    x = _two_source_rows(xa_ref, xb_ref, n_a_tiles)
    hb = _norm_mod(x, g_ref[...], sh_ref[...], sc_ref[...]).astype(bf16)
    h_ref[...] = hb

    cos = cos_ref[...]
    sin = sin_ref[...]
    lane = lax.broadcasted_iota(jnp.int32, cos.shape, 1)
    first_half = (lane % 32) < 16

    def rope(z):
        rot = jnp.where(first_half, pltpu.roll(z, 112, 1), pltpu.roll(z, 16, 1))
        return z * cos + rot * sin

    zq = _dot(hb, w_ref[:, 0:512])
    for c in range(4):
        qkv_ref[:, c * 128:(c + 1) * 128] = (
            rope(zq[:, c * 128:(c + 1) * 128]) * (HEAD_DIM ** -0.5)).astype(bf16)
    zkv = _dot(hb, w_ref[:, 512:768])
    qkv_ref[:, 512:640] = rope(zkv[:, 0:128]).astype(bf16)
    qkv_ref[:, 640:768] = zkv[:, 128:256].astype(bf16)
    pool_ref[...] = _dot(hb, w_ref[:, 768:1280])
    _spatial_gate_rows(_dot(hb, w_ref[:, 1280:2304]), lg_ref, lb_ref, ws_ref, bs_ref, sg_ref)
    gla_ref[...] = _dot(hb, w_ref[:, 2304:3840])
    low_rank = _dot(hb, w_ref[:, 3840:3968])
    lr_hi = low_rank.astype(bf16)
    lr_lo = (low_rank - lr_hi.astype(f32)).astype(bf16)
    logit = _dot(jnp.concatenate([lr_hi, lr_lo, lr_hi], axis=1), w2_ref[...]) + gb_ref[...]
    gate_ref[...] = _log_sigmoid(logit) / GLA_TAU


def _log_sigmoid(x):
    return jnp.minimum(x, 0.0) - jnp.log(1.0 + jnp.exp(-jnp.abs(x)))


def _in_proj(xa, xb, n_a_tiles, b_tile0, rows, layer, g, mods, cos_t, sin_t, w_pad, w2p, gbias,
             ln_g, ln_b, ws, bs_b, tiles_per_mod):
    d = xa.shape[1]
    n_w = w_pad.shape[2]
    tm = ROW_TILE
    mod_spec = lambda k: pl.BlockSpec((None, 1, d), lambda i: (i // tiles_per_mod, 0, k))
    row_spec = lambda w: pl.BlockSpec((tm, w), lambda i: (i, 0))
    return pl.pallas_call(
        functools.partial(_in_proj_kernel, n_a_tiles=n_a_tiles),
        grid=(rows // tm,),
        in_specs=_two_source_specs(tm, d, n_a_tiles, b_tile0) + [
            pl.BlockSpec((1, d), lambda i: (0, 0)),
            mod_spec(0), mod_spec(1),
            row_spec(128), row_spec(128),
            pl.BlockSpec((None, d, n_w), lambda i: (layer, 0, 0), pipeline_mode=pl.Buffered(1)),
            pl.BlockSpec((384, 512), lambda i: (0, 0)),
            pl.BlockSpec((1, 512), lambda i: (0, 0)),
            pl.BlockSpec((1, 512), lambda i: (0, 0)),
            pl.BlockSpec((1, 512), lambda i: (0, 0)),
            pl.BlockSpec((4, 128, 128), lambda i: (0, 0, 0)),
            pl.BlockSpec((4, 128, 128), lambda i: (0, 0, 0)),
        ],
        out_specs=[row_spec(d), row_spec(768), row_spec(512), row_spec(512), row_spec(1536),
                   row_spec(512)],
        out_shape=[
            jax.ShapeDtypeStruct((rows, d), bf16),
            jax.ShapeDtypeStruct((rows, 768), bf16),
            jax.ShapeDtypeStruct((rows, 512), f32),
            jax.ShapeDtypeStruct((rows, 512), bf16),
            jax.ShapeDtypeStruct((rows, 1536), f32),
            jax.ShapeDtypeStruct((rows, 512), f32),
        ],
        compiler_params=_cparams("arbitrary"),
        name="in_proj",
    )(xa, xb, g, mods, mods, cos_t, sin_t, w_pad, w2p, gbias, ln_g, ln_b, ws, bs_b)


def _sink_column(sink_ref, kvh, rows_per_head):
    r = lax.broadcasted_iota(jnp.int32, (Q_PER_KV * rows_per_head, 1), 0) // rows_per_head
    col = jnp.full(r.shape, sink_ref[kvh * Q_PER_KV], f32)
    for g in range(1, Q_PER_KV):
        col = jnp.where(r == g, sink_ref[kvh * Q_PER_KV + g], col)
    return col


def _win_attn_kernel(sink_ref, main_ref, prev_ref, next_ref, ctx_ref, o_ref, *, tiles_per_seq, seq_len):
    i = pl.program_id(0)
    tq = main_ref.shape[0]
    n_sub = tq // ATT_BLOCK
    blk0 = (i % tiles_per_seq) * n_sub
    blocks_per_seq = seq_len // ATT_BLOCK
    r = lax.broadcasted_iota(jnp.int32, (Q_PER_KV * ATT_BLOCK, ATT_BLOCK), 0) % ATT_BLOCK
    j = lax.broadcasted_iota(jnp.int32, (Q_PER_KV * ATT_BLOCK, ATT_BLOCK), 1)
    keep_prev = j >= r
    keep_next = j <= r
    n_ctx = ctx_ref.shape[0]

    def with_ones(v):
        return jnp.concatenate([v, jnp.ones_like(v)], axis=1)

    for kvh in range(N_KV_HEADS):
        kc = 512 + kvh * HEAD_DIM
        vc = 640 + kvh * HEAD_DIM
        k_all = jnp.concatenate([prev_ref[:, kc:kc + HEAD_DIM], main_ref[:, kc:kc + HEAD_DIM],
                                 next_ref[:, kc:kc + HEAD_DIM]], axis=0)
        v_all = with_ones(jnp.concatenate([prev_ref[:, vc:vc + HEAD_DIM], main_ref[:, vc:vc + HEAD_DIM],
                                           next_ref[:, vc:vc + HEAD_DIM]], axis=0))
        k_ctx = ctx_ref[:, kc:kc + HEAD_DIM]
        v_ctx = with_ones(ctx_ref[:, vc:vc + HEAD_DIM])
        sink = _sink_column(sink_ref, kvh, ATT_BLOCK)
        for sb in range(n_sub):
            rows = slice(sb * ATT_BLOCK, (sb + 1) * ATT_BLOCK)
            q = jnp.concatenate(
                [main_ref[rows, (kvh * Q_PER_KV + g) * HEAD_DIM:(kvh * Q_PER_KV + g + 1) * HEAD_DIM]
                 for g in range(Q_PER_KV)], axis=0)
            band = slice(sb * ATT_BLOCK, (sb + 3) * ATT_BLOCK)
            s_band = _dot(q, k_all[band], _NT)
            s_ctx = _dot(q, k_ctx, _NT)
            blk = blk0 + sb
            parts = [jnp.where(keep_prev & (blk >= 1), s_band[:, 0:ATT_BLOCK], -1e30),
                     s_band[:, ATT_BLOCK:2 * ATT_BLOCK],
                     jnp.where(keep_next & (blk <= blocks_per_seq - 2), s_band[:, 2 * ATT_BLOCK:], -1e30)]
            parts += [s_ctx[:, c:c + 128] for c in range(0, n_ctx, 128)]
            m = jnp.maximum(sink, jnp.max(functools.reduce(jnp.maximum, parts), axis=-1, keepdims=True))
            p = jnp.concatenate([jnp.exp(x - m).astype(bf16) for x in parts], axis=1)
            o_sum = (_dot(p[:, 0:3 * ATT_BLOCK], v_all[band]) + _dot(p[:, 3 * ATT_BLOCK:], v_ctx))
            denom = jnp.exp(sink - m) + o_sum[:, HEAD_DIM:HEAD_DIM + 1]
            o = o_sum[:, 0:HEAD_DIM] / denom
            for g in range(Q_PER_KV):
                c0 = (kvh * Q_PER_KV + g) * HEAD_DIM
                o_ref[rows, c0:c0 + HEAD_DIM] = o[g * ATT_BLOCK:(g + 1) * ATT_BLOCK].astype(o_ref.dtype)


def _window_attention(qkv, sink, n_batch, seq_len, n_ctx):
    tq = ROW_TILE
    n_lat = n_batch * seq_len
    tiles_per_seq = seq_len // tq
    sub = tq // ATT_BLOCK
    n_blocks = n_lat // ATT_BLOCK
    ctx_blk0 = n_lat // n_ctx
    w = qkv.shape[1]
    kern = functools.partial(_win_attn_kernel, tiles_per_seq=tiles_per_seq, seq_len=seq_len)
    return pl.pallas_call(
        kern,
        grid=(n_lat // tq,),
        in_specs=[
            pl.BlockSpec(memory_space=pltpu.SMEM),
            pl.BlockSpec((tq, w), lambda i: (i, 0)),
            pl.BlockSpec((ATT_BLOCK, w), lambda i: (jnp.maximum(i * sub - 1, 0), 0)),
            pl.BlockSpec((ATT_BLOCK, w), lambda i: (jnp.minimum(i * sub + sub, n_blocks - 1), 0)),
            pl.BlockSpec((n_ctx, w), lambda i: (ctx_blk0 + i // tiles_per_seq, 0)),
        ],
        out_specs=pl.BlockSpec((tq, 512), lambda i: (i, 0)),
        out_shape=jax.ShapeDtypeStruct((n_lat, 512), bf16),
        compiler_params=_cparams("arbitrary"),
        name="window_attention",
    )(sink, qkv, qkv, qkv, qkv)


def _ctx_attn_kernel(sink_ref, qkv_ref, o_ref):
    n = qkv_ref.shape[0]
    for kvh in range(N_KV_HEADS):
        kc = 512 + kvh * HEAD_DIM
        vc = 640 + kvh * HEAD_DIM
        q = jnp.concatenate(
            [qkv_ref[:, (kvh * Q_PER_KV + g) * HEAD_DIM:(kvh * Q_PER_KV + g + 1) * HEAD_DIM]
             for g in range(Q_PER_KV)], axis=0)
        s = _dot(q, qkv_ref[:, kc:kc + HEAD_DIM], _NT)
        sink = _sink_column(sink_ref, kvh, n)
        m = jnp.maximum(sink, jnp.max(s, axis=-1, keepdims=True))
        p = jnp.exp(s - m)
        denom = jnp.exp(sink - m) + jnp.sum(p, axis=-1, keepdims=True)
        o = _dot(p.astype(bf16), qkv_ref[:, vc:vc + HEAD_DIM]) / denom
        for g in range(Q_PER_KV):
            c0 = (kvh * Q_PER_KV + g) * HEAD_DIM
            o_ref[:, c0:c0 + HEAD_DIM] = o[g * n:(g + 1) * n].astype(o_ref.dtype)


def _context_attention(qkv, sink, n_batch, n_lat, n_ctx):
    w = qkv.shape[1]
    blk0 = n_lat // n_ctx
    return pl.pallas_call(
        _ctx_attn_kernel,
        grid=(n_batch,),
        in_specs=[pl.BlockSpec(memory_space=pltpu.SMEM),
                  pl.BlockSpec((n_ctx, w), lambda b: (blk0 + b, 0))],
        out_specs=pl.BlockSpec((n_ctx, 512), lambda b: (b, 0)),
        out_shape=jax.ShapeDtypeStruct((n_batch * n_ctx, 512), bf16),
        compiler_params=_cparams("arbitrary"),
        name="context_attention",
    )(sink, qkv)


def _pool_kernel(main_ref, prev_ref, next_ref, pw_ref, ps_ref, o_ref, xe_ref, *, tiles_per_seq, seq_len):
    i = pl.program_id(0)
    tp = main_ref.shape[0]
    t_in_seq = i % tiles_per_seq
    h = POOL_HALO
    xe_ref[0:h, :] = jnp.where(t_in_seq == 0, 0.0, prev_ref[...])
    xe_ref[h:h + tp, :] = main_ref[...]
    xe_ref[h + tp:2 * h + tp, :] = jnp.where(t_in_seq == tiles_per_seq - 1, 0.0, next_ref[...])
    pos = t_in_seq * tp + lax.broadcasted_iota(jnp.int32, (tp, 1), 0)
    for gi, w in enumerate(POOL_WINDOWS):
        cols = slice(gi * 128, (gi + 1) * 128)
        acc = xe_ref[h - w // 2:h - w // 2 + tp, cols]
        for u in range(-w // 2 + 1, w // 2):
            acc = acc + xe_ref[h + u:h + u + tp, cols]
        lo = jnp.maximum(pos - w // 2, 0)
        hi = jnp.minimum(pos + w // 2, seq_len)
        cnt = (hi - lo).astype(f32)
        pooled = acc / cnt - main_ref[:, cols]
        y = _dot(pooled.astype(bf16), pw_ref[gi])
        o_ref[:, cols] = (y * ps_ref[:, cols]).astype(o_ref.dtype)


def _multi_scale_pool(z, row0, n_rows, seq_len, tp, p_w, p_scale):
    tiles_per_seq = seq_len // tp
    t0 = row0 // tp
    h0 = row0 // POOL_HALO
    hb = tp // POOL_HALO
    n_halo = n_rows // POOL_HALO
    kern = functools.partial(_pool_kernel, tiles_per_seq=tiles_per_seq, seq_len=seq_len)
    return pl.pallas_call(
        kern,
        grid=(n_rows // tp,),
        in_specs=[
            pl.BlockSpec((tp, 512), lambda i: (t0 + i, 0)),
            pl.BlockSpec((POOL_HALO, 512), lambda i: (h0 + jnp.maximum(i * hb - 1, 0), 0)),
            pl.BlockSpec((POOL_HALO, 512), lambda i: (h0 + jnp.minimum((i + 1) * hb, n_halo - 1), 0)),
            pl.BlockSpec((4, 128, 128), lambda i: (0, 0, 0)),
            pl.BlockSpec((1, 512), lambda i: (0, 0)),
        ],
        out_specs=pl.BlockSpec((tp, 512), lambda i: (i, 0)),
        out_shape=jax.ShapeDtypeStruct((n_rows, 512), bf16),
        scratch_shapes=[pltpu.VMEM((tp + 2 * POOL_HALO, 512), f32)],
        compiler_params=_cparams("arbitrary"),
        name="multi_scale_pool",
    )(z, z, z, p_w, p_scale)


GLA_SEGMENTS = (1, 2, 4, 8, 16, 32, 64)


def _gla_tables():
    c = GLA_CHUNK
    i = np.arange(c)[:, None]
    j = np.arange(c)[None, :]
    tri = (j <= i).astype(np.float32)
    mask = np.zeros((2, 7, c, c), np.float32)
    mask[:, 0] = np.eye(c)
    for lv, s in enumerate(GLA_SEGMENTS[:-1]):
        m = ((i // (2 * s)) == (j // (2 * s))) & ((i // s) % 2 == 1) & ((j // s) % 2 == 0)
        mask[0, 1 + lv] = m
        mask[1, 1 + lv] = m.T
    mask = np.tile(mask, (1, 1, 1, GLA_HEADS))
    return tri, mask


def _segment_sums(g, cum):
    c = GLA_CHUNK
    row = lax.broadcasted_iota(jnp.int32, (c, 1), 0)
    zero = jnp.zeros_like(g)
    before = [None] + [pltpu.roll(g, kk, 0) for kk in (1, 2, 3)]
    after = [None] + [pltpu.roll(g, c - kk, 0) for kk in (1, 2, 3)]
    a, r = {1: g}, {1: zero}
    for s in (2, 4):
        pos = row % s
        a_s, r_s = g, zero
        for kk in range(1, s):
            a_s = a_s + jnp.where(pos >= kk, before[kk], 0.0)
            r_s = r_s + jnp.where(pos < s - kk, after[kk], 0.0)
        a[s], r[s] = a_s, r_s
    blocks = [cum[8 * b:8 * b + 8] for b in range(c // 8)]
    last = [cum[8 * b + 7:8 * b + 8] for b in range(c // 8)]
    for s in (8, 16, 32, 64):
        nb = s // 8
        a_blk, r_blk = [], []
        for b in range(c // 8):
            prev_end = (b // nb) * nb - 1
            a_blk.append(blocks[b] - last[prev_end] if prev_end >= 0 else blocks[b])
            r_blk.append(last[(b // nb + 1) * nb - 1] - blocks[b])
        a[s] = jnp.concatenate(a_blk, axis=0)
        r[s] = jnp.concatenate(r_blk, axis=0)
    return a, r


def _gla_chunk(d, q, k, v, g, tri_ref, mask_ref, st_ref):
    c = GLA_CHUNK
    w = g.shape[1]
    g_hi = g.astype(bf16)
    r1 = g - g_hi.astype(f32)
    g_mid = r1.astype(bf16)
    g_lo = (r1 - g_mid.astype(f32)).astype(bf16)
    cum3 = _dot(tri_ref[...], jnp.concatenate([g_hi, g_mid, g_lo], axis=1))
    cum = cum3[:, 0:w] + cum3[:, w:2 * w] + cum3[:, 2 * w:3 * w]
    a, r = _segment_sums(g, cum)
    if d == 0:
        cq = [a[s] for s in GLA_SEGMENTS]
        ck = [r[s] for s in GLA_SEGMENTS]
    else:
        cq = [r[s] + g for s in GLA_SEGMENTS]
        ck = [a[s] - g for s in GLA_SEGMENTS]
    total = cum[c - 1:c]

    q = q * (GLA_DK ** -0.5)
    head_of_lane = lax.broadcasted_iota(jnp.int32, (1, w), 1) // GLA_DK
    vhead_of_lane = lax.broadcasted_iota(jnp.int32, (1, v.shape[1]), 1) // GLA_DV

    def stack_heads(x, lane_head):
        return jnp.concatenate([jnp.where(lane_head == h, x, jnp.zeros_like(x))
                                for h in range(GLA_HEADS)], axis=0)

    kb = k.astype(bf16)
    k_plain = stack_heads(kb, head_of_lane)
    q01 = jnp.concatenate([q.astype(bf16), (q * jnp.exp(cq[0])).astype(bf16)], axis=0)
    s01 = _dot(q01, k_plain, _NT)
    att = mask_ref[d, 0] * s01[0:c] + mask_ref[d, 1] * s01[c:2 * c]
    for lv in range(1, 6):
        qs = (q * jnp.exp(cq[lv])).astype(bf16)
        ks = stack_heads((k * jnp.exp(ck[lv])).astype(bf16), head_of_lane)
        att = att + mask_ref[d, 1 + lv] * _dot(qs, ks, _NT)
    v_heads = stack_heads(v.astype(bf16), vhead_of_lane)
    st = st_ref[d]
    q_state = (q * jnp.exp(cq[6])).astype(bf16)
    o = _dot(att.astype(bf16), v_heads) + _dot(q_state, st.astype(bf16), _NT)
    k_state = stack_heads((k * jnp.exp(ck[6])).astype(bf16), head_of_lane)
    st_ref[d] = st * jnp.exp(total) + _dot(v_heads, k_state, _TN)
    return o


def _gla_kernel(qf_ref, kf_ref, vf_ref, gf_ref, qb_ref, kb_ref, vb_ref, gb_ref,
                tri_ref, mask_ref, of_ref, ob_ref, st_ref):
    @pl.when(pl.program_id(1) == 0)
    def _():
        st_ref[...] = jnp.zeros_like(st_ref)

    n_chunks = qf_ref.shape[0] // GLA_CHUNK
    for ci in range(n_chunks):
        rf = slice(ci * GLA_CHUNK, (ci + 1) * GLA_CHUNK)
        of_ref[rf, :] = _gla_chunk(0, qf_ref[rf, :], kf_ref[rf, :], vf_ref[rf, :], gf_ref[rf, :],
                                   tri_ref, mask_ref, st_ref)
        cb = n_chunks - 1 - ci
        rb = slice(cb * GLA_CHUNK, (cb + 1) * GLA_CHUNK)
        ob_ref[rb, :] = _gla_chunk(1, qb_ref[rb, :], kb_ref[rb, :], vb_ref[rb, :], gb_ref[rb, :],
                                   tri_ref, mask_ref, st_ref)


def _gla_scan(z_gla, gates, n_batch, seq_len, n_ctx):
    rows = z_gla.shape[0]
    tb = GLA_BLOCK
    assert n_ctx == tb
    lat_blocks = seq_len // tb
    ctx_blk0 = n_batch * lat_blocks
    tri, mask = _gla_tables()

    def fwd_row(b, s):
        return jnp.where(s == 0, ctx_blk0 + b, b * lat_blocks + s - 1)

    def bwd_row(b, s):
        return jnp.where(s == 0, ctx_blk0 + b, b * lat_blocks + lat_blocks - s)

    def specs(row, direction):
        return [
            pl.BlockSpec((tb, 256), lambda b, s: (row(b, s), 0)),
            pl.BlockSpec((tb, 256), lambda b, s: (row(b, s), 1)),
            pl.BlockSpec((tb, 512), lambda b, s: (row(b, s), 1)),
            pl.BlockSpec((tb, 256), lambda b, s: (row(b, s), direction)),
        ]

    const = lambda shape: pl.BlockSpec(shape, lambda b, s: (0,) * len(shape))
    return pl.pallas_call(
        _gla_kernel,
        grid=(n_batch, 1 + lat_blocks),
        in_specs=specs(fwd_row, 0) + specs(bwd_row, 1) + [
            const(tri.shape), const(mask.shape)],
        out_specs=[pl.BlockSpec((tb, 512), lambda b, s: (fwd_row(b, s), 0)),
                   pl.BlockSpec((tb, 512), lambda b, s: (bwd_row(b, s), 0))],
        out_shape=[jax.ShapeDtypeStruct((rows, 512), f32)] * 2,
        scratch_shapes=[pltpu.VMEM((2, GLA_HEADS * GLA_DV, GLA_HEADS * GLA_DK), f32)],
        compiler_params=_cparams("arbitrary", "arbitrary"),
        name="gla_scan",
    )(z_gla, z_gla, z_gla, gates, z_gla, z_gla, z_gla, gates,
      jnp.asarray(tri, bf16), jnp.asarray(mask, f32))


def _merge_kernel(h_ref, a_ref, p_ref, s_ref, of_ref, ob_ref, r_ref, ng_ref, wg_ref, wb_ref, o_ref, g_ref):
    @pl.when(pl.program_id(1) == 0)
    def _():
        o = of_ref[...] + ob_ref[...]
        r = r_ref[...]
        gate = r * jax.nn.sigmoid(r)
        for h in range(GLA_HEADS):
            cols = slice(h * GLA_DV, (h + 1) * GLA_DV)
            oh = o[:, cols]
            oh = oh * lax.rsqrt(jnp.mean(oh * oh, axis=-1, keepdims=True) + EPS)
            g_ref[:, cols] = (oh * ng_ref[:, cols] * gate[:, cols]).astype(g_ref.dtype)

    hb = h_ref[...]
    acc = None
    for br, b_ref in enumerate((a_ref, p_ref, s_ref, g_ref)):
        gate = jax.nn.sigmoid(_dot(hb, wg_ref[br]))
        term = gate * _dot(b_ref[...], wb_ref[br])
        acc = term if acc is None else acc + term
    o_ref[...] = acc.astype(o_ref.dtype)


def _gated_merge(h, branches, o_f, o_b, z_gla, norm_g, n_rows, layer, wg, wb):
    d = h.shape[1]
    tm, tn = ROW_TILE, 512
    bspec = pl.BlockSpec((tm, 512), lambda i, j: (i, 0))
    return pl.pallas_call(
        _merge_kernel,
        grid=(n_rows // tm, d // tn),
        in_specs=[pl.BlockSpec((tm, d), lambda i, j: (i, 0)), bspec, bspec, bspec,
                  bspec, bspec,
                  pl.BlockSpec((tm, 512), lambda i, j: (i, 2)),
                  pl.BlockSpec((1, 512), lambda i, j: (0, 0)),
                  pl.BlockSpec((None, 4, d, tn), lambda i, j: (layer, 0, 0, j)),
                  pl.BlockSpec((None, 4, 512, tn), lambda i, j: (layer, 0, 0, j))],
        out_specs=pl.BlockSpec((tm, tn), lambda i, j: (i, j)),
        out_shape=jax.ShapeDtypeStruct((n_rows, d), bf16),
        scratch_shapes=[pltpu.VMEM((tm, 512), bf16)],
        compiler_params=_cparams("arbitrary", "arbitrary"),
        name="gated_merge",
    )(h, *branches, o_f, o_b, z_gla, norm_g, wg, wb)


def _out_proj_kernel(acc_ref, wo_ref, xa_ref, xb_ref, gate_ref, o_ref, *, n_a_tiles):
    x = _two_source_rows(xa_ref, xb_ref, n_a_tiles)
    o_ref[...] = x + gate_ref[...] * _dot(acc_ref[...], wo_ref[...])


def _out_proj_residual(acc, layer, wo, xa, xb, n_a_tiles, b_tile0, mods, n_rows, tiles_per_mod):
    d = acc.shape[1]
    tm = ROW_TILE
    return pl.pallas_call(
        functools.partial(_out_proj_kernel, n_a_tiles=n_a_tiles),
        grid=(n_rows // tm,),
        in_specs=[pl.BlockSpec((tm, d), lambda i: (i, 0)),
                  pl.BlockSpec((None, d, d), lambda i: (layer, 0, 0))]
        + _two_source_specs(tm, d, n_a_tiles, b_tile0)
        + [pl.BlockSpec((None, 1, d), lambda i: (i // tiles_per_mod, 0, 2))],
        out_specs=pl.BlockSpec((tm, d), lambda i: (i, 0)),
        out_shape=jax.ShapeDtypeStruct((n_rows, d), f32),
        compiler_params=_cparams("arbitrary"),
        name="out_proj_residual",
    )(acc, wo, xa, xb, mods)


def _router_kernel(x_ref, g_ref, sh_ref, sc_ref, rw_ref, rb_ref, tri_ref, info_ref, cnt_ref, carry_ref):
    i = pl.program_id(0)

    @pl.when(i == 0)
    def _():
        carry_ref[...] = jnp.zeros_like(carry_ref)

    h2 = _norm_mod(x_ref[...], g_ref[...], sh_ref[...], sc_ref[...])
    tm = h2.shape[0]
    h_hi = h2.astype(bf16)
    h_lo = (h2 - h_hi.astype(f32)).astype(bf16)
    part = _dot(rw_ref[...], h_hi, _NT)
    logits = (part[0:N_EXPERTS] + part[N_EXPERTS:2 * N_EXPERTS]
              + _dot(rw_ref[0:N_EXPERTS, :], h_lo, _NT))
    score = jax.nn.sigmoid(logits)
    biased = score + rb_ref[...]
    b = [biased[e:e + 1] for e in range(N_EXPERTS)]
    sc = [score[e:e + 1] for e in range(N_EXPERTS)]

    def top2_sum(v):
        hi01, lo01 = jnp.maximum(v[0], v[1]), jnp.minimum(v[0], v[1])
        hi23, lo23 = jnp.maximum(v[2], v[3]), jnp.minimum(v[2], v[3])
        return jnp.maximum(hi01, hi23) + jnp.maximum(jnp.minimum(hi01, hi23), jnp.maximum(lo01, lo23))

    best = jnp.zeros((1, tm), jnp.int32)
    best_score = top2_sum(b[0:4])
    for gidx in range(1, N_GROUPS):
        gs = top2_sum(b[4 * gidx:4 * gidx + 4])
        take = gs > best_score
        best = jnp.where(take, gidx, best)
        best_score = jnp.where(take, gs, best_score)
    cb, cs = [], []
    for kk in range(EXPERTS_PER_GROUP):
        vb, vs = b[kk], sc[kk]
        for gidx in range(1, N_GROUPS):
            vb = jnp.where(best == gidx, b[4 * gidx + kk], vb)
            vs = jnp.where(best == gidx, sc[4 * gidx + kk], vs)
        cb.append(vb)
        cs.append(vs)
    i1 = jnp.zeros((1, tm), jnp.int32)
    m1, w1 = cb[0], cs[0]
    for kk in range(1, EXPERTS_PER_GROUP):
        take = cb[kk] > m1
        i1 = jnp.where(take, kk, i1)
        m1 = jnp.where(take, cb[kk], m1)
        w1 = jnp.where(take, cs[kk], w1)
    i2 = jnp.where(i1 == 0, 1, 0).astype(jnp.int32)
    m2 = jnp.where(i1 == 0, cb[1], cb[0])
    w2 = jnp.where(i1 == 0, cs[1], cs[0])
    for kk in range(1, EXPERTS_PER_GROUP):
        take = (cb[kk] > m2) & (i1 != kk) & (i2 != kk)
        i2 = jnp.where(take, kk, i2)
        m2 = jnp.where(take, cb[kk], m2)
        w2 = jnp.where(take, cs[kk], w2)
    e1 = best * EXPERTS_PER_GROUP + i1
    e2 = best * EXPERTS_PER_GROUP + i2
    wsum = w1 + w2
    erow = lax.broadcasted_iota(jnp.int32, (N_EXPERTS, tm), 0)
    oh1 = (erow == e1).astype(f32)
    oh2 = (erow == e2).astype(f32)
    oh = oh1 + oh2
    rank = _dot(oh.astype(bf16), tri_ref[...]) + carry_ref[:, 0:1]
    carry_ref[...] = carry_ref[...] + jnp.sum(oh, axis=1, keepdims=True)
    r1 = jnp.sum(oh1 * rank, axis=0, keepdims=True)
    r2 = jnp.sum(oh2 * rank, axis=0, keepdims=True)
    info_ref[...] = jnp.concatenate(
        [e1.astype(f32), e2.astype(f32), w1 / wsum, w2 / wsum, r1, r2,
         jnp.zeros((2, tm), f32)], axis=0)
    cnt_ref[...] = carry_ref[...]


def _router(x_all, n_rows, g, mods, rw_t, rb_col, tiles_per_mod):
    d = x_all.shape[1]
    tm = ROW_TILE
    tri = jnp.asarray(np.triu(np.ones((tm, tm), np.float32), k=1), bf16)
    mod_spec = lambda k: pl.BlockSpec((None, 1, d), lambda i: (i // tiles_per_mod, 0, k))
    return pl.pallas_call(
        _router_kernel,
        grid=(n_rows // tm,),
        in_specs=[pl.BlockSpec((tm, d), lambda i: (i, 0)),
                  pl.BlockSpec((1, d), lambda i: (0, 0)),
                  mod_spec(3), mod_spec(4),
                  pl.BlockSpec((2 * N_EXPERTS, d), lambda i: (0, 0)),
                  pl.BlockSpec((N_EXPERTS, 1), lambda i: (0, 0)),
                  pl.BlockSpec((tm, tm), lambda i: (0, 0))],
        out_specs=[pl.BlockSpec((8, tm), lambda i: (0, i)),
                   pl.BlockSpec((N_EXPERTS, 128), lambda i: (0, 0))],
        out_shape=[jax.ShapeDtypeStruct((8, n_rows), f32),
                   jax.ShapeDtypeStruct((N_EXPERTS, 128), f32)],
        scratch_shapes=[pltpu.VMEM((N_EXPERTS, 128), f32)],
        compiler_params=_cparams("arbitrary"),
        name="moe_router",
    )(x_all, g, mods, mods, rw_t, rb_col, tri)


MOE_DMA_CHUNK = 64


def _dispatch_kernel(pos_ref, zpos_ref, x_ref, g_ref, sh_ref, sc_ref, xs_ref, buf_ref, zero_ref, sem, zsem):
    i = pl.program_id(0)
    tm = x_ref.shape[0]

    @pl.when(i == 0)
    def _():
        zero_ref[...] = jnp.zeros_like(zero_ref)

        def zero_tile(row):
            cp = pltpu.make_async_copy(zero_ref, xs_ref.at[pl.ds(pl.multiple_of(row, 8), tm)], zsem)
            cp.start()
            cp.wait()

        for e in range(N_EXPERTS):
            zero_tile((zpos_ref[e] // 8) * 8)

        def tail(t, c):
            zero_tile(t * tm)
            return c

        lax.fori_loop(zpos_ref[N_EXPERTS], xs_ref.shape[0] // tm, tail, 0)

    slot = i % 2

    def wait_slot(s):
        for k in range(2):
            pltpu.make_async_copy(buf_ref.at[s], xs_ref.at[pl.ds(0, tm)], sem.at[s]).wait()

    @pl.when(i >= 2)
    def _():
        wait_slot(slot)

    src = buf_ref.at[slot]
    for c0 in range(0, tm, MOE_DMA_CHUNK):
        rows = slice(c0, c0 + MOE_DMA_CHUNK)
        buf_ref[slot, rows, :] = _norm_mod(x_ref[rows, :], g_ref[...], sh_ref[...], sc_ref[...])
        for r in range(c0, c0 + MOE_DMA_CHUNK):
            for k in range(2):
                dst = pos_ref[k, i * tm + r]
                pltpu.make_async_copy(src.at[pl.ds(r, 1)], xs_ref.at[pl.ds(dst, 1)], sem.at[slot]).start()

    @pl.when(i == pl.num_programs(0) - 1)
    def _():
        wait_slot(slot)
        wait_slot(1 - slot)


def _dispatch(pos, zpos, x_all, n_rows, g, mods, tiles_per_mod, p_rows):
    d = x_all.shape[1]
    tm = MOE_TILE
    tpm = tiles_per_mod * (ROW_TILE // tm)
    mod_spec = lambda k: pl.BlockSpec((None, 1, d), lambda i, pos, zpos: (i // tpm, 0, k))
    return pl.pallas_call(
        _dispatch_kernel,
        grid_spec=pltpu.PrefetchScalarGridSpec(
            num_scalar_prefetch=2,
            grid=(n_rows // tm,),
            in_specs=[pl.BlockSpec((tm, d), lambda i, pos, zpos: (i, 0)),
                      pl.BlockSpec((1, d), lambda i, pos, zpos: (0, 0)),
                      mod_spec(3), mod_spec(4)],
            out_specs=pl.BlockSpec(memory_space=pl.ANY),
            scratch_shapes=[pltpu.VMEM((2, tm, d), f32), pltpu.VMEM((tm, d), f32),
                            pltpu.SemaphoreType.DMA((2,)), pltpu.SemaphoreType.DMA(())],
        ),
        out_shape=jax.ShapeDtypeStruct((p_rows, d), f32),
        compiler_params=_cparams("arbitrary"),
        name="moe_dispatch",
    )(pos, zpos, x_all, g, mods, mods)


def _expert_kernel(te_ref, nu_ref, xs_ref, wg_ref, wu_ref, wd_ref, y_ref):
    i = pl.program_id(0)

    @pl.when(i < nu_ref[0])
    def _():
        xb = xs_ref[...].astype(bf16)
        a = _dot(xb, wg_ref[...])
        act = (a * jax.nn.sigmoid(a)) * _dot(xb, wu_ref[...])
        y_ref[...] = _dot(act.astype(bf16), wd_ref[...])

    @pl.when(i >= nu_ref[0])
    def _():
        y_ref[...] = jnp.zeros_like(y_ref)


def _expert_ffn(tile_expert, n_used, xs, layer, wg, wu, wd):
    p_rows, d = xs.shape
    de = wg.shape[3]
    tm = MOE_TILE
    wspec = lambda a, b: pl.BlockSpec((None, None, a, b), lambda i, te, nu: (layer, te[i], 0, 0))
    return pl.pallas_call(
        _expert_kernel,
        grid_spec=pltpu.PrefetchScalarGridSpec(
            num_scalar_prefetch=2,
            grid=(p_rows // tm,),
            in_specs=[pl.BlockSpec((tm, d), lambda i, te, nu: (jnp.where(i < nu[0], i, 0), 0)),
                      wspec(d, de), wspec(d, de), wspec(de, d)],
            out_specs=pl.BlockSpec((tm, d), lambda i, te, nu: (i, 0)),
        ),
        out_shape=jax.ShapeDtypeStruct((p_rows, d), f32),
        compiler_params=_cparams("arbitrary"),
        name="moe_experts",
    )(tile_expert, n_used, xs, wg, wu, wd)


def _combine_kernel(pos_ref, x_ref, w_ref, gate_ref, fg_ref, y_ref, o_ref, ybuf_ref, sem, *, final_norm):
    i = pl.program_id(0)
    tm = x_ref.shape[0]
    slot = i % 2

    def gather(step, s, r0, r1):
        for r in range(r0, r1):
            for k in range(2):
                src = pos_ref[k, step * tm + r]
                pltpu.make_async_copy(y_ref.at[pl.ds(src, 1)], ybuf_ref.at[s, k].at[pl.ds(r, 1)],
                                      sem.at[s]).start()

    @pl.when(i == 0)
    def _():
        gather(0, 0, 0, tm)

    for k in range(2):
        pltpu.make_async_copy(y_ref.at[pl.ds(0, tm)], ybuf_ref.at[slot, k], sem.at[slot]).wait()

    has_next = i + 1 < pl.num_programs(0)
    for c0 in range(0, tm, MOE_DMA_CHUNK):
        @pl.when(has_next)
        def _(c0=c0):
            gather(i + 1, 1 - slot, c0, c0 + MOE_DMA_CHUNK)

        rows = slice(c0, c0 + MOE_DMA_CHUNK)
        w = w_ref[rows, :]
        moe = w[:, 0:1] * ybuf_ref[slot, 0, rows, :] + w[:, 1:2] * ybuf_ref[slot, 1, rows, :]
        x = x_ref[rows, :] + gate_ref[...] * moe
        if final_norm:
            x = x * lax.rsqrt(jnp.mean(x * x, axis=-1, keepdims=True) + EPS) * fg_ref[...]
        o_ref[rows, :] = x


def _combine(pos, x_all, n_rows, w_tok, mods, final_g, y, tiles_per_mod, final_norm):
    d = x_all.shape[1]
    tm = MOE_TILE
    tpm = tiles_per_mod * (ROW_TILE // tm)
    kern = functools.partial(_combine_kernel, final_norm=final_norm)
    return pl.pallas_call(
        kern,
        grid_spec=pltpu.PrefetchScalarGridSpec(
            num_scalar_prefetch=1,
            grid=(n_rows // tm,),
            in_specs=[pl.BlockSpec((tm, d), lambda i, pos: (i, 0)),
                      pl.BlockSpec((tm, 2), lambda i, pos: (i, 0)),
                      pl.BlockSpec((None, 1, d), lambda i, pos: (i // tpm, 0, 5)),
                      pl.BlockSpec((1, d), lambda i, pos: (0, 0)),
                      pl.BlockSpec(memory_space=pl.ANY)],
            out_specs=pl.BlockSpec((tm, d), lambda i, pos: (i, 0)),
            scratch_shapes=[pltpu.VMEM((2, 2, tm, d), f32), pltpu.SemaphoreType.DMA((2,))],
        ),
        out_shape=jax.ShapeDtypeStruct((n_rows, d), f32),
        compiler_params=_cparams("arbitrary"),
        name="moe_combine",
    )(pos, x_all, w_tok, mods, final_g, y)


def _moe(x_all, n_rows, norm_g, mods, tiles_per_mod, rw_t, rb_col, layer, wg, wu, wd, final_g, final_norm):
    info, counts = _router(x_all, n_rows, norm_g, mods, rw_t, rb_col, tiles_per_mod)
    tm = MOE_TILE
    p_rows = 2 * n_rows + (N_EXPERTS + 1) * tm
    cnt = counts[:, 0].astype(jnp.int32)
    padded = ((cnt + tm - 1) // tm) * tm
    ends = jnp.cumsum(padded)
    starts = ends - padded
    e1 = info[0].astype(jnp.int32)
    e2 = info[1].astype(jnp.int32)
    pos = jnp.stack([starts[e1] + info[4].astype(jnp.int32),
                     starts[e2] + info[5].astype(jnp.int32)])
    n_used = (ends[-1] // tm).astype(jnp.int32)
    tile_start = jnp.arange(p_rows // tm, dtype=jnp.int32) * tm
    tile_expert = jnp.sum((ends[None, :] <= tile_start[:, None]).astype(jnp.int32), axis=1)
    tile_expert = jnp.minimum(tile_expert, N_EXPERTS - 1)
    zpos = jnp.concatenate([starts + cnt, n_used.reshape(1)])
    xs = _dispatch(pos, zpos, x_all, n_rows, norm_g, mods, tiles_per_mod, p_rows)
    y = _expert_ffn(tile_expert, n_used.reshape(1), xs, layer, wg, wu, wd)
    w_tok = jnp.transpose(info[2:4])
    return _combine(pos, x_all, n_rows, w_tok, mods, final_g, y, tiles_per_mod, final_norm)


def _rope_tables(seq_len, n_batch, n_ctx_rows):
    rows = seq_len // GRID_W
    row = jnp.repeat(jnp.arange(rows), GRID_W)
    col = jnp.tile(jnp.arange(GRID_W), rows)
    nf = HEAD_DIM // 4
    inv_freq = ROPE_BASE ** (-jnp.arange(nf, dtype=f32) / nf)
    ang_r = row[:, None].astype(f32) * inv_freq
    ang_c = col[:, None].astype(f32) * inv_freq
    cos64 = jnp.concatenate([jnp.cos(ang_r)] * 2 + [jnp.cos(ang_c)] * 2, axis=-1)
    sin64 = jnp.concatenate([-jnp.sin(ang_r), jnp.sin(ang_r), -jnp.sin(ang_c), jnp.sin(ang_c)], axis=-1)
    cos_t = jnp.tile(jnp.concatenate([cos64, cos64], axis=-1), (n_batch, 1))
    sin_t = jnp.tile(jnp.concatenate([sin64, sin64], axis=-1), (n_batch, 1))
    cos_t = jnp.concatenate([cos_t, jnp.ones((n_ctx_rows, 128), f32)], axis=0)
    sin_t = jnp.concatenate([sin_t, jnp.zeros((n_ctx_rows, 128), f32)], axis=0)
    return cos_t, sin_t


def kernel(x, c, ctx, c_ctx, w_mod, b_mod, norm1_g, norm2_g, final_norm_g, w_in, a_sink,
           p_w, p_scale, c_ln_g, c_ln_b, c_ws, c_bs, g_w2, g_b, g_norm_g,
           w_branch, w_gate, w_out, router_w, router_b, e_gate, e_up, e_down):
    n_batch, seq_len, d = x.shape
    n_ctx = ctx.shape[1]
    depth = w_mod.shape[0]
    n_lat = n_batch * seq_len
    n_all = n_lat + n_batch * n_ctx
    tiles_per_mod = seq_len // ROW_TILE

    cond = jnp.zeros((8, d), f32).at[0:n_batch].set(c).at[n_batch].set(c_ctx)
    mods_all = _modulation(cond, w_mod, b_mod)
    cos_t, sin_t = _rope_tables(seq_len, n_batch, n_batch * n_ctx)
    rw_f = jnp.transpose(router_w)
    rw_hi = rw_f.astype(bf16)
    rw_t = jnp.concatenate([rw_hi, (rw_f - rw_hi.astype(f32)).astype(bf16)], axis=0)
    rb_col = router_b.reshape(N_EXPERTS, 1)
    final_g = final_norm_g.reshape(1, d)

    w_in_b = jnp.pad(w_in, ((0, 0), (0, 0), (0, 96))).astype(bf16)
    w_gate_b = w_gate.astype(bf16)
    w_branch_b = w_branch.astype(bf16)
    w_out_b = w_out.astype(bf16)
    e_gate_b = e_gate.astype(bf16)
    e_up_b = e_up.astype(bf16)
    e_down_b = e_down.astype(bf16)

    n_lat_tiles = n_lat // ROW_TILE
    xa, xb, b_tile0 = x.reshape(n_lat, d), ctx.reshape(n_batch * n_ctx, d), 0
    for l in range(depth):
        last = l == depth - 1
        n_out = n_lat if last else n_all
        mods = mods_all[l].reshape(8, 1, 6 * d)
        w2p = jnp.zeros((128, 512), f32)
        w2p = w2p.at[0:GLA_RANK, 0:256].set(g_w2[l, 0]).at[GLA_RANK:2 * GLA_RANK, 256:512].set(g_w2[l, 1])
        w2_hi = w2p.astype(bf16)
        w2_split = jnp.concatenate([w2_hi, w2_hi, (w2p - w2_hi.astype(f32)).astype(bf16)], axis=0)
        bs_b = jnp.broadcast_to(c_bs[l][:, :, None], (4, SG_CHUNK, 128))
        h, qkv, z_pool, s_br, z_gla, gates = _in_proj(
            xa, xb, n_lat_tiles, b_tile0, n_all, l, norm1_g[l].reshape(1, d), mods, cos_t, sin_t,
            w_in_b, w2_split, g_b[l].reshape(1, 512), c_ln_g[l].reshape(1, 512),
            c_ln_b[l].reshape(1, 512), c_ws[l].astype(bf16), bs_b, tiles_per_mod)
        a_br = _window_attention(qkv, a_sink[l], n_batch, seq_len, n_ctx)
        pw = p_w[l].astype(bf16)
        ps = p_scale[l].reshape(1, 512)
        p_br = _multi_scale_pool(z_pool, 0, n_lat, seq_len, ROW_TILE, pw, ps)
        o_f, o_b = _gla_scan(z_gla, gates, n_batch, seq_len, n_ctx)
        if not last:
            a_c = _context_attention(qkv, a_sink[l], n_batch, n_lat, n_ctx)
            p_c = _multi_scale_pool(z_pool, n_lat, n_batch * n_ctx, n_ctx, n_ctx, pw, ps)
            a_br = jnp.concatenate([a_br, a_c], axis=0)
            p_br = jnp.concatenate([p_br, p_c], axis=0)
        acc = _gated_merge(h, (a_br, p_br, s_br), o_f, o_b, z_gla, g_norm_g[l].reshape(1, 512),
                           n_out, l, w_gate_b, w_branch_b)
        x_all = _out_proj_residual(acc, l, w_out_b, xa, xb, n_lat_tiles, b_tile0, mods, n_out,
                                   tiles_per_mod)
        x_all = _moe(x_all, n_out, norm2_g[l].reshape(1, d), mods, tiles_per_mod, rw_t, rb_col,
                     l, e_gate_b, e_up_b, e_down_b, final_g, last)
        xa, xb, b_tile0 = x_all, x_all, n_lat_tiles
    return x_all.reshape(n_batch, seq_len, d)
```

```python
import functools

import numpy as np
import jax
import jax.numpy as jnp
from jax import lax
from jax.experimental import pallas as pl
from jax.experimental.pallas import tpu as pltpu

f32 = jnp.float32
bf16 = jnp.bfloat16

EPS = 1e-6
GRID_W = 64
ROPE_BASE = 10000.0

HEAD_DIM = 64
N_Q_HEADS = 8
N_KV_HEADS = 2
Q_PER_KV = N_Q_HEADS // N_KV_HEADS
ATT_BLOCK = 128
POOL_WINDOWS = (2, 4, 8, 16)
POOL_HALO = 8
SG_CHUNK = 128
GLA_HEADS = 4
GLA_DK = 64
GLA_DV = 128
GLA_RANK = 16
GLA_TAU = 16.0
GLA_CHUNK = 64
N_EXPERTS = 16
EXPERTS_PER_GROUP = 4
N_GROUPS = 4

ROW_TILE = 512
GLA_BLOCK = 256
MOE_TILE = 256
V7X_VMEM_BYTES = 64 * 1024 * 1024
VMEM_LIMIT = V7X_VMEM_BYTES * 7 // 8
EXPERT_VMEM_LIMIT = V7X_VMEM_BYTES * 31 // 32


def _cparams(*sem, vmem_limit=VMEM_LIMIT):
    return pltpu.CompilerParams(dimension_semantics=sem, vmem_limit_bytes=vmem_limit)


def _dot(a, b, dims=(((1,), (0,)), ((), ())), precision=None):
    return lax.dot_general(a, b, dims, precision=precision, preferred_element_type=f32)


_NT = (((1,), (1,)), ((), ()))
_TN = (((0,), (0,)), ((), ()))


def _mod_kernel(c_ref, w_ref, b_ref, o_ref):
    c = c_ref[...]
    a = c * jax.nn.sigmoid(c)
    w = w_ref[...]
    a_hi = a.astype(bf16)
    a_lo = (a - a_hi.astype(f32)).astype(bf16)
    w_hi = w.astype(bf16)
    w_lo = (w - w_hi.astype(f32)).astype(bf16)
    n = a.shape[0]
    both = _dot(jnp.concatenate([a_hi, a_lo], axis=0), w_hi)
    o_ref[...] = both[0:n] + both[n:2 * n] + _dot(a_hi, w_lo) + b_ref[...]


def _modulation(cond, w_mod, b_mod):
    n_layers, d, d6 = w_mod.shape
    tn = 1024
    return pl.pallas_call(
        _mod_kernel,
        grid=(n_layers, d6 // tn),
        in_specs=[
            pl.BlockSpec((8, d), lambda l, j: (0, 0)),
            pl.BlockSpec((None, d, tn), lambda l, j: (l, 0, j)),
            pl.BlockSpec((None, 1, tn), lambda l, j: (l, 0, j)),
        ],
        out_specs=pl.BlockSpec((None, 8, tn), lambda l, j: (l, 0, j)),
        out_shape=jax.ShapeDtypeStruct((n_layers, 8, d6), f32),
        compiler_params=_cparams("arbitrary", "arbitrary"),
        name="modulation",
    )(cond, w_mod, b_mod.reshape(n_layers, 1, d6))


def _norm_mod(x, g, shift, scale):
    y = x * lax.rsqrt(jnp.mean(x * x, axis=-1, keepdims=True) + EPS)
    return (y * g) * (1.0 + scale) + shift


def _two_source_rows(xa_ref, xb_ref, n_a_tiles):
    return jnp.where(pl.program_id(0) < n_a_tiles, xa_ref[...], xb_ref[...])


def _two_source_specs(tm, d, n_a_tiles, b_tile0):
    return [pl.BlockSpec((tm, d), lambda i: (jnp.minimum(i, n_a_tiles - 1), 0)),
            pl.BlockSpec((tm, d), lambda i: (b_tile0 + jnp.maximum(i - n_a_tiles, 0), 0),
                         pipeline_mode=pl.Buffered(1))]


def _spatial_gate_rows(uv, lg_ref, lb_ref, ws_ref, bs_ref, o_ref):
    a = jax.nn.gelu(uv)
    u = a[:, 0:512]
    v = a[:, 512:1024]
    mu = jnp.mean(v, axis=-1, keepdims=True)
    var = jnp.mean(jnp.square(v - mu), axis=-1, keepdims=True)
    vn = ((v - mu) * lax.rsqrt(var + EPS) * lg_ref[...] + lb_ref[...]).astype(bf16)
    for c in range(uv.shape[0] // SG_CHUNK):
        rows = slice(c * SG_CHUNK, (c + 1) * SG_CHUNK)
        for g in range(4):
            cols = slice(g * 128, (g + 1) * 128)
            mixed = _dot(ws_ref[g], vn[rows, cols]) + bs_ref[g]
            o_ref[rows, cols] = (u[rows, cols] * mixed).astype(o_ref.dtype)


def _in_proj_kernel(xa_ref, xb_ref, g_ref, sh_ref, sc_ref, cos_ref, sin_ref, w_ref,
                    w2_ref, gb_ref, lg_ref, lb_ref, ws_ref, bs_ref,
                    h_ref, qkv_ref, pool_ref, sg_ref, gla_ref, gate_ref, *, n_a_tiles):
    x = _two_source_rows(xa_ref, xb_ref, n_a_tiles)
    hb = _norm_mod(x, g_ref[...], sh_ref[...], sc_ref[...]).astype(bf16)
    h_ref[...] = hb

    cos = cos_ref[...]
    sin = sin_ref[...]
    lane = lax.broadcasted_iota(jnp.int32, cos.shape, 1)
    first_half = (lane % 32) < 16

    def rope(z):
        rot = jnp.where(first_half, pltpu.roll(z, 112, 1), pltpu.roll(z, 16, 1))
        return z * cos + rot * sin

    zq = _dot(hb, w_ref[:, 0:512])
    for c in range(4):
        qkv_ref[:, c * 128:(c + 1) * 128] = (
            rope(zq[:, c * 128:(c + 1) * 128]) * (HEAD_DIM ** -0.5)).astype(bf16)
    zkv = _dot(hb, w_ref[:, 512:768])
    qkv_ref[:, 512:640] = rope(zkv[:, 0:128]).astype(bf16)
    qkv_ref[:, 640:768] = zkv[:, 128:256].astype(bf16)
    pool_ref[...] = _dot(hb, w_ref[:, 768:1280])
    _spatial_gate_rows(_dot(hb, w_ref[:, 1280:2304]), lg_ref, lb_ref, ws_ref, bs_ref, sg_ref)
    gla_ref[...] = _dot(hb, w_ref[:, 2304:3840])
    low_rank = _dot(hb, w_ref[:, 3840:3968])
    lr_hi = low_rank.astype(bf16)
    lr_lo = (low_rank - lr_hi.astype(f32)).astype(bf16)
    logit = _dot(jnp.concatenate([lr_hi, lr_lo, lr_hi], axis=1), w2_ref[...]) + gb_ref[...]
    gate_ref[...] = _log_sigmoid(logit) / GLA_TAU


def _log_sigmoid(x):
    return jnp.minimum(x, 0.0) - jnp.log(1.0 + jnp.exp(-jnp.abs(x)))


def _in_proj(xa, xb, n_a_tiles, b_tile0, rows, layer, g, mods, cos_t, sin_t, w_pad, w2p, gbias,
             ln_g, ln_b, ws, bs_b, tiles_per_mod):
    d = xa.shape[1]
    n_w = w_pad.shape[2]
    tm = ROW_TILE
    mod_spec = lambda k: pl.BlockSpec((None, 1, d), lambda i: (i // tiles_per_mod, 0, k))
    row_spec = lambda w: pl.BlockSpec((tm, w), lambda i: (i, 0))
    return pl.pallas_call(
        functools.partial(_in_proj_kernel, n_a_tiles=n_a_tiles),
        grid=(rows // tm,),
        in_specs=_two_source_specs(tm, d, n_a_tiles, b_tile0) + [
            pl.BlockSpec((1, d), lambda i: (0, 0)),
            mod_spec(0), mod_spec(1),
            row_spec(128), row_spec(128),
            pl.BlockSpec((None, d, n_w), lambda i: (layer, 0, 0), pipeline_mode=pl.Buffered(1)),
            pl.BlockSpec((384, 512), lambda i: (0, 0)),
            pl.BlockSpec((1, 512), lambda i: (0, 0)),
            pl.BlockSpec((1, 512), lambda i: (0, 0)),
            pl.BlockSpec((1, 512), lambda i: (0, 0)),
            pl.BlockSpec((4, 128, 128), lambda i: (0, 0, 0)),
            pl.BlockSpec((4, 128, 128), lambda i: (0, 0, 0)),
        ],
        out_specs=[row_spec(d), row_spec(768), row_spec(512), row_spec(512), row_spec(1536),
                   row_spec(512)],
        out_shape=[
            jax.ShapeDtypeStruct((rows, d), bf16),
            jax.ShapeDtypeStruct((rows, 768), bf16),
            jax.ShapeDtypeStruct((rows, 512), f32),
            jax.ShapeDtypeStruct((rows, 512), bf16),
            jax.ShapeDtypeStruct((rows, 1536), f32),
            jax.ShapeDtypeStruct((rows, 512), f32),
        ],
        compiler_params=_cparams("arbitrary"),
        name="in_proj",
    )(xa, xb, g, mods, mods, cos_t, sin_t, w_pad, w2p, gbias, ln_g, ln_b, ws, bs_b)


def _sink_column(sink_ref, kvh, rows_per_head):
    r = lax.broadcasted_iota(jnp.int32, (Q_PER_KV * rows_per_head, 1), 0) // rows_per_head
    col = jnp.full(r.shape, sink_ref[kvh * Q_PER_KV], f32)
    for g in range(1, Q_PER_KV):
        col = jnp.where(r == g, sink_ref[kvh * Q_PER_KV + g], col)
    return col


def _win_attn_kernel(sink_ref, main_ref, prev_ref, next_ref, ctx_ref, o_ref, *, tiles_per_seq, seq_len):
    i = pl.program_id(0)
    tq = main_ref.shape[0]
    n_sub = tq // ATT_BLOCK
    blk0 = (i % tiles_per_seq) * n_sub
    blocks_per_seq = seq_len // ATT_BLOCK
    r = lax.broadcasted_iota(jnp.int32, (Q_PER_KV * ATT_BLOCK, ATT_BLOCK), 0) % ATT_BLOCK
    j = lax.broadcasted_iota(jnp.int32, (Q_PER_KV * ATT_BLOCK, ATT_BLOCK), 1)
    keep_prev = j >= r
    keep_next = j <= r
    n_ctx = ctx_ref.shape[0]

    def with_ones(v):
        return jnp.concatenate([v, jnp.ones_like(v)], axis=1)

    for kvh in range(N_KV_HEADS):
        kc = 512 + kvh * HEAD_DIM
        vc = 640 + kvh * HEAD_DIM
        k_all = jnp.concatenate([prev_ref[:, kc:kc + HEAD_DIM], main_ref[:, kc:kc + HEAD_DIM],
                                 next_ref[:, kc:kc + HEAD_DIM]], axis=0)
        v_all = with_ones(jnp.concatenate([prev_ref[:, vc:vc + HEAD_DIM], main_ref[:, vc:vc + HEAD_DIM],
                                           next_ref[:, vc:vc + HEAD_DIM]], axis=0))
        k_ctx = ctx_ref[:, kc:kc + HEAD_DIM]
        v_ctx = with_ones(ctx_ref[:, vc:vc + HEAD_DIM])
        sink = _sink_column(sink_ref, kvh, ATT_BLOCK)
        for sb in range(n_sub):
            rows = slice(sb * ATT_BLOCK, (sb + 1) * ATT_BLOCK)
            q = jnp.concatenate(
                [main_ref[rows, (kvh * Q_PER_KV + g) * HEAD_DIM:(kvh * Q_PER_KV + g + 1) * HEAD_DIM]
                 for g in range(Q_PER_KV)], axis=0)
            band = slice(sb * ATT_BLOCK, (sb + 3) * ATT_BLOCK)
            s_band = _dot(q, k_all[band], _NT)
            s_ctx = _dot(q, k_ctx, _NT)
            blk = blk0 + sb
            parts = [jnp.where(keep_prev & (blk >= 1), s_band[:, 0:ATT_BLOCK], -1e30),
                     s_band[:, ATT_BLOCK:2 * ATT_BLOCK],
                     jnp.where(keep_next & (blk <= blocks_per_seq - 2), s_band[:, 2 * ATT_BLOCK:], -1e30)]
            parts += [s_ctx[:, c:c + 128] for c in range(0, n_ctx, 128)]
            m = jnp.maximum(sink, jnp.max(functools.reduce(jnp.maximum, parts), axis=-1, keepdims=True))
            p = jnp.concatenate([jnp.exp(x - m).astype(bf16) for x in parts], axis=1)
            o_sum = (_dot(p[:, 0:3 * ATT_BLOCK], v_all[band]) + _dot(p[:, 3 * ATT_BLOCK:], v_ctx))
            denom = jnp.exp(sink - m) + o_sum[:, HEAD_DIM:HEAD_DIM + 1]
            o = o_sum[:, 0:HEAD_DIM] / denom
            for g in range(Q_PER_KV):
                c0 = (kvh * Q_PER_KV + g) * HEAD_DIM
                o_ref[rows, c0:c0 + HEAD_DIM] = o[g * ATT_BLOCK:(g + 1) * ATT_BLOCK].astype(o_ref.dtype)


def _window_attention(qkv, sink, n_batch, seq_len, n_ctx):
    tq = ROW_TILE
    n_lat = n_batch * seq_len
    tiles_per_seq = seq_len // tq
    sub = tq // ATT_BLOCK
    n_blocks = n_lat // ATT_BLOCK
    ctx_blk0 = n_lat // n_ctx
    w = qkv.shape[1]
    kern = functools.partial(_win_attn_kernel, tiles_per_seq=tiles_per_seq, seq_len=seq_len)
    return pl.pallas_call(
        kern,
        grid=(n_lat // tq,),
        in_specs=[
            pl.BlockSpec(memory_space=pltpu.SMEM),
            pl.BlockSpec((tq, w), lambda i: (i, 0)),
            pl.BlockSpec((ATT_BLOCK, w), lambda i: (jnp.maximum(i * sub - 1, 0), 0)),
            pl.BlockSpec((ATT_BLOCK, w), lambda i: (jnp.minimum(i * sub + sub, n_blocks - 1), 0)),
            pl.BlockSpec((n_ctx, w), lambda i: (ctx_blk0 + i // tiles_per_seq, 0)),
        ],
        out_specs=pl.BlockSpec((tq, 512), lambda i: (i, 0)),
        out_shape=jax.ShapeDtypeStruct((n_lat, 512), bf16),
        compiler_params=_cparams("arbitrary"),
        name="window_attention",
    )(sink, qkv, qkv, qkv, qkv)


def _ctx_attn_kernel(sink_ref, qkv_ref, o_ref):
    n = qkv_ref.shape[0]
    for kvh in range(N_KV_HEADS):
        kc = 512 + kvh * HEAD_DIM
        vc = 640 + kvh * HEAD_DIM
        q = jnp.concatenate(
            [qkv_ref[:, (kvh * Q_PER_KV + g) * HEAD_DIM:(kvh * Q_PER_KV + g + 1) * HEAD_DIM]
             for g in range(Q_PER_KV)], axis=0)
        s = _dot(q, qkv_ref[:, kc:kc + HEAD_DIM], _NT)
        sink = _sink_column(sink_ref, kvh, n)
        m = jnp.maximum(sink, jnp.max(s, axis=-1, keepdims=True))
        p = jnp.exp(s - m)
        denom = jnp.exp(sink - m) + jnp.sum(p, axis=-1, keepdims=True)
        o = _dot(p.astype(bf16), qkv_ref[:, vc:vc + HEAD_DIM]) / denom
        for g in range(Q_PER_KV):
            c0 = (kvh * Q_PER_KV + g) * HEAD_DIM
            o_ref[:, c0:c0 + HEAD_DIM] = o[g * n:(g + 1) * n].astype(o_ref.dtype)


def _context_attention(qkv, sink, n_batch, n_lat, n_ctx):
    w = qkv.shape[1]
    blk0 = n_lat // n_ctx
    return pl.pallas_call(
        _ctx_attn_kernel,
        grid=(n_batch,),
        in_specs=[pl.BlockSpec(memory_space=pltpu.SMEM),
                  pl.BlockSpec((n_ctx, w), lambda b: (blk0 + b, 0))],
        out_specs=pl.BlockSpec((n_ctx, 512), lambda b: (b, 0)),
        out_shape=jax.ShapeDtypeStruct((n_batch * n_ctx, 512), bf16),
        compiler_params=_cparams("arbitrary"),
        name="context_attention",
    )(sink, qkv)


def _pool_kernel(main_ref, prev_ref, next_ref, pw_ref, ps_ref, o_ref, xe_ref, *, tiles_per_seq, seq_len):
    i = pl.program_id(0)
    tp = main_ref.shape[0]
    t_in_seq = i % tiles_per_seq
    h = POOL_HALO
    xe_ref[0:h, :] = jnp.where(t_in_seq == 0, 0.0, prev_ref[...])
    xe_ref[h:h + tp, :] = main_ref[...]
    xe_ref[h + tp:2 * h + tp, :] = jnp.where(t_in_seq == tiles_per_seq - 1, 0.0, next_ref[...])
    pos = t_in_seq * tp + lax.broadcasted_iota(jnp.int32, (tp, 1), 0)
    for gi, w in enumerate(POOL_WINDOWS):
        cols = slice(gi * 128, (gi + 1) * 128)
        acc = xe_ref[h - w // 2:h - w // 2 + tp, cols]
        for u in range(-w // 2 + 1, w // 2):
            acc = acc + xe_ref[h + u:h + u + tp, cols]
        lo = jnp.maximum(pos - w // 2, 0)
        hi = jnp.minimum(pos + w // 2, seq_len)
        cnt = (hi - lo).astype(f32)
        pooled = acc / cnt - main_ref[:, cols]
        y = _dot(pooled.astype(bf16), pw_ref[gi])
        o_ref[:, cols] = (y * ps_ref[:, cols]).astype(o_ref.dtype)


def _multi_scale_pool(z, row0, n_rows, seq_len, tp, p_w, p_scale):
    tiles_per_seq = seq_len // tp
    t0 = row0 // tp
    h0 = row0 // POOL_HALO
    hb = tp // POOL_HALO
    n_halo = n_rows // POOL_HALO
    kern = functools.partial(_pool_kernel, tiles_per_seq=tiles_per_seq, seq_len=seq_len)
    return pl.pallas_call(
        kern,
        grid=(n_rows // tp,),
        in_specs=[
            pl.BlockSpec((tp, 512), lambda i: (t0 + i, 0)),
            pl.BlockSpec((POOL_HALO, 512), lambda i: (h0 + jnp.maximum(i * hb - 1, 0), 0)),
            pl.BlockSpec((POOL_HALO, 512), lambda i: (h0 + jnp.minimum((i + 1) * hb, n_halo - 1), 0)),
            pl.BlockSpec((4, 128, 128), lambda i: (0, 0, 0)),
            pl.BlockSpec((1, 512), lambda i: (0, 0)),
        ],
        out_specs=pl.BlockSpec((tp, 512), lambda i: (i, 0)),
        out_shape=jax.ShapeDtypeStruct((n_rows, 512), bf16),
        scratch_shapes=[pltpu.VMEM((tp + 2 * POOL_HALO, 512), f32)],
        compiler_params=_cparams("arbitrary"),
        name="multi_scale_pool",
    )(z, z, z, p_w, p_scale)


GLA_SEGMENTS = (1, 2, 4, 8, 16, 32, 64)


def _gla_tables():
    c = GLA_CHUNK
    i = np.arange(c)[:, None]
    j = np.arange(c)[None, :]
    tri = (j <= i).astype(np.float32)
    mask = np.zeros((2, 7, c, c), np.float32)
    mask[:, 0] = np.eye(c)
    for lv, s in enumerate(GLA_SEGMENTS[:-1]):
        m = ((i // (2 * s)) == (j // (2 * s))) & ((i // s) % 2 == 1) & ((j // s) % 2 == 0)
        mask[0, 1 + lv] = m
        mask[1, 1 + lv] = m.T
    mask = np.tile(mask, (1, 1, 1, GLA_HEADS))
    return tri, mask


def _segment_sums(g, cum):
    c = GLA_CHUNK
    row = lax.broadcasted_iota(jnp.int32, (c, 1), 0)
    zero = jnp.zeros_like(g)
    before = [None] + [pltpu.roll(g, kk, 0) for kk in (1, 2, 3)]
    after = [None] + [pltpu.roll(g, c - kk, 0) for kk in (1, 2, 3)]
    a, r = {1: g}, {1: zero}
    for s in (2, 4):
        pos = row % s
        a_s, r_s = g, zero
        for kk in range(1, s):
            a_s = a_s + jnp.where(pos >= kk, before[kk], 0.0)
            r_s = r_s + jnp.where(pos < s - kk, after[kk], 0.0)
        a[s], r[s] = a_s, r_s
    blocks = [cum[8 * b:8 * b + 8] for b in range(c // 8)]
    last = [cum[8 * b + 7:8 * b + 8] for b in range(c // 8)]
    for s in (8, 16, 32, 64):
        nb = s // 8
        a_blk, r_blk = [], []
        for b in range(c // 8):
            prev_end = (b // nb) * nb - 1
            a_blk.append(blocks[b] - last[prev_end] if prev_end >= 0 else blocks[b])
            r_blk.append(last[(b // nb + 1) * nb - 1] - blocks[b])
        a[s] = jnp.concatenate(a_blk, axis=0)
        r[s] = jnp.concatenate(r_blk, axis=0)
    return a, r


def _gla_chunk(d, q, k, v, g, tri_ref, mask_ref, st_ref):
    c = GLA_CHUNK
    w = g.shape[1]
    g_hi = g.astype(bf16)
    r1 = g - g_hi.astype(f32)
    g_mid = r1.astype(bf16)
    g_lo = (r1 - g_mid.astype(f32)).astype(bf16)
    cum3 = _dot(tri_ref[...], jnp.concatenate([g_hi, g_mid, g_lo], axis=1))
    cum = cum3[:, 0:w] + cum3[:, w:2 * w] + cum3[:, 2 * w:3 * w]
    a, r = _segment_sums(g, cum)
    if d == 0:
        cq = [a[s] for s in GLA_SEGMENTS]
        ck = [r[s] for s in GLA_SEGMENTS]
    else:
        cq = [r[s] + g for s in GLA_SEGMENTS]
        ck = [a[s] - g for s in GLA_SEGMENTS]
    total = cum[c - 1:c]

    q = q * (GLA_DK ** -0.5)
    head_of_lane = lax.broadcasted_iota(jnp.int32, (1, w), 1) // GLA_DK
    vhead_of_lane = lax.broadcasted_iota(jnp.int32, (1, v.shape[1]), 1) // GLA_DV

    def stack_heads(x, lane_head):
        return jnp.concatenate([jnp.where(lane_head == h, x, jnp.zeros_like(x))
                                for h in range(GLA_HEADS)], axis=0)

    kb = k.astype(bf16)
    k_plain = stack_heads(kb, head_of_lane)
    q01 = jnp.concatenate([q.astype(bf16), (q * jnp.exp(cq[0])).astype(bf16)], axis=0)
    s01 = _dot(q01, k_plain, _NT)
    att = mask_ref[d, 0] * s01[0:c] + mask_ref[d, 1] * s01[c:2 * c]
    for lv in range(1, 6):
        qs = (q * jnp.exp(cq[lv])).astype(bf16)
        ks = stack_heads((k * jnp.exp(ck[lv])).astype(bf16), head_of_lane)
        att = att + mask_ref[d, 1 + lv] * _dot(qs, ks, _NT)
    v_heads = stack_heads(v.astype(bf16), vhead_of_lane)
    st = st_ref[d]
    q_state = (q * jnp.exp(cq[6])).astype(bf16)
    o = _dot(att.astype(bf16), v_heads) + _dot(q_state, st.astype(bf16), _NT)
    k_state = stack_heads((k * jnp.exp(ck[6])).astype(bf16), head_of_lane)
    st_ref[d] = st * jnp.exp(total) + _dot(v_heads, k_state, _TN)
    return o


def _gla_kernel(qf_ref, kf_ref, vf_ref, gf_ref, qb_ref, kb_ref, vb_ref, gb_ref,
                tri_ref, mask_ref, of_ref, ob_ref, st_ref):
    @pl.when(pl.program_id(1) == 0)
    def _():
        st_ref[...] = jnp.zeros_like(st_ref)

    n_chunks = qf_ref.shape[0] // GLA_CHUNK
    for ci in range(n_chunks):
        rf = slice(ci * GLA_CHUNK, (ci + 1) * GLA_CHUNK)
        of_ref[rf, :] = _gla_chunk(0, qf_ref[rf, :], kf_ref[rf, :], vf_ref[rf, :], gf_ref[rf, :],
                                   tri_ref, mask_ref, st_ref)
        cb = n_chunks - 1 - ci
        rb = slice(cb * GLA_CHUNK, (cb + 1) * GLA_CHUNK)
        ob_ref[rb, :] = _gla_chunk(1, qb_ref[rb, :], kb_ref[rb, :], vb_ref[rb, :], gb_ref[rb, :],
                                   tri_ref, mask_ref, st_ref)


def _gla_scan(z_gla, gates, n_batch, seq_len, n_ctx):
    rows = z_gla.shape[0]
    tb = GLA_BLOCK
    assert n_ctx == tb
    lat_blocks = seq_len // tb
    ctx_blk0 = n_batch * lat_blocks
    tri, mask = _gla_tables()

    def fwd_row(b, s):
        return jnp.where(s == 0, ctx_blk0 + b, b * lat_blocks + s - 1)

    def bwd_row(b, s):
        return jnp.where(s == 0, ctx_blk0 + b, b * lat_blocks + lat_blocks - s)

    def specs(row, direction):
        return [
            pl.BlockSpec((tb, 256), lambda b, s: (row(b, s), 0)),
            pl.BlockSpec((tb, 256), lambda b, s: (row(b, s), 1)),
            pl.BlockSpec((tb, 512), lambda b, s: (row(b, s), 1)),
            pl.BlockSpec((tb, 256), lambda b, s: (row(b, s), direction)),
        ]

    const = lambda shape: pl.BlockSpec(shape, lambda b, s: (0,) * len(shape))
    return pl.pallas_call(
        _gla_kernel,
        grid=(n_batch, 1 + lat_blocks),
        in_specs=specs(fwd_row, 0) + specs(bwd_row, 1) + [
            const(tri.shape), const(mask.shape)],
        out_specs=[pl.BlockSpec((tb, 512), lambda b, s: (fwd_row(b, s), 0)),
                   pl.BlockSpec((tb, 512), lambda b, s: (bwd_row(b, s), 0))],
        out_shape=[jax.ShapeDtypeStruct((rows, 512), f32)] * 2,
        scratch_shapes=[pltpu.VMEM((2, GLA_HEADS * GLA_DV, GLA_HEADS * GLA_DK), f32)],
        compiler_params=_cparams("arbitrary", "arbitrary"),
        name="gla_scan",
    )(z_gla, z_gla, z_gla, gates, z_gla, z_gla, z_gla, gates,
      jnp.asarray(tri, bf16), jnp.asarray(mask, f32))


def _merge_kernel(h_ref, a_ref, p_ref, s_ref, of_ref, ob_ref, r_ref, ng_ref, wg_ref, wb_ref, o_ref, g_ref):
    @pl.when(pl.program_id(1) == 0)
    def _():
        o = of_ref[...] + ob_ref[...]
        r = r_ref[...]
        gate = r * jax.nn.sigmoid(r)
        for h in range(GLA_HEADS):
            cols = slice(h * GLA_DV, (h + 1) * GLA_DV)
            oh = o[:, cols]
            oh = oh * lax.rsqrt(jnp.mean(oh * oh, axis=-1, keepdims=True) + EPS)
            g_ref[:, cols] = (oh * ng_ref[:, cols] * gate[:, cols]).astype(g_ref.dtype)

    hb = h_ref[...]
    acc = None
    for br, b_ref in enumerate((a_ref, p_ref, s_ref, g_ref)):
        gate = jax.nn.sigmoid(_dot(hb, wg_ref[br]))
        term = gate * _dot(b_ref[...], wb_ref[br])
        acc = term if acc is None else acc + term
    o_ref[...] = acc.astype(o_ref.dtype)


def _gated_merge(h, branches, o_f, o_b, z_gla, norm_g, n_rows, layer, wg, wb):
    d = h.shape[1]
    tm, tn = ROW_TILE, 512
    bspec = pl.BlockSpec((tm, 512), lambda i, j: (i, 0))
    return pl.pallas_call(
        _merge_kernel,
        grid=(n_rows // tm, d // tn),
        in_specs=[pl.BlockSpec((tm, d), lambda i, j: (i, 0)), bspec, bspec, bspec,
                  bspec, bspec,
                  pl.BlockSpec((tm, 512), lambda i, j: (i, 2)),
                  pl.BlockSpec((1, 512), lambda i, j: (0, 0)),
                  pl.BlockSpec((None, 4, d, tn), lambda i, j: (layer, 0, 0, j)),
                  pl.BlockSpec((None, 4, 512, tn), lambda i, j: (layer, 0, 0, j))],
        out_specs=pl.BlockSpec((tm, tn), lambda i, j: (i, j)),
        out_shape=jax.ShapeDtypeStruct((n_rows, d), bf16),
        scratch_shapes=[pltpu.VMEM((tm, 512), bf16)],
        compiler_params=_cparams("arbitrary", "arbitrary"),
        name="gated_merge",
    )(h, *branches, o_f, o_b, z_gla, norm_g, wg, wb)


def _out_proj_kernel(acc_ref, wo_ref, xa_ref, xb_ref, gate_ref, o_ref, *, n_a_tiles):
    x = _two_source_rows(xa_ref, xb_ref, n_a_tiles)
    o_ref[...] = x + gate_ref[...] * _dot(acc_ref[...], wo_ref[...])


def _out_proj_residual(acc, layer, wo, xa, xb, n_a_tiles, b_tile0, mods, n_rows, tiles_per_mod):
    d = acc.shape[1]
    tm = ROW_TILE
    return pl.pallas_call(
        functools.partial(_out_proj_kernel, n_a_tiles=n_a_tiles),
        grid=(n_rows // tm,),
        in_specs=[pl.BlockSpec((tm, d), lambda i: (i, 0)),
                  pl.BlockSpec((None, d, d), lambda i: (layer, 0, 0))]
        + _two_source_specs(tm, d, n_a_tiles, b_tile0)
        + [pl.BlockSpec((None, 1, d), lambda i: (i // tiles_per_mod, 0, 2))],
        out_specs=pl.BlockSpec((tm, d), lambda i: (i, 0)),
        out_shape=jax.ShapeDtypeStruct((n_rows, d), f32),
        compiler_params=_cparams("arbitrary"),
        name="out_proj_residual",
    )(acc, wo, xa, xb, mods)


def _router_kernel(x_ref, g_ref, sh_ref, sc_ref, rw_ref, rb_ref, tri_ref, info_ref, cnt_ref, carry_ref):
    i = pl.program_id(0)

    @pl.when(i == 0)
    def _():
        carry_ref[...] = jnp.zeros_like(carry_ref)

    h2 = _norm_mod(x_ref[...], g_ref[...], sh_ref[...], sc_ref[...])
    tm = h2.shape[0]
    h_hi = h2.astype(bf16)
    h_lo = (h2 - h_hi.astype(f32)).astype(bf16)
    part = _dot(rw_ref[...], h_hi, _NT)
    logits = (part[0:N_EXPERTS] + part[N_EXPERTS:2 * N_EXPERTS]
              + _dot(rw_ref[0:N_EXPERTS, :], h_lo, _NT))
    score = jax.nn.sigmoid(logits)
    biased = score + rb_ref[...]
    b = [biased[e:e + 1] for e in range(N_EXPERTS)]
    sc = [score[e:e + 1] for e in range(N_EXPERTS)]

    def top2_sum(v):
        hi01, lo01 = jnp.maximum(v[0], v[1]), jnp.minimum(v[0], v[1])
        hi23, lo23 = jnp.maximum(v[2], v[3]), jnp.minimum(v[2], v[3])
        return jnp.maximum(hi01, hi23) + jnp.maximum(jnp.minimum(hi01, hi23), jnp.maximum(lo01, lo23))

    best = jnp.zeros((1, tm), jnp.int32)
    best_score = top2_sum(b[0:4])
    for gidx in range(1, N_GROUPS):
        gs = top2_sum(b[4 * gidx:4 * gidx + 4])
        take = gs > best_score
        best = jnp.where(take, gidx, best)
        best_score = jnp.where(take, gs, best_score)
    cb, cs = [], []
    for kk in range(EXPERTS_PER_GROUP):
        vb, vs = b[kk], sc[kk]
        for gidx in range(1, N_GROUPS):
            vb = jnp.where(best == gidx, b[4 * gidx + kk], vb)
            vs = jnp.where(best == gidx, sc[4 * gidx + kk], vs)
        cb.append(vb)
        cs.append(vs)
    i1 = jnp.zeros((1, tm), jnp.int32)
    m1, w1 = cb[0], cs[0]
    for kk in range(1, EXPERTS_PER_GROUP):
        take = cb[kk] > m1
        i1 = jnp.where(take, kk, i1)
        m1 = jnp.where(take, cb[kk], m1)
        w1 = jnp.where(take, cs[kk], w1)
    i2 = jnp.where(i1 == 0, 1, 0).astype(jnp.int32)
    m2 = jnp.where(i1 == 0, cb[1], cb[0])
    w2 = jnp.where(i1 == 0, cs[1], cs[0])
    for kk in range(1, EXPERTS_PER_GROUP):
        take = (cb[kk] > m2) & (i1 != kk) & (i2 != kk)
        i2 = jnp.where(take, kk, i2)
        m2 = jnp.where(take, cb[kk], m2)
        w2 = jnp.where(take, cs[kk], w2)
    e1 = best * EXPERTS_PER_GROUP + i1
    e2 = best * EXPERTS_PER_GROUP + i2
    wsum = w1 + w2
    erow = lax.broadcasted_iota(jnp.int32, (N_EXPERTS, tm), 0)
    oh1 = (erow == e1).astype(f32)
    oh2 = (erow == e2).astype(f32)
    oh = oh1 + oh2
    rank = _dot(oh.astype(bf16), tri_ref[...]) + carry_ref[:, 0:1]
    carry_ref[...] = carry_ref[...] + jnp.sum(oh, axis=1, keepdims=True)
    r1 = jnp.sum(oh1 * rank, axis=0, keepdims=True)
    r2 = jnp.sum(oh2 * rank, axis=0, keepdims=True)
    info_ref[...] = jnp.concatenate(
        [e1.astype(f32), e2.astype(f32), w1 / wsum, w2 / wsum, r1, r2,
         jnp.zeros((2, tm), f32)], axis=0)
    cnt_ref[...] = carry_ref[...]


def _router(x_all, n_rows, g, mods, rw_t, rb_col, tiles_per_mod):
    d = x_all.shape[1]
    tm = ROW_TILE
    tri = jnp.asarray(np.triu(np.ones((tm, tm), np.float32), k=1), bf16)
    mod_spec = lambda k: pl.BlockSpec((None, 1, d), lambda i: (i // tiles_per_mod, 0, k))
    return pl.pallas_call(
        _router_kernel,
        grid=(n_rows // tm,),
        in_specs=[pl.BlockSpec((tm, d), lambda i: (i, 0)),
                  pl.BlockSpec((1, d), lambda i: (0, 0)),
                  mod_spec(3), mod_spec(4),
                  pl.BlockSpec((2 * N_EXPERTS, d), lambda i: (0, 0)),
                  pl.BlockSpec((N_EXPERTS, 1), lambda i: (0, 0)),
                  pl.BlockSpec((tm, tm), lambda i: (0, 0))],
        out_specs=[pl.BlockSpec((8, tm), lambda i: (0, i)),
                   pl.BlockSpec((N_EXPERTS, 128), lambda i: (0, 0))],
        out_shape=[jax.ShapeDtypeStruct((8, n_rows), f32),
                   jax.ShapeDtypeStruct((N_EXPERTS, 128), f32)],
        scratch_shapes=[pltpu.VMEM((N_EXPERTS, 128), f32)],
        compiler_params=_cparams("arbitrary"),
        name="moe_router",
    )(x_all, g, mods, mods, rw_t, rb_col, tri)


def _dispatch_kernel(pos_ref, zpos_ref, x_ref, g_ref, sh_ref, sc_ref, xs_ref, buf_ref, zero_ref, sem, zsem):
    i = pl.program_id(0)
    tm = x_ref.shape[0]

    @pl.when(i == 0)
    def _():
        zero_ref[...] = jnp.zeros_like(zero_ref)

        def zero_tile(row):
            cp = pltpu.make_async_copy(zero_ref, xs_ref.at[pl.ds(pl.multiple_of(row, 8), tm)], zsem)
            cp.start()
            cp.wait()

        for e in range(N_EXPERTS):
            zero_tile((zpos_ref[e] // 8) * 8)

        def tail(t, c):
            zero_tile(t * tm)
            return c

        lax.fori_loop(zpos_ref[N_EXPERTS], xs_ref.shape[0] // tm, tail, 0)

    slot = i % 2

    def wait_slot(s):
        for k in range(2):
            pltpu.make_async_copy(buf_ref.at[s], xs_ref.at[pl.ds(0, tm)], sem.at[s]).wait()

    @pl.when(i >= 2)
    def _():
        wait_slot(slot)

    buf_ref[slot] = _norm_mod(x_ref[...], g_ref[...], sh_ref[...], sc_ref[...])
    src = buf_ref.at[slot]
    for r in range(tm):
        for k in range(2):
            dst = pos_ref[k, i * tm + r]
            pltpu.make_async_copy(src.at[pl.ds(r, 1)], xs_ref.at[pl.ds(dst, 1)], sem.at[slot]).start()

    @pl.when(i == pl.num_programs(0) - 1)
    def _():
        wait_slot(slot)
        wait_slot(1 - slot)


def _dispatch(pos, zpos, x_all, n_rows, g, mods, tiles_per_mod, p_rows):
    d = x_all.shape[1]
    tm = MOE_TILE
    tpm = tiles_per_mod * (ROW_TILE // tm)
    mod_spec = lambda k: pl.BlockSpec((None, 1, d), lambda i, pos, zpos: (i // tpm, 0, k))
    return pl.pallas_call(
        _dispatch_kernel,
        grid_spec=pltpu.PrefetchScalarGridSpec(
            num_scalar_prefetch=2,
            grid=(n_rows // tm,),
            in_specs=[pl.BlockSpec((tm, d), lambda i, pos, zpos: (i, 0)),
                      pl.BlockSpec((1, d), lambda i, pos, zpos: (0, 0)),
                      mod_spec(3), mod_spec(4)],
            out_specs=pl.BlockSpec(memory_space=pl.ANY),
            scratch_shapes=[pltpu.VMEM((2, tm, d), f32), pltpu.VMEM((tm, d), f32),
                            pltpu.SemaphoreType.DMA((2,)), pltpu.SemaphoreType.DMA(())],
        ),
        out_shape=jax.ShapeDtypeStruct((p_rows, d), f32),
        compiler_params=_cparams("arbitrary"),
        name="moe_dispatch",
    )(pos, zpos, x_all, g, mods, mods)


def _expert_kernel(te_ref, nu_ref, xs_ref, wg_ref, wu_ref, wd_ref, y_ref):
    i = pl.program_id(0)

    @pl.when(i < nu_ref[0])
    def _():
        x = xs_ref[...]
        a = _dot(x, wg_ref[...])
        act = (a * jax.nn.sigmoid(a)) * _dot(x, wu_ref[...])
        y_ref[...] = _dot(act, wd_ref[...])

    @pl.when(i >= nu_ref[0])
    def _():
        y_ref[...] = jnp.zeros_like(y_ref)


def _expert_ffn(tile_expert, n_used, xs, layer, wg, wu, wd):
    p_rows, d = xs.shape
    de = wg.shape[3]
    tm = MOE_TILE
    wspec = lambda a, b: pl.BlockSpec((None, None, a, b), lambda i, te, nu: (layer, te[i], 0, 0))
    return pl.pallas_call(
        _expert_kernel,
        grid_spec=pltpu.PrefetchScalarGridSpec(
            num_scalar_prefetch=2,
            grid=(p_rows // tm,),
            in_specs=[pl.BlockSpec((tm, d), lambda i, te, nu: (jnp.where(i < nu[0], i, 0), 0)),
                      wspec(d, de), wspec(d, de), wspec(de, d)],
            out_specs=pl.BlockSpec((tm, d), lambda i, te, nu: (i, 0)),
        ),
        out_shape=jax.ShapeDtypeStruct((p_rows, d), f32),
        compiler_params=_cparams("arbitrary", vmem_limit=EXPERT_VMEM_LIMIT),
        name="moe_experts",
    )(tile_expert, n_used, xs, wg, wu, wd)


def _combine_kernel(pos_ref, x_ref, w_ref, gate_ref, fg_ref, y_ref, o_ref, ybuf_ref, sem, *, final_norm):
    i = pl.program_id(0)
    tm = x_ref.shape[0]
    slot = i % 2

    def gather(step, s):
        for r in range(tm):
            for k in range(2):
                src = pos_ref[k, step * tm + r]
                pltpu.make_async_copy(y_ref.at[pl.ds(src, 1)], ybuf_ref.at[s, k].at[pl.ds(r, 1)],
                                      sem.at[s]).start()

    @pl.when(i == 0)
    def _():
        gather(0, 0)

    @pl.when(i + 1 < pl.num_programs(0))
    def _():
        gather(i + 1, 1 - slot)

    for k in range(2):
        pltpu.make_async_copy(y_ref.at[pl.ds(0, tm)], ybuf_ref.at[slot, k], sem.at[slot]).wait()
    w = w_ref[...]
    moe = w[:, 0:1] * ybuf_ref[slot, 0] + w[:, 1:2] * ybuf_ref[slot, 1]
    x = x_ref[...] + gate_ref[...] * moe
    if final_norm:
        x = x * lax.rsqrt(jnp.mean(x * x, axis=-1, keepdims=True) + EPS) * fg_ref[...]
    o_ref[...] = x


def _combine(pos, x_all, n_rows, w_tok, mods, final_g, y, tiles_per_mod, final_norm):
    d = x_all.shape[1]
    tm = MOE_TILE
    tpm = tiles_per_mod * (ROW_TILE // tm)
    kern = functools.partial(_combine_kernel, final_norm=final_norm)
    return pl.pallas_call(
        kern,
        grid_spec=pltpu.PrefetchScalarGridSpec(
            num_scalar_prefetch=1,
            grid=(n_rows // tm,),
            in_specs=[pl.BlockSpec((tm, d), lambda i, pos: (i, 0)),
                      pl.BlockSpec((tm, 2), lambda i, pos: (i, 0)),
                      pl.BlockSpec((None, 1, d), lambda i, pos: (i // tpm, 0, 5)),
                      pl.BlockSpec((1, d), lambda i, pos: (0, 0)),
                      pl.BlockSpec(memory_space=pl.ANY)],
            out_specs=pl.BlockSpec((tm, d), lambda i, pos: (i, 0)),
            scratch_shapes=[pltpu.VMEM((2, 2, tm, d), f32), pltpu.SemaphoreType.DMA((2,))],
        ),
        out_shape=jax.ShapeDtypeStruct((n_rows, d), f32),
        compiler_params=_cparams("arbitrary"),
        name="moe_combine",
    )(pos, x_all, w_tok, mods, final_g, y)


def _moe(x_all, n_rows, norm_g, mods, tiles_per_mod, rw_t, rb_col, layer, wg, wu, wd, final_g, final_norm):
    info, counts = _router(x_all, n_rows, norm_g, mods, rw_t, rb_col, tiles_per_mod)
    tm = MOE_TILE
    p_rows = 2 * n_rows + (N_EXPERTS + 1) * tm
    cnt = counts[:, 0].astype(jnp.int32)
    padded = ((cnt + tm - 1) // tm) * tm
    ends = jnp.cumsum(padded)
    starts = ends - padded
    e1 = info[0].astype(jnp.int32)
    e2 = info[1].astype(jnp.int32)
    pos = jnp.stack([starts[e1] + info[4].astype(jnp.int32),
                     starts[e2] + info[5].astype(jnp.int32)])
    n_used = (ends[-1] // tm).astype(jnp.int32)
    tile_start = jnp.arange(p_rows // tm, dtype=jnp.int32) * tm
    tile_expert = jnp.sum((ends[None, :] <= tile_start[:, None]).astype(jnp.int32), axis=1)
    tile_expert = jnp.minimum(tile_expert, N_EXPERTS - 1)
    zpos = jnp.concatenate([starts + cnt, n_used.reshape(1)])
    xs = _dispatch(pos, zpos, x_all, n_rows, norm_g, mods, tiles_per_mod, p_rows)
    y = _expert_ffn(tile_expert, n_used.reshape(1), xs, layer, wg, wu, wd)
    w_tok = jnp.transpose(info[2:4])
    return _combine(pos, x_all, n_rows, w_tok, mods, final_g, y, tiles_per_mod, final_norm)


def _rope_tables(seq_len, n_batch, n_ctx_rows):
    rows = seq_len // GRID_W
    row = jnp.repeat(jnp.arange(rows), GRID_W)
    col = jnp.tile(jnp.arange(GRID_W), rows)
    nf = HEAD_DIM // 4
    inv_freq = ROPE_BASE ** (-jnp.arange(nf, dtype=f32) / nf)
    ang_r = row[:, None].astype(f32) * inv_freq
    ang_c = col[:, None].astype(f32) * inv_freq
    cos64 = jnp.concatenate([jnp.cos(ang_r)] * 2 + [jnp.cos(ang_c)] * 2, axis=-1)
    sin64 = jnp.concatenate([-jnp.sin(ang_r), jnp.sin(ang_r), -jnp.sin(ang_c), jnp.sin(ang_c)], axis=-1)
    cos_t = jnp.tile(jnp.concatenate([cos64, cos64], axis=-1), (n_batch, 1))
    sin_t = jnp.tile(jnp.concatenate([sin64, sin64], axis=-1), (n_batch, 1))
    cos_t = jnp.concatenate([cos_t, jnp.ones((n_ctx_rows, 128), f32)], axis=0)
    sin_t = jnp.concatenate([sin_t, jnp.zeros((n_ctx_rows, 128), f32)], axis=0)
    return cos_t, sin_t


def kernel(x, c, ctx, c_ctx, w_mod, b_mod, norm1_g, norm2_g, final_norm_g, w_in, a_sink,
           p_w, p_scale, c_ln_g, c_ln_b, c_ws, c_bs, g_w2, g_b, g_norm_g,
           w_branch, w_gate, w_out, router_w, router_b, e_gate, e_up, e_down):
    n_batch, seq_len, d = x.shape
    n_ctx = ctx.shape[1]
    depth = w_mod.shape[0]
    n_lat = n_batch * seq_len
    n_all = n_lat + n_batch * n_ctx
    tiles_per_mod = seq_len // ROW_TILE

    cond = jnp.zeros((8, d), f32).at[0:n_batch].set(c).at[n_batch].set(c_ctx)
    mods_all = _modulation(cond, w_mod, b_mod)
    cos_t, sin_t = _rope_tables(seq_len, n_batch, n_batch * n_ctx)
    rw_f = jnp.transpose(router_w)
    rw_hi = rw_f.astype(bf16)
    rw_t = jnp.concatenate([rw_hi, (rw_f - rw_hi.astype(f32)).astype(bf16)], axis=0)
    rb_col = router_b.reshape(N_EXPERTS, 1)
    final_g = final_norm_g.reshape(1, d)

    w_in_b = jnp.pad(w_in, ((0, 0), (0, 0), (0, 96))).astype(bf16)
    w_gate_b = w_gate.astype(bf16)
    w_branch_b = w_branch.astype(bf16)
    w_out_b = w_out.astype(bf16)

    n_lat_tiles = n_lat // ROW_TILE
    xa, xb, b_tile0 = x.reshape(n_lat, d), ctx.reshape(n_batch * n_ctx, d), 0
    for l in range(depth):
        last = l == depth - 1
        n_out = n_lat if last else n_all
        mods = mods_all[l].reshape(8, 1, 6 * d)
        w2p = jnp.zeros((128, 512), f32)
        w2p = w2p.at[0:GLA_RANK, 0:256].set(g_w2[l, 0]).at[GLA_RANK:2 * GLA_RANK, 256:512].set(g_w2[l, 1])
        w2_hi = w2p.astype(bf16)
        w2_split = jnp.concatenate([w2_hi, w2_hi, (w2p - w2_hi.astype(f32)).astype(bf16)], axis=0)
        bs_b = jnp.broadcast_to(c_bs[l][:, :, None], (4, SG_CHUNK, 128))
        h, qkv, z_pool, s_br, z_gla, gates = _in_proj(
            xa, xb, n_lat_tiles, b_tile0, n_all, l, norm1_g[l].reshape(1, d), mods, cos_t, sin_t,
            w_in_b, w2_split, g_b[l].reshape(1, 512), c_ln_g[l].reshape(1, 512),
            c_ln_b[l].reshape(1, 512), c_ws[l].astype(bf16), bs_b, tiles_per_mod)
        a_br = _window_attention(qkv, a_sink[l], n_batch, seq_len, n_ctx)
        pw = p_w[l].astype(bf16)
        ps = p_scale[l].reshape(1, 512)
        p_br = _multi_scale_pool(z_pool, 0, n_lat, seq_len, ROW_TILE, pw, ps)
        o_f, o_b = _gla_scan(z_gla, gates, n_batch, seq_len, n_ctx)
        if not last:
            a_c = _context_attention(qkv, a_sink[l], n_batch, n_lat, n_ctx)
            p_c = _multi_scale_pool(z_pool, n_lat, n_batch * n_ctx, n_ctx, n_ctx, pw, ps)
            a_br = jnp.concatenate([a_br, a_c], axis=0)
            p_br = jnp.concatenate([p_br, p_c], axis=0)
        acc = _gated_merge(h, (a_br, p_br, s_br), o_f, o_b, z_gla, g_norm_g[l].reshape(1, 512),
                           n_out, l, w_gate_b, w_branch_b)
        x_all = _out_proj_residual(acc, l, w_out_b, xa, xb, n_lat_tiles, b_tile0, mods, n_out,
                                   tiles_per_mod)
        x_all = _moe(x_all, n_out, norm2_g[l].reshape(1, d), mods, tiles_per_mod, rw_t, rb_col,
                     l, e_gate, e_up, e_down, final_g, last)
        xa, xb, b_tile0 = x_all, x_all, n_lat_tiles
    return x_all.reshape(n_batch, seq_len, d)
```

```python
import functools

import numpy as np
import jax
import jax.numpy as jnp
from jax import lax
from jax.experimental import pallas as pl
from jax.experimental.pallas import tpu as pltpu

f32 = jnp.float32
bf16 = jnp.bfloat16

EPS = 1e-6
GRID_W = 64
ROPE_BASE = 10000.0

HEAD_DIM = 64
N_Q_HEADS = 8
N_KV_HEADS = 2
Q_PER_KV = N_Q_HEADS // N_KV_HEADS
ATT_BLOCK = 128
POOL_WINDOWS = (2, 4, 8, 16)
POOL_HALO = 8
SG_CHUNK = 128
GLA_HEADS = 4
GLA_DK = 64
GLA_DV = 128
GLA_RANK = 16
GLA_TAU = 16.0
GLA_CHUNK = 64
N_EXPERTS = 16
EXPERTS_PER_GROUP = 4
N_GROUPS = 4

ROW_TILE = 512
GLA_BLOCK = 256
MOE_TILE = 256
V7X_VMEM_BYTES = 64 * 1024 * 1024
VMEM_LIMIT = V7X_VMEM_BYTES * 7 // 8
EXPERT_VMEM_LIMIT = V7X_VMEM_BYTES * 31 // 32


def _cparams(*sem, vmem_limit=VMEM_LIMIT):
    return pltpu.CompilerParams(dimension_semantics=sem, vmem_limit_bytes=vmem_limit)


def _dot(a, b, dims=(((1,), (0,)), ((), ())), precision=None):
    return lax.dot_general(a, b, dims, precision=precision, preferred_element_type=f32)


_NT = (((1,), (1,)), ((), ()))
_TN = (((0,), (0,)), ((), ()))


def _mod_kernel(c_ref, w_ref, b_ref, o_ref):
    c = c_ref[...]
    a = c * jax.nn.sigmoid(c)
    w = w_ref[...]
    a_hi = a.astype(bf16)
    a_lo = (a - a_hi.astype(f32)).astype(bf16)
    w_hi = w.astype(bf16)
    w_lo = (w - w_hi.astype(f32)).astype(bf16)
    n = a.shape[0]
    both = _dot(jnp.concatenate([a_hi, a_lo], axis=0), w_hi)
    o_ref[...] = both[0:n] + both[n:2 * n] + _dot(a_hi, w_lo) + b_ref[...]


def _modulation(cond, w_mod, b_mod):
    n_layers, d, d6 = w_mod.shape
    tn = 1024
    return pl.pallas_call(
        _mod_kernel,
        grid=(n_layers, d6 // tn),
        in_specs=[
            pl.BlockSpec((8, d), lambda l, j: (0, 0)),
            pl.BlockSpec((None, d, tn), lambda l, j: (l, 0, j)),
            pl.BlockSpec((None, 1, tn), lambda l, j: (l, 0, j)),
        ],
        out_specs=pl.BlockSpec((None, 8, tn), lambda l, j: (l, 0, j)),
        out_shape=jax.ShapeDtypeStruct((n_layers, 8, d6), f32),
        compiler_params=_cparams("arbitrary", "arbitrary"),
        name="modulation",
    )(cond, w_mod, b_mod.reshape(n_layers, 1, d6))


def _norm_mod(x, g, shift, scale):
    y = x * lax.rsqrt(jnp.mean(x * x, axis=-1, keepdims=True) + EPS)
    return (y * g) * (1.0 + scale) + shift


def _two_source_rows(xa_ref, xb_ref, n_a_tiles):
    return jnp.where(pl.program_id(0) < n_a_tiles, xa_ref[...], xb_ref[...])


def _two_source_specs(tm, d, n_a_tiles, b_tile0):
    return [pl.BlockSpec((tm, d), lambda i: (jnp.minimum(i, n_a_tiles - 1), 0)),
            pl.BlockSpec((tm, d), lambda i: (b_tile0 + jnp.maximum(i - n_a_tiles, 0), 0),
                         pipeline_mode=pl.Buffered(1))]


def _spatial_gate_rows(uv, lg_ref, lb_ref, ws_ref, bs_ref, o_ref):
    a = jax.nn.gelu(uv)
    u = a[:, 0:512]
    v = a[:, 512:1024]
    mu = jnp.mean(v, axis=-1, keepdims=True)
    var = jnp.mean(jnp.square(v - mu), axis=-1, keepdims=True)
    vn = ((v - mu) * lax.rsqrt(var + EPS) * lg_ref[...] + lb_ref[...]).astype(bf16)
    for c in range(uv.shape[0] // SG_CHUNK):
        rows = slice(c * SG_CHUNK, (c + 1) * SG_CHUNK)
        for g in range(4):
            cols = slice(g * 128, (g + 1) * 128)
            mixed = _dot(ws_ref[g], vn[rows, cols]) + bs_ref[g]
            o_ref[rows, cols] = (u[rows, cols] * mixed).astype(o_ref.dtype)


def _in_proj_kernel(xa_ref, xb_ref, g_ref, sh_ref, sc_ref, cos_ref, sin_ref, w_ref,
                    w2_ref, gb_ref, lg_ref, lb_ref, ws_ref, bs_ref,
                    h_ref, qkv_ref, pool_ref, sg_ref, gla_ref, gate_ref, *, n_a_tiles):
    x = _two_source_rows(xa_ref, xb_ref, n_a_tiles)
    hb = _norm_mod(x, g_ref[...], sh_ref[...], sc_ref[...]).astype(bf16)
    h_ref[...] = hb

    cos = cos_ref[...]
    sin = sin_ref[...]
    lane = lax.broadcasted_iota(jnp.int32, cos.shape, 1)
    first_half = (lane % 32) < 16

    def rope(z):
        rot = jnp.where(first_half, pltpu.roll(z, 112, 1), pltpu.roll(z, 16, 1))
        return z * cos + rot * sin

    zq = _dot(hb, w_ref[:, 0:512])
    for c in range(4):
        qkv_ref[:, c * 128:(c + 1) * 128] = (
            rope(zq[:, c * 128:(c + 1) * 128]) * (HEAD_DIM ** -0.5)).astype(bf16)
    zkv = _dot(hb, w_ref[:, 512:768])
    qkv_ref[:, 512:640] = rope(zkv[:, 0:128]).astype(bf16)
    qkv_ref[:, 640:768] = zkv[:, 128:256].astype(bf16)
    pool_ref[...] = _dot(hb, w_ref[:, 768:1280])
    _spatial_gate_rows(_dot(hb, w_ref[:, 1280:2304]), lg_ref, lb_ref, ws_ref, bs_ref, sg_ref)
    gla_ref[...] = _dot(hb, w_ref[:, 2304:3840])
    low_rank = _dot(hb, w_ref[:, 3840:3968])
    lr_hi = low_rank.astype(bf16)
    lr_lo = (low_rank - lr_hi.astype(f32)).astype(bf16)
    logit = _dot(jnp.concatenate([lr_hi, lr_lo, lr_hi], axis=1), w2_ref[...]) + gb_ref[...]
    gate_ref[...] = _log_sigmoid(logit) / GLA_TAU


def _log_sigmoid(x):
    return jnp.minimum(x, 0.0) - jnp.log(1.0 + jnp.exp(-jnp.abs(x)))


def _in_proj(xa, xb, n_a_tiles, b_tile0, rows, layer, g, mods, cos_t, sin_t, w_pad, w2p, gbias,
             ln_g, ln_b, ws, bs_b, tiles_per_mod):
    d = xa.shape[1]
    n_w = w_pad.shape[2]
    tm = ROW_TILE
    mod_spec = lambda k: pl.BlockSpec((None, 1, d), lambda i: (i // tiles_per_mod, 0, k))
    row_spec = lambda w: pl.BlockSpec((tm, w), lambda i: (i, 0))
    return pl.pallas_call(
        functools.partial(_in_proj_kernel, n_a_tiles=n_a_tiles),
        grid=(rows // tm,),
        in_specs=_two_source_specs(tm, d, n_a_tiles, b_tile0) + [
            pl.BlockSpec((1, d), lambda i: (0, 0)),
            mod_spec(0), mod_spec(1),
            row_spec(128), row_spec(128),
            pl.BlockSpec((None, d, n_w), lambda i: (layer, 0, 0), pipeline_mode=pl.Buffered(1)),
            pl.BlockSpec((384, 512), lambda i: (0, 0)),
            pl.BlockSpec((1, 512), lambda i: (0, 0)),
            pl.BlockSpec((1, 512), lambda i: (0, 0)),
            pl.BlockSpec((1, 512), lambda i: (0, 0)),
            pl.BlockSpec((4, 128, 128), lambda i: (0, 0, 0)),
            pl.BlockSpec((4, 128, 128), lambda i: (0, 0, 0)),
        ],
        out_specs=[row_spec(d), row_spec(768), row_spec(512), row_spec(512), row_spec(1536),
                   row_spec(512)],
        out_shape=[
            jax.ShapeDtypeStruct((rows, d), bf16),
            jax.ShapeDtypeStruct((rows, 768), bf16),
            jax.ShapeDtypeStruct((rows, 512), f32),
            jax.ShapeDtypeStruct((rows, 512), bf16),
            jax.ShapeDtypeStruct((rows, 1536), f32),
            jax.ShapeDtypeStruct((rows, 512), f32),
        ],
        compiler_params=_cparams("arbitrary"),
        name="in_proj",
    )(xa, xb, g, mods, mods, cos_t, sin_t, w_pad, w2p, gbias, ln_g, ln_b, ws, bs_b)


def _sink_column(sink_ref, kvh, rows_per_head):
    r = lax.broadcasted_iota(jnp.int32, (Q_PER_KV * rows_per_head, 1), 0) // rows_per_head
    col = jnp.full(r.shape, sink_ref[kvh * Q_PER_KV], f32)
    for g in range(1, Q_PER_KV):
        col = jnp.where(r == g, sink_ref[kvh * Q_PER_KV + g], col)
    return col


def _win_attn_kernel(sink_ref, main_ref, prev_ref, next_ref, ctx_ref, o_ref, *, tiles_per_seq, seq_len):
    i = pl.program_id(0)
    tq = main_ref.shape[0]
    n_sub = tq // ATT_BLOCK
    blk0 = (i % tiles_per_seq) * n_sub
    blocks_per_seq = seq_len // ATT_BLOCK
    r = lax.broadcasted_iota(jnp.int32, (Q_PER_KV * ATT_BLOCK, ATT_BLOCK), 0) % ATT_BLOCK
    j = lax.broadcasted_iota(jnp.int32, (Q_PER_KV * ATT_BLOCK, ATT_BLOCK), 1)
    keep_prev = j >= r
    keep_next = j <= r
    n_ctx = ctx_ref.shape[0]

    def with_ones(v):
        return jnp.concatenate([v, jnp.ones_like(v)], axis=1)

    for kvh in range(N_KV_HEADS):
        kc = 512 + kvh * HEAD_DIM
        vc = 640 + kvh * HEAD_DIM
        k_all = jnp.concatenate([prev_ref[:, kc:kc + HEAD_DIM], main_ref[:, kc:kc + HEAD_DIM],
                                 next_ref[:, kc:kc + HEAD_DIM]], axis=0)
        v_all = with_ones(jnp.concatenate([prev_ref[:, vc:vc + HEAD_DIM], main_ref[:, vc:vc + HEAD_DIM],
                                           next_ref[:, vc:vc + HEAD_DIM]], axis=0))
        k_ctx = ctx_ref[:, kc:kc + HEAD_DIM]
        v_ctx = with_ones(ctx_ref[:, vc:vc + HEAD_DIM])
        sink = _sink_column(sink_ref, kvh, ATT_BLOCK)
        for sb in range(n_sub):
            rows = slice(sb * ATT_BLOCK, (sb + 1) * ATT_BLOCK)
            q = jnp.concatenate(
                [main_ref[rows, (kvh * Q_PER_KV + g) * HEAD_DIM:(kvh * Q_PER_KV + g + 1) * HEAD_DIM]
                 for g in range(Q_PER_KV)], axis=0)
            band = slice(sb * ATT_BLOCK, (sb + 3) * ATT_BLOCK)
            s_band = _dot(q, k_all[band], _NT)
            s_ctx = _dot(q, k_ctx, _NT)
            blk = blk0 + sb
            parts = [jnp.where(keep_prev & (blk >= 1), s_band[:, 0:ATT_BLOCK], -1e30),
                     s_band[:, ATT_BLOCK:2 * ATT_BLOCK],
                     jnp.where(keep_next & (blk <= blocks_per_seq - 2), s_band[:, 2 * ATT_BLOCK:], -1e30)]
            parts += [s_ctx[:, c:c + 128] for c in range(0, n_ctx, 128)]
            m = jnp.maximum(sink, jnp.max(functools.reduce(jnp.maximum, parts), axis=-1, keepdims=True))
            p = jnp.concatenate([jnp.exp(x - m).astype(bf16) for x in parts], axis=1)
            o_sum = (_dot(p[:, 0:3 * ATT_BLOCK], v_all[band]) + _dot(p[:, 3 * ATT_BLOCK:], v_ctx))
            denom = jnp.exp(sink - m) + o_sum[:, HEAD_DIM:HEAD_DIM + 1]
            o = o_sum[:, 0:HEAD_DIM] / denom
            for g in range(Q_PER_KV):
                c0 = (kvh * Q_PER_KV + g) * HEAD_DIM
                o_ref[rows, c0:c0 + HEAD_DIM] = o[g * ATT_BLOCK:(g + 1) * ATT_BLOCK].astype(o_ref.dtype)


def _window_attention(qkv, sink, n_batch, seq_len, n_ctx):
    tq = ROW_TILE
    n_lat = n_batch * seq_len
    tiles_per_seq = seq_len // tq
    sub = tq // ATT_BLOCK
    n_blocks = n_lat // ATT_BLOCK
    ctx_blk0 = n_lat // n_ctx
    w = qkv.shape[1]
    kern = functools.partial(_win_attn_kernel, tiles_per_seq=tiles_per_seq, seq_len=seq_len)
    return pl.pallas_call(
        kern,
        grid=(n_lat // tq,),
        in_specs=[
            pl.BlockSpec(memory_space=pltpu.SMEM),
            pl.BlockSpec((tq, w), lambda i: (i, 0)),
            pl.BlockSpec((ATT_BLOCK, w), lambda i: (jnp.maximum(i * sub - 1, 0), 0)),
            pl.BlockSpec((ATT_BLOCK, w), lambda i: (jnp.minimum(i * sub + sub, n_blocks - 1), 0)),
            pl.BlockSpec((n_ctx, w), lambda i: (ctx_blk0 + i // tiles_per_seq, 0)),
        ],
        out_specs=pl.BlockSpec((tq, 512), lambda i: (i, 0)),
        out_shape=jax.ShapeDtypeStruct((n_lat, 512), bf16),
        compiler_params=_cparams("arbitrary"),
        name="window_attention",
    )(sink, qkv, qkv, qkv, qkv)


def _ctx_attn_kernel(sink_ref, qkv_ref, o_ref):
    n = qkv_ref.shape[0]
    for kvh in range(N_KV_HEADS):
        kc = 512 + kvh * HEAD_DIM
        vc = 640 + kvh * HEAD_DIM
        q = jnp.concatenate(
            [qkv_ref[:, (kvh * Q_PER_KV + g) * HEAD_DIM:(kvh * Q_PER_KV + g + 1) * HEAD_DIM]
             for g in range(Q_PER_KV)], axis=0)
        s = _dot(q, qkv_ref[:, kc:kc + HEAD_DIM], _NT)
        sink = _sink_column(sink_ref, kvh, n)
        m = jnp.maximum(sink, jnp.max(s, axis=-1, keepdims=True))
        p = jnp.exp(s - m)
        denom = jnp.exp(sink - m) + jnp.sum(p, axis=-1, keepdims=True)
        o = _dot(p.astype(bf16), qkv_ref[:, vc:vc + HEAD_DIM]) / denom
        for g in range(Q_PER_KV):
            c0 = (kvh * Q_PER_KV + g) * HEAD_DIM
            o_ref[:, c0:c0 + HEAD_DIM] = o[g * n:(g + 1) * n].astype(o_ref.dtype)


def _context_attention(qkv, sink, n_batch, n_lat, n_ctx):
    w = qkv.shape[1]
    blk0 = n_lat // n_ctx
    return pl.pallas_call(
        _ctx_attn_kernel,
        grid=(n_batch,),
        in_specs=[pl.BlockSpec(memory_space=pltpu.SMEM),
                  pl.BlockSpec((n_ctx, w), lambda b: (blk0 + b, 0))],
        out_specs=pl.BlockSpec((n_ctx, 512), lambda b: (b, 0)),
        out_shape=jax.ShapeDtypeStruct((n_batch * n_ctx, 512), bf16),
        compiler_params=_cparams("arbitrary"),
        name="context_attention",
    )(sink, qkv)


def _pool_kernel(main_ref, prev_ref, next_ref, pw_ref, ps_ref, o_ref, xe_ref, *, tiles_per_seq, seq_len):
    i = pl.program_id(0)
    tp = main_ref.shape[0]
    t_in_seq = i % tiles_per_seq
    h = POOL_HALO
    xe_ref[0:h, :] = jnp.where(t_in_seq == 0, 0.0, prev_ref[...])
    xe_ref[h:h + tp, :] = main_ref[...]
    xe_ref[h + tp:2 * h + tp, :] = jnp.where(t_in_seq == tiles_per_seq - 1, 0.0, next_ref[...])
    pos = t_in_seq * tp + lax.broadcasted_iota(jnp.int32, (tp, 1), 0)
    for gi, w in enumerate(POOL_WINDOWS):
        cols = slice(gi * 128, (gi + 1) * 128)
        acc = xe_ref[h - w // 2:h - w // 2 + tp, cols]
        for u in range(-w // 2 + 1, w // 2):
            acc = acc + xe_ref[h + u:h + u + tp, cols]
        lo = jnp.maximum(pos - w // 2, 0)
        hi = jnp.minimum(pos + w // 2, seq_len)
        cnt = (hi - lo).astype(f32)
        pooled = acc / cnt - main_ref[:, cols]
        y = _dot(pooled.astype(bf16), pw_ref[gi])
        o_ref[:, cols] = (y * ps_ref[:, cols]).astype(o_ref.dtype)


def _multi_scale_pool(z, row0, n_rows, seq_len, tp, p_w, p_scale):
    tiles_per_seq = seq_len // tp
    t0 = row0 // tp
    h0 = row0 // POOL_HALO
    hb = tp // POOL_HALO
    n_halo = n_rows // POOL_HALO
    kern = functools.partial(_pool_kernel, tiles_per_seq=tiles_per_seq, seq_len=seq_len)
    return pl.pallas_call(
        kern,
        grid=(n_rows // tp,),
        in_specs=[
            pl.BlockSpec((tp, 512), lambda i: (t0 + i, 0)),
            pl.BlockSpec((POOL_HALO, 512), lambda i: (h0 + jnp.maximum(i * hb - 1, 0), 0)),
            pl.BlockSpec((POOL_HALO, 512), lambda i: (h0 + jnp.minimum((i + 1) * hb, n_halo - 1), 0)),
            pl.BlockSpec((4, 128, 128), lambda i: (0, 0, 0)),
            pl.BlockSpec((1, 512), lambda i: (0, 0)),
        ],
        out_specs=pl.BlockSpec((tp, 512), lambda i: (i, 0)),
        out_shape=jax.ShapeDtypeStruct((n_rows, 512), bf16),
        scratch_shapes=[pltpu.VMEM((tp + 2 * POOL_HALO, 512), f32)],
        compiler_params=_cparams("arbitrary"),
        name="multi_scale_pool",
    )(z, z, z, p_w, p_scale)


GLA_SEGMENTS = (1, 2, 4, 8, 16, 32, 64)


def _gla_tables():
    c = GLA_CHUNK
    i = np.arange(c)[:, None]
    j = np.arange(c)[None, :]
    tri = (j <= i).astype(np.float32)
    mask = np.zeros((2, 7, c, c), np.float32)
    mask[:, 0] = np.eye(c)
    for lv, s in enumerate(GLA_SEGMENTS[:-1]):
        m = ((i // (2 * s)) == (j // (2 * s))) & ((i // s) % 2 == 1) & ((j // s) % 2 == 0)
        mask[0, 1 + lv] = m
        mask[1, 1 + lv] = m.T
    mask = np.tile(mask, (1, 1, 1, GLA_HEADS))
    return tri, mask


def _segment_sums(g, cum):
    c = GLA_CHUNK
    row = lax.broadcasted_iota(jnp.int32, (c, 1), 0)
    zero = jnp.zeros_like(g)
    before = [None] + [pltpu.roll(g, kk, 0) for kk in (1, 2, 3)]
    after = [None] + [pltpu.roll(g, c - kk, 0) for kk in (1, 2, 3)]
    a, r = {1: g}, {1: zero}
    for s in (2, 4):
        pos = row % s
        a_s, r_s = g, zero
        for kk in range(1, s):
            a_s = a_s + jnp.where(pos >= kk, before[kk], 0.0)
            r_s = r_s + jnp.where(pos < s - kk, after[kk], 0.0)
        a[s], r[s] = a_s, r_s
    blocks = [cum[8 * b:8 * b + 8] for b in range(c // 8)]
    last = [cum[8 * b + 7:8 * b + 8] for b in range(c // 8)]
    for s in (8, 16, 32, 64):
        nb = s // 8
        a_blk, r_blk = [], []
        for b in range(c // 8):
            prev_end = (b // nb) * nb - 1
            a_blk.append(blocks[b] - last[prev_end] if prev_end >= 0 else blocks[b])
            r_blk.append(last[(b // nb + 1) * nb - 1] - blocks[b])
        a[s] = jnp.concatenate(a_blk, axis=0)
        r[s] = jnp.concatenate(r_blk, axis=0)
    return a, r


def _gla_chunk(d, q, k, v, g, tri_ref, mask_ref, st_ref):
    c = GLA_CHUNK
    w = g.shape[1]
    g_hi = g.astype(bf16)
    r1 = g - g_hi.astype(f32)
    g_mid = r1.astype(bf16)
    g_lo = (r1 - g_mid.astype(f32)).astype(bf16)
    cum3 = _dot(tri_ref[...], jnp.concatenate([g_hi, g_mid, g_lo], axis=1))
    cum = cum3[:, 0:w] + cum3[:, w:2 * w] + cum3[:, 2 * w:3 * w]
    a, r = _segment_sums(g, cum)
    if d == 0:
        cq = [a[s] for s in GLA_SEGMENTS]
        ck = [r[s] for s in GLA_SEGMENTS]
    else:
        cq = [r[s] + g for s in GLA_SEGMENTS]
        ck = [a[s] - g for s in GLA_SEGMENTS]
    total = cum[c - 1:c]

    q = q * (GLA_DK ** -0.5)
    head_of_lane = lax.broadcasted_iota(jnp.int32, (1, w), 1) // GLA_DK
    vhead_of_lane = lax.broadcasted_iota(jnp.int32, (1, v.shape[1]), 1) // GLA_DV

    def stack_heads(x, lane_head):
        return jnp.concatenate([jnp.where(lane_head == h, x, jnp.zeros_like(x))
                                for h in range(GLA_HEADS)], axis=0)

    kb = k.astype(bf16)
    k_plain = stack_heads(kb, head_of_lane)
    q01 = jnp.concatenate([q.astype(bf16), (q * jnp.exp(cq[0])).astype(bf16)], axis=0)
    s01 = _dot(q01, k_plain, _NT)
    att = mask_ref[d, 0] * s01[0:c] + mask_ref[d, 1] * s01[c:2 * c]
    for lv in range(1, 6):
        qs = (q * jnp.exp(cq[lv])).astype(bf16)
        ks = stack_heads((k * jnp.exp(ck[lv])).astype(bf16), head_of_lane)
        att = att + mask_ref[d, 1 + lv] * _dot(qs, ks, _NT)
    v_heads = stack_heads(v.astype(bf16), vhead_of_lane)
    st = st_ref[d]
    q_state = (q * jnp.exp(cq[6])).astype(bf16)
    o = _dot(att.astype(bf16), v_heads) + _dot(q_state, st.astype(bf16), _NT)
    k_state = stack_heads((k * jnp.exp(ck[6])).astype(bf16), head_of_lane)
    st_ref[d] = st * jnp.exp(total) + _dot(v_heads, k_state, _TN)
    return o


def _gla_kernel(qf_ref, kf_ref, vf_ref, gf_ref, qb_ref, kb_ref, vb_ref, gb_ref,
                tri_ref, mask_ref, of_ref, ob_ref, st_ref):
    @pl.when(pl.program_id(1) == 0)
    def _():
        st_ref[...] = jnp.zeros_like(st_ref)

    n_chunks = qf_ref.shape[0] // GLA_CHUNK
    for ci in range(n_chunks):
        rf = slice(ci * GLA_CHUNK, (ci + 1) * GLA_CHUNK)
        of_ref[rf, :] = _gla_chunk(0, qf_ref[rf, :], kf_ref[rf, :], vf_ref[rf, :], gf_ref[rf, :],
                                   tri_ref, mask_ref, st_ref)
        cb = n_chunks - 1 - ci
        rb = slice(cb * GLA_CHUNK, (cb + 1) * GLA_CHUNK)
        ob_ref[rb, :] = _gla_chunk(1, qb_ref[rb, :], kb_ref[rb, :], vb_ref[rb, :], gb_ref[rb, :],
                                   tri_ref, mask_ref, st_ref)


def _gla_scan(z_gla, gates, n_batch, seq_len, n_ctx):
    rows = z_gla.shape[0]
    tb = GLA_BLOCK
    assert n_ctx == tb
    lat_blocks = seq_len // tb
    ctx_blk0 = n_batch * lat_blocks
    tri, mask = _gla_tables()

    def fwd_row(b, s):
        return jnp.where(s == 0, ctx_blk0 + b, b * lat_blocks + s - 1)

    def bwd_row(b, s):
        return jnp.where(s == 0, ctx_blk0 + b, b * lat_blocks + lat_blocks - s)

    def specs(row, direction):
        return [
            pl.BlockSpec((tb, 256), lambda b, s: (row(b, s), 0)),
            pl.BlockSpec((tb, 256), lambda b, s: (row(b, s), 1)),
            pl.BlockSpec((tb, 512), lambda b, s: (row(b, s), 1)),
            pl.BlockSpec((tb, 256), lambda b, s: (row(b, s), direction)),
        ]

    const = lambda shape: pl.BlockSpec(shape, lambda b, s: (0,) * len(shape))
    return pl.pallas_call(
        _gla_kernel,
        grid=(n_batch, 1 + lat_blocks),
        in_specs=specs(fwd_row, 0) + specs(bwd_row, 1) + [
            const(tri.shape), const(mask.shape)],
        out_specs=[pl.BlockSpec((tb, 512), lambda b, s: (fwd_row(b, s), 0)),
                   pl.BlockSpec((tb, 512), lambda b, s: (bwd_row(b, s), 0))],
        out_shape=[jax.ShapeDtypeStruct((rows, 512), f32)] * 2,
        scratch_shapes=[pltpu.VMEM((2, GLA_HEADS * GLA_DV, GLA_HEADS * GLA_DK), f32)],
        compiler_params=_cparams("arbitrary", "arbitrary"),
        name="gla_scan",
    )(z_gla, z_gla, z_gla, gates, z_gla, z_gla, z_gla, gates,
      jnp.asarray(tri, bf16), jnp.asarray(mask, f32))


def _merge_kernel(h_ref, a_ref, p_ref, s_ref, of_ref, ob_ref, r_ref, ng_ref, wg_ref, wb_ref, o_ref, g_ref):
    @pl.when(pl.program_id(1) == 0)
    def _():
        o = of_ref[...] + ob_ref[...]
        r = r_ref[...]
        gate = r * jax.nn.sigmoid(r)
        for h in range(GLA_HEADS):
            cols = slice(h * GLA_DV, (h + 1) * GLA_DV)
            oh = o[:, cols]
            oh = oh * lax.rsqrt(jnp.mean(oh * oh, axis=-1, keepdims=True) + EPS)
            g_ref[:, cols] = (oh * ng_ref[:, cols] * gate[:, cols]).astype(g_ref.dtype)

    hb = h_ref[...]
    acc = None
    for br, b_ref in enumerate((a_ref, p_ref, s_ref, g_ref)):
        gate = jax.nn.sigmoid(_dot(hb, wg_ref[br]))
        term = gate * _dot(b_ref[...], wb_ref[br])
        acc = term if acc is None else acc + term
    o_ref[...] = acc.astype(o_ref.dtype)


def _gated_merge(h, branches, o_f, o_b, z_gla, norm_g, n_rows, layer, wg, wb):
    d = h.shape[1]
    tm, tn = ROW_TILE, 512
    bspec = pl.BlockSpec((tm, 512), lambda i, j: (i, 0))
    return pl.pallas_call(
        _merge_kernel,
        grid=(n_rows // tm, d // tn),
        in_specs=[pl.BlockSpec((tm, d), lambda i, j: (i, 0)), bspec, bspec, bspec,
                  bspec, bspec,
                  pl.BlockSpec((tm, 512), lambda i, j: (i, 2)),
                  pl.BlockSpec((1, 512), lambda i, j: (0, 0)),
                  pl.BlockSpec((None, 4, d, tn), lambda i, j: (layer, 0, 0, j)),
                  pl.BlockSpec((None, 4, 512, tn), lambda i, j: (layer, 0, 0, j))],
        out_specs=pl.BlockSpec((tm, tn), lambda i, j: (i, j)),
        out_shape=jax.ShapeDtypeStruct((n_rows, d), bf16),
        scratch_shapes=[pltpu.VMEM((tm, 512), bf16)],
        compiler_params=_cparams("arbitrary", "arbitrary"),
        name="gated_merge",
    )(h, *branches, o_f, o_b, z_gla, norm_g, wg, wb)


def _out_proj_kernel(acc_ref, wo_ref, xa_ref, xb_ref, gate_ref, o_ref, *, n_a_tiles):
    x = _two_source_rows(xa_ref, xb_ref, n_a_tiles)
    o_ref[...] = x + gate_ref[...] * _dot(acc_ref[...], wo_ref[...])


def _out_proj_residual(acc, layer, wo, xa, xb, n_a_tiles, b_tile0, mods, n_rows, tiles_per_mod):
    d = acc.shape[1]
    tm = ROW_TILE
    return pl.pallas_call(
        functools.partial(_out_proj_kernel, n_a_tiles=n_a_tiles),
        grid=(n_rows // tm,),
        in_specs=[pl.BlockSpec((tm, d), lambda i: (i, 0)),
                  pl.BlockSpec((None, d, d), lambda i: (layer, 0, 0))]
        + _two_source_specs(tm, d, n_a_tiles, b_tile0)
        + [pl.BlockSpec((None, 1, d), lambda i: (i // tiles_per_mod, 0, 2))],
        out_specs=pl.BlockSpec((tm, d), lambda i: (i, 0)),
        out_shape=jax.ShapeDtypeStruct((n_rows, d), f32),
        compiler_params=_cparams("arbitrary"),
        name="out_proj_residual",
    )(acc, wo, xa, xb, mods)


def _router_kernel(x_ref, g_ref, sh_ref, sc_ref, rw_ref, rb_ref, tri_ref, info_ref, cnt_ref, carry_ref):
    i = pl.program_id(0)

    @pl.when(i == 0)
    def _():
        carry_ref[...] = jnp.zeros_like(carry_ref)

    h2 = _norm_mod(x_ref[...], g_ref[...], sh_ref[...], sc_ref[...])
    tm = h2.shape[0]
    h_hi = h2.astype(bf16)
    h_lo = (h2 - h_hi.astype(f32)).astype(bf16)
    part = _dot(rw_ref[...], h_hi, _NT)
    logits = (part[0:N_EXPERTS] + part[N_EXPERTS:2 * N_EXPERTS]
              + _dot(rw_ref[0:N_EXPERTS, :], h_lo, _NT))
    score = jax.nn.sigmoid(logits)
    biased = score + rb_ref[...]
    b = [biased[e:e + 1] for e in range(N_EXPERTS)]
    sc = [score[e:e + 1] for e in range(N_EXPERTS)]

    def top2_sum(v):
        hi01, lo01 = jnp.maximum(v[0], v[1]), jnp.minimum(v[0], v[1])
        hi23, lo23 = jnp.maximum(v[2], v[3]), jnp.minimum(v[2], v[3])
        return jnp.maximum(hi01, hi23) + jnp.maximum(jnp.minimum(hi01, hi23), jnp.maximum(lo01, lo23))

    best = jnp.zeros((1, tm), jnp.int32)
    best_score = top2_sum(b[0:4])
    for gidx in range(1, N_GROUPS):
        gs = top2_sum(b[4 * gidx:4 * gidx + 4])
        take = gs > best_score
        best = jnp.where(take, gidx, best)
        best_score = jnp.where(take, gs, best_score)
    cb, cs = [], []
    for kk in range(EXPERTS_PER_GROUP):
        vb, vs = b[kk], sc[kk]
        for gidx in range(1, N_GROUPS):
            vb = jnp.where(best == gidx, b[4 * gidx + kk], vb)
            vs = jnp.where(best == gidx, sc[4 * gidx + kk], vs)
        cb.append(vb)
        cs.append(vs)
    i1 = jnp.zeros((1, tm), jnp.int32)
    m1, w1 = cb[0], cs[0]
    for kk in range(1, EXPERTS_PER_GROUP):
        take = cb[kk] > m1
        i1 = jnp.where(take, kk, i1)
        m1 = jnp.where(take, cb[kk], m1)
        w1 = jnp.where(take, cs[kk], w1)
    i2 = jnp.where(i1 == 0, 1, 0).astype(jnp.int32)
    m2 = jnp.where(i1 == 0, cb[1], cb[0])
    w2 = jnp.where(i1 == 0, cs[1], cs[0])
    for kk in range(1, EXPERTS_PER_GROUP):
        take = (cb[kk] > m2) & (i1 != kk) & (i2 != kk)
        i2 = jnp.where(take, kk, i2)
        m2 = jnp.where(take, cb[kk], m2)
        w2 = jnp.where(take, cs[kk], w2)
    e1 = best * EXPERTS_PER_GROUP + i1
    e2 = best * EXPERTS_PER_GROUP + i2
    wsum = w1 + w2
    erow = lax.broadcasted_iota(jnp.int32, (N_EXPERTS, tm), 0)
    oh1 = (erow == e1).astype(f32)
    oh2 = (erow == e2).astype(f32)
    oh = oh1 + oh2
    rank = _dot(oh.astype(bf16), tri_ref[...]) + carry_ref[:, 0:1]
    carry_ref[...] = carry_ref[...] + jnp.sum(oh, axis=1, keepdims=True)
    r1 = jnp.sum(oh1 * rank, axis=0, keepdims=True)
    r2 = jnp.sum(oh2 * rank, axis=0, keepdims=True)
    info_ref[...] = jnp.concatenate(
        [e1.astype(f32), e2.astype(f32), w1 / wsum, w2 / wsum, r1, r2,
         jnp.zeros((2, tm), f32)], axis=0)
    cnt_ref[...] = carry_ref[...]


def _router(x_all, n_rows, g, mods, rw_t, rb_col, tiles_per_mod):
    d = x_all.shape[1]
    tm = ROW_TILE
    tri = jnp.asarray(np.triu(np.ones((tm, tm), np.float32), k=1), bf16)
    mod_spec = lambda k: pl.BlockSpec((None, 1, d), lambda i: (i // tiles_per_mod, 0, k))
    return pl.pallas_call(
        _router_kernel,
        grid=(n_rows // tm,),
        in_specs=[pl.BlockSpec((tm, d), lambda i: (i, 0)),
                  pl.BlockSpec((1, d), lambda i: (0, 0)),
                  mod_spec(3), mod_spec(4),
                  pl.BlockSpec((2 * N_EXPERTS, d), lambda i: (0, 0)),
                  pl.BlockSpec((N_EXPERTS, 1), lambda i: (0, 0)),
                  pl.BlockSpec((tm, tm), lambda i: (0, 0))],
        out_specs=[pl.BlockSpec((8, tm), lambda i: (0, i)),
                   pl.BlockSpec((N_EXPERTS, 128), lambda i: (0, 0))],
        out_shape=[jax.ShapeDtypeStruct((8, n_rows), f32),
                   jax.ShapeDtypeStruct((N_EXPERTS, 128), f32)],
        scratch_shapes=[pltpu.VMEM((N_EXPERTS, 128), f32)],
        compiler_params=_cparams("arbitrary"),
        name="moe_router",
    )(x_all, g, mods, mods, rw_t, rb_col, tri)


def _dispatch_kernel(pos_ref, zpos_ref, x_ref, g_ref, sh_ref, sc_ref, xs_ref, buf_ref, zero_ref, sem, zsem):
    i = pl.program_id(0)
    tm = x_ref.shape[0]

    @pl.when(i == 0)
    def _():
        zero_ref[...] = jnp.zeros_like(zero_ref)

        def zero_tile(row):
            cp = pltpu.make_async_copy(zero_ref, xs_ref.at[pl.ds(pl.multiple_of(row, 8), tm)], zsem)
            cp.start()
            cp.wait()

        for e in range(N_EXPERTS):
            zero_tile((zpos_ref[e] // 8) * 8)

        def tail(t, c):
            zero_tile(t * tm)
            return c

        lax.fori_loop(zpos_ref[N_EXPERTS], xs_ref.shape[0] // tm, tail, 0)

    slot = i % 2

    def wait_slot(s):
        for k in range(2):
            pltpu.make_async_copy(buf_ref.at[s], xs_ref.at[pl.ds(0, tm)], sem.at[s]).wait()

    @pl.when(i >= 2)
    def _():
        wait_slot(slot)

    buf_ref[slot] = _norm_mod(x_ref[...], g_ref[...], sh_ref[...], sc_ref[...])
    src = buf_ref.at[slot]
    for r in range(tm):
        for k in range(2):
            dst = pos_ref[k, i * tm + r]
            pltpu.make_async_copy(src.at[pl.ds(r, 1)], xs_ref.at[pl.ds(dst, 1)], sem.at[slot]).start()

    @pl.when(i == pl.num_programs(0) - 1)
    def _():
        wait_slot(slot)
        wait_slot(1 - slot)


def _dispatch(pos, zpos, x_all, n_rows, g, mods, tiles_per_mod, p_rows):
    d = x_all.shape[1]
    tm = MOE_TILE
    tpm = tiles_per_mod * (ROW_TILE // tm)
    mod_spec = lambda k: pl.BlockSpec((None, 1, d), lambda i, pos, zpos: (i // tpm, 0, k))
    return pl.pallas_call(
        _dispatch_kernel,
        grid_spec=pltpu.PrefetchScalarGridSpec(
            num_scalar_prefetch=2,
            grid=(n_rows // tm,),
            in_specs=[pl.BlockSpec((tm, d), lambda i, pos, zpos: (i, 0)),
                      pl.BlockSpec((1, d), lambda i, pos, zpos: (0, 0)),
                      mod_spec(3), mod_spec(4)],
            out_specs=pl.BlockSpec(memory_space=pl.ANY),
            scratch_shapes=[pltpu.VMEM((2, tm, d), f32), pltpu.VMEM((tm, d), f32),
                            pltpu.SemaphoreType.DMA((2,)), pltpu.SemaphoreType.DMA(())],
        ),
        out_shape=jax.ShapeDtypeStruct((p_rows, d), f32),
        compiler_params=_cparams("arbitrary"),
        name="moe_dispatch",
    )(pos, zpos, x_all, g, mods, mods)


def _expert_kernel(sched_ref, nu_ref, xs_ref, wg_hbm, wu_hbm, wd_hbm, y_ref, wg_buf, wu_buf, wd_buf, sem,
                   *, layer):
    i = pl.program_id(0)
    slot = sched_ref[2, i]

    def weight_copies(e, s):
        return [pltpu.make_async_copy(hbm.at[layer, e], buf.at[s], sem.at[s])
                for hbm, buf in ((wg_hbm, wg_buf), (wu_hbm, wu_buf), (wd_hbm, wd_buf))]

    @pl.when(i == 0)
    def _():
        for cp in weight_copies(sched_ref[0, 0], 0):
            cp.start()

    @pl.when((sched_ref[1, i] == 1) & (i < nu_ref[0]))
    def _():
        for cp in weight_copies(sched_ref[0, i], slot):
            cp.wait()
        nxt = sched_ref[3, i]

        @pl.when(nxt >= 0)
        def _():
            for cp in weight_copies(nxt, 1 - slot):
                cp.start()

    @pl.when(i < nu_ref[0])
    def _():
        x = xs_ref[...]
        a = _dot(x, wg_buf[slot])
        act = (a * jax.nn.sigmoid(a)) * _dot(x, wu_buf[slot])
        y_ref[...] = _dot(act, wd_buf[slot])

    @pl.when(i >= nu_ref[0])
    def _():
        y_ref[...] = jnp.zeros_like(y_ref)


def _expert_ffn(sched, n_used, xs, layer, wg, wu, wd):
    p_rows, d = xs.shape
    de = wg.shape[3]
    tm = MOE_TILE
    return pl.pallas_call(
        functools.partial(_expert_kernel, layer=layer),
        grid_spec=pltpu.PrefetchScalarGridSpec(
            num_scalar_prefetch=2,
            grid=(p_rows // tm,),
            in_specs=[pl.BlockSpec((tm, d), lambda i, sc, nu: (jnp.where(i < nu[0], i, 0), 0)),
                      pl.BlockSpec(memory_space=pl.ANY), pl.BlockSpec(memory_space=pl.ANY),
                      pl.BlockSpec(memory_space=pl.ANY)],
            out_specs=pl.BlockSpec((tm, d), lambda i, sc, nu: (i, 0)),
            scratch_shapes=[pltpu.VMEM((2, d, de), f32), pltpu.VMEM((2, d, de), f32),
                            pltpu.VMEM((2, de, d), f32), pltpu.SemaphoreType.DMA((2,))],
        ),
        out_shape=jax.ShapeDtypeStruct((p_rows, d), f32),
        compiler_params=_cparams("arbitrary", vmem_limit=EXPERT_VMEM_LIMIT),
        name="moe_experts",
    )(sched, n_used, xs, wg, wu, wd)


def _combine_kernel(pos_ref, x_ref, w_ref, gate_ref, fg_ref, y_ref, o_ref, ybuf_ref, sem, *, final_norm):
    i = pl.program_id(0)
    tm = x_ref.shape[0]
    slot = i % 2

    def gather(step, s):
        for r in range(tm):
            for k in range(2):
                src = pos_ref[k, step * tm + r]
                pltpu.make_async_copy(y_ref.at[pl.ds(src, 1)], ybuf_ref.at[s, k].at[pl.ds(r, 1)],
                                      sem.at[s]).start()

    @pl.when(i == 0)
    def _():
        gather(0, 0)

    @pl.when(i + 1 < pl.num_programs(0))
    def _():
        gather(i + 1, 1 - slot)

    for k in range(2):
        pltpu.make_async_copy(y_ref.at[pl.ds(0, tm)], ybuf_ref.at[slot, k], sem.at[slot]).wait()
    w = w_ref[...]
    moe = w[:, 0:1] * ybuf_ref[slot, 0] + w[:, 1:2] * ybuf_ref[slot, 1]
    x = x_ref[...] + gate_ref[...] * moe
    if final_norm:
        x = x * lax.rsqrt(jnp.mean(x * x, axis=-1, keepdims=True) + EPS) * fg_ref[...]
    o_ref[...] = x


def _combine(pos, x_all, n_rows, w_tok, mods, final_g, y, tiles_per_mod, final_norm):
    d = x_all.shape[1]
    tm = MOE_TILE
    tpm = tiles_per_mod * (ROW_TILE // tm)
    kern = functools.partial(_combine_kernel, final_norm=final_norm)
    return pl.pallas_call(
        kern,
        grid_spec=pltpu.PrefetchScalarGridSpec(
            num_scalar_prefetch=1,
            grid=(n_rows // tm,),
            in_specs=[pl.BlockSpec((tm, d), lambda i, pos: (i, 0)),
                      pl.BlockSpec((tm, 2), lambda i, pos: (i, 0)),
                      pl.BlockSpec((None, 1, d), lambda i, pos: (i // tpm, 0, 5)),
                      pl.BlockSpec((1, d), lambda i, pos: (0, 0)),
                      pl.BlockSpec(memory_space=pl.ANY)],
            out_specs=pl.BlockSpec((tm, d), lambda i, pos: (i, 0)),
            scratch_shapes=[pltpu.VMEM((2, 2, tm, d), f32), pltpu.SemaphoreType.DMA((2,))],
        ),
        out_shape=jax.ShapeDtypeStruct((n_rows, d), f32),
        compiler_params=_cparams("arbitrary"),
        name="moe_combine",
    )(pos, x_all, w_tok, mods, final_g, y)


def _moe(x_all, n_rows, norm_g, mods, tiles_per_mod, rw_t, rb_col, layer, wg, wu, wd, final_g, final_norm):
    info, counts = _router(x_all, n_rows, norm_g, mods, rw_t, rb_col, tiles_per_mod)
    tm = MOE_TILE
    p_rows = 2 * n_rows + (N_EXPERTS + 1) * tm
    cnt = counts[:, 0].astype(jnp.int32)
    padded = ((cnt + tm - 1) // tm) * tm
    ends = jnp.cumsum(padded)
    starts = ends - padded
    e1 = info[0].astype(jnp.int32)
    e2 = info[1].astype(jnp.int32)
    pos = jnp.stack([starts[e1] + info[4].astype(jnp.int32),
                     starts[e2] + info[5].astype(jnp.int32)])
    n_used = (ends[-1] // tm).astype(jnp.int32)
    tile_start = jnp.arange(p_rows // tm, dtype=jnp.int32) * tm
    tile_expert = jnp.sum((ends[None, :] <= tile_start[:, None]).astype(jnp.int32), axis=1)
    tile_expert = jnp.minimum(tile_expert, N_EXPERTS - 1)
    eid = jnp.arange(N_EXPERTS, dtype=jnp.int32)
    used = cnt > 0
    slot_e = (jnp.cumsum(used.astype(jnp.int32)) - 1) % 2
    later_used = used[None, :] & (eid[None, :] > eid[:, None])
    next_e = jnp.min(jnp.where(later_used, eid[None, :], N_EXPERTS), axis=1)
    next_e = jnp.where(next_e < N_EXPERTS, next_e, -1)
    first = ((tile_start == starts[tile_expert]) & (tile_start < ends[-1])).astype(jnp.int32)
    sched = jnp.stack([tile_expert, first, slot_e[tile_expert], next_e[tile_expert]])
    zpos = jnp.concatenate([starts + cnt, n_used.reshape(1)])
    xs = _dispatch(pos, zpos, x_all, n_rows, norm_g, mods, tiles_per_mod, p_rows)
    y = _expert_ffn(sched, n_used.reshape(1), xs, layer, wg, wu, wd)
    w_tok = jnp.transpose(info[2:4])
    return _combine(pos, x_all, n_rows, w_tok, mods, final_g, y, tiles_per_mod, final_norm)


def _rope_tables(seq_len, n_batch, n_ctx_rows):
    rows = seq_len // GRID_W
    row = jnp.repeat(jnp.arange(rows), GRID_W)
    col = jnp.tile(jnp.arange(GRID_W), rows)
    nf = HEAD_DIM // 4
    inv_freq = ROPE_BASE ** (-jnp.arange(nf, dtype=f32) / nf)
    ang_r = row[:, None].astype(f32) * inv_freq
    ang_c = col[:, None].astype(f32) * inv_freq
    cos64 = jnp.concatenate([jnp.cos(ang_r)] * 2 + [jnp.cos(ang_c)] * 2, axis=-1)
    sin64 = jnp.concatenate([-jnp.sin(ang_r), jnp.sin(ang_r), -jnp.sin(ang_c), jnp.sin(ang_c)], axis=-1)
    cos_t = jnp.tile(jnp.concatenate([cos64, cos64], axis=-1), (n_batch, 1))
    sin_t = jnp.tile(jnp.concatenate([sin64, sin64], axis=-1), (n_batch, 1))
    cos_t = jnp.concatenate([cos_t, jnp.ones((n_ctx_rows, 128), f32)], axis=0)
    sin_t = jnp.concatenate([sin_t, jnp.zeros((n_ctx_rows, 128), f32)], axis=0)
    return cos_t, sin_t


def kernel(x, c, ctx, c_ctx, w_mod, b_mod, norm1_g, norm2_g, final_norm_g, w_in, a_sink,
           p_w, p_scale, c_ln_g, c_ln_b, c_ws, c_bs, g_w2, g_b, g_norm_g,
           w_branch, w_gate, w_out, router_w, router_b, e_gate, e_up, e_down):
    n_batch, seq_len, d = x.shape
    n_ctx = ctx.shape[1]
    depth = w_mod.shape[0]
    n_lat = n_batch * seq_len
    n_all = n_lat + n_batch * n_ctx
    tiles_per_mod = seq_len // ROW_TILE

    cond = jnp.zeros((8, d), f32).at[0:n_batch].set(c).at[n_batch].set(c_ctx)
    mods_all = _modulation(cond, w_mod, b_mod)
    cos_t, sin_t = _rope_tables(seq_len, n_batch, n_batch * n_ctx)
    rw_f = jnp.transpose(router_w)
    rw_hi = rw_f.astype(bf16)
    rw_t = jnp.concatenate([rw_hi, (rw_f - rw_hi.astype(f32)).astype(bf16)], axis=0)
    rb_col = router_b.reshape(N_EXPERTS, 1)
    final_g = final_norm_g.reshape(1, d)

    w_in_b = jnp.pad(w_in, ((0, 0), (0, 0), (0, 96))).astype(bf16)
    w_gate_b = w_gate.astype(bf16)
    w_branch_b = w_branch.astype(bf16)
    w_out_b = w_out.astype(bf16)

    n_lat_tiles = n_lat // ROW_TILE
    xa, xb, b_tile0 = x.reshape(n_lat, d), ctx.reshape(n_batch * n_ctx, d), 0
    for l in range(depth):
        last = l == depth - 1
        n_out = n_lat if last else n_all
        mods = mods_all[l].reshape(8, 1, 6 * d)
        w2p = jnp.zeros((128, 512), f32)
        w2p = w2p.at[0:GLA_RANK, 0:256].set(g_w2[l, 0]).at[GLA_RANK:2 * GLA_RANK, 256:512].set(g_w2[l, 1])
        w2_hi = w2p.astype(bf16)
        w2_split = jnp.concatenate([w2_hi, w2_hi, (w2p - w2_hi.astype(f32)).astype(bf16)], axis=0)
        bs_b = jnp.broadcast_to(c_bs[l][:, :, None], (4, SG_CHUNK, 128))
        h, qkv, z_pool, s_br, z_gla, gates = _in_proj(
            xa, xb, n_lat_tiles, b_tile0, n_all, l, norm1_g[l].reshape(1, d), mods, cos_t, sin_t,
            w_in_b, w2_split, g_b[l].reshape(1, 512), c_ln_g[l].reshape(1, 512),
            c_ln_b[l].reshape(1, 512), c_ws[l].astype(bf16), bs_b, tiles_per_mod)
        a_br = _window_attention(qkv, a_sink[l], n_batch, seq_len, n_ctx)
        pw = p_w[l].astype(bf16)
        ps = p_scale[l].reshape(1, 512)
        p_br = _multi_scale_pool(z_pool, 0, n_lat, seq_len, ROW_TILE, pw, ps)
        o_f, o_b = _gla_scan(z_gla, gates, n_batch, seq_len, n_ctx)
        if not last:
            a_c = _context_attention(qkv, a_sink[l], n_batch, n_lat, n_ctx)
            p_c = _multi_scale_pool(z_pool, n_lat, n_batch * n_ctx, n_ctx, n_ctx, pw, ps)
            a_br = jnp.concatenate([a_br, a_c], axis=0)
            p_br = jnp.concatenate([p_br, p_c], axis=0)
        acc = _gated_merge(h, (a_br, p_br, s_br), o_f, o_b, z_gla, g_norm_g[l].reshape(1, 512),
                           n_out, l, w_gate_b, w_branch_b)
        x_all = _out_proj_residual(acc, l, w_out_b, xa, xb, n_lat_tiles, b_tile0, mods, n_out,
                                   tiles_per_mod)
        x_all = _moe(x_all, n_out, norm2_g[l].reshape(1, d), mods, tiles_per_mod, rw_t, rb_col,
                     l, e_gate, e_up, e_down, final_g, last)
        xa, xb, b_tile0 = x_all, x_all, n_lat_tiles
    return x_all.reshape(n_batch, seq_len, d)
```

```python
import functools

import numpy as np
import jax
import jax.numpy as jnp
from jax import lax
from jax.experimental import pallas as pl
from jax.experimental.pallas import tpu as pltpu

f32 = jnp.float32
bf16 = jnp.bfloat16

EPS = 1e-6
GRID_W = 64
ROPE_BASE = 10000.0

HEAD_DIM = 64
N_Q_HEADS = 8
N_KV_HEADS = 2
Q_PER_KV = N_Q_HEADS // N_KV_HEADS
ATT_BLOCK = 128
POOL_WINDOWS = (2, 4, 8, 16)
POOL_HALO = 8
SG_CHUNK = 128
GLA_HEADS = 4
GLA_DK = 64
GLA_DV = 128
GLA_RANK = 16
GLA_TAU = 16.0
GLA_CHUNK = 64
N_EXPERTS = 16
EXPERTS_PER_GROUP = 4
N_GROUPS = 4

ROW_TILE = 512
GLA_BLOCK = 256
MOE_TILE = 256
V7X_VMEM_BYTES = 64 * 1024 * 1024
VMEM_LIMIT = V7X_VMEM_BYTES * 7 // 8
EXPERT_VMEM_LIMIT = V7X_VMEM_BYTES * 31 // 32


def _cparams(*sem, vmem_limit=VMEM_LIMIT):
    return pltpu.CompilerParams(dimension_semantics=sem, vmem_limit_bytes=vmem_limit)


def _dot(a, b, dims=(((1,), (0,)), ((), ())), precision=None):
    return lax.dot_general(a, b, dims, precision=precision, preferred_element_type=f32)


_NT = (((1,), (1,)), ((), ()))
_TN = (((0,), (0,)), ((), ()))


def _mod_kernel(c_ref, w_ref, b_ref, o_ref):
    c = c_ref[...]
    a = c * jax.nn.sigmoid(c)
    w = w_ref[...]
    a_hi = a.astype(bf16)
    a_lo = (a - a_hi.astype(f32)).astype(bf16)
    w_hi = w.astype(bf16)
    w_lo = (w - w_hi.astype(f32)).astype(bf16)
    n = a.shape[0]
    both = _dot(jnp.concatenate([a_hi, a_lo], axis=0), w_hi)
    o_ref[...] = both[0:n] + both[n:2 * n] + _dot(a_hi, w_lo) + b_ref[...]


def _modulation(cond, w_mod, b_mod):
    n_layers, d, d6 = w_mod.shape
    tn = 1024
    return pl.pallas_call(
        _mod_kernel,
        grid=(n_layers, d6 // tn),
        in_specs=[
            pl.BlockSpec((8, d), lambda l, j: (0, 0)),
            pl.BlockSpec((None, d, tn), lambda l, j: (l, 0, j)),
            pl.BlockSpec((None, 1, tn), lambda l, j: (l, 0, j)),
        ],
        out_specs=pl.BlockSpec((None, 8, tn), lambda l, j: (l, 0, j)),
        out_shape=jax.ShapeDtypeStruct((n_layers, 8, d6), f32),
        compiler_params=_cparams("arbitrary", "arbitrary"),
        name="modulation",
    )(cond, w_mod, b_mod.reshape(n_layers, 1, d6))


def _norm_mod(x, g, shift, scale):
    y = x * lax.rsqrt(jnp.mean(x * x, axis=-1, keepdims=True) + EPS)
    return (y * g) * (1.0 + scale) + shift


def _two_source_rows(xa_ref, xb_ref, n_a_tiles):
    return jnp.where(pl.program_id(0) < n_a_tiles, xa_ref[...], xb_ref[...])


def _two_source_specs(tm, d, n_a_tiles, b_tile0):
    return [pl.BlockSpec((tm, d), lambda i: (jnp.minimum(i, n_a_tiles - 1), 0)),
            pl.BlockSpec((tm, d), lambda i: (b_tile0 + jnp.maximum(i - n_a_tiles, 0), 0),
                         pipeline_mode=pl.Buffered(1))]


def _spatial_gate_rows(uv, lg_ref, lb_ref, ws_ref, bs_ref, o_ref):
    a = jax.nn.gelu(uv)
    u = a[:, 0:512]
    v = a[:, 512:1024]
    mu = jnp.mean(v, axis=-1, keepdims=True)
    var = jnp.mean(jnp.square(v - mu), axis=-1, keepdims=True)
    vn = ((v - mu) * lax.rsqrt(var + EPS) * lg_ref[...] + lb_ref[...]).astype(bf16)
    for c in range(uv.shape[0] // SG_CHUNK):
        rows = slice(c * SG_CHUNK, (c + 1) * SG_CHUNK)
        for g in range(4):
            cols = slice(g * 128, (g + 1) * 128)
            mixed = _dot(ws_ref[g], vn[rows, cols]) + bs_ref[g]
            o_ref[rows, cols] = (u[rows, cols] * mixed).astype(o_ref.dtype)


def _in_proj_kernel(xa_ref, xb_ref, g_ref, sh_ref, sc_ref, cos_ref, sin_ref, w_ref,
                    w2_ref, gb_ref, lg_ref, lb_ref, ws_ref, bs_ref,
                    h_ref, qkv_ref, pool_ref, sg_ref, gla_ref, gate_ref, *, n_a_tiles):
    x = _two_source_rows(xa_ref, xb_ref, n_a_tiles)
    hb = _norm_mod(x, g_ref[...], sh_ref[...], sc_ref[...]).astype(bf16)
    h_ref[...] = hb

    cos = cos_ref[...]
    sin = sin_ref[...]
    lane = lax.broadcasted_iota(jnp.int32, cos.shape, 1)
    first_half = (lane % 32) < 16

    def rope(z):
        rot = jnp.where(first_half, pltpu.roll(z, 112, 1), pltpu.roll(z, 16, 1))
        return z * cos + rot * sin

    zq = _dot(hb, w_ref[:, 0:512])
    for c in range(4):
        qkv_ref[:, c * 128:(c + 1) * 128] = (
            rope(zq[:, c * 128:(c + 1) * 128]) * (HEAD_DIM ** -0.5)).astype(bf16)
    zkv = _dot(hb, w_ref[:, 512:768])
    qkv_ref[:, 512:640] = rope(zkv[:, 0:128]).astype(bf16)
    qkv_ref[:, 640:768] = zkv[:, 128:256].astype(bf16)
    pool_ref[...] = _dot(hb, w_ref[:, 768:1280])
    _spatial_gate_rows(_dot(hb, w_ref[:, 1280:2304]), lg_ref, lb_ref, ws_ref, bs_ref, sg_ref)
    gla_ref[...] = _dot(hb, w_ref[:, 2304:3840])
    low_rank = _dot(hb, w_ref[:, 3840:3968])
    lr_hi = low_rank.astype(bf16)
    lr_lo = (low_rank - lr_hi.astype(f32)).astype(bf16)
    logit = _dot(jnp.concatenate([lr_hi, lr_lo, lr_hi], axis=1), w2_ref[...]) + gb_ref[...]
    gate_ref[...] = _log_sigmoid(logit) / GLA_TAU


def _log_sigmoid(x):
    return jnp.minimum(x, 0.0) - jnp.log(1.0 + jnp.exp(-jnp.abs(x)))


def _in_proj(xa, xb, n_a_tiles, b_tile0, rows, layer, g, mods, cos_t, sin_t, w_pad, w2p, gbias,
             ln_g, ln_b, ws, bs_b, tiles_per_mod):
    d = xa.shape[1]
    n_w = w_pad.shape[2]
    tm = ROW_TILE
    mod_spec = lambda k: pl.BlockSpec((None, 1, d), lambda i: (i // tiles_per_mod, 0, k))
    row_spec = lambda w: pl.BlockSpec((tm, w), lambda i: (i, 0))
    return pl.pallas_call(
        functools.partial(_in_proj_kernel, n_a_tiles=n_a_tiles),
        grid=(rows // tm,),
        in_specs=_two_source_specs(tm, d, n_a_tiles, b_tile0) + [
            pl.BlockSpec((1, d), lambda i: (0, 0)),
            mod_spec(0), mod_spec(1),
            row_spec(128), row_spec(128),
            pl.BlockSpec((None, d, n_w), lambda i: (layer, 0, 0), pipeline_mode=pl.Buffered(1)),
            pl.BlockSpec((384, 512), lambda i: (0, 0)),
            pl.BlockSpec((1, 512), lambda i: (0, 0)),
            pl.BlockSpec((1, 512), lambda i: (0, 0)),
            pl.BlockSpec((1, 512), lambda i: (0, 0)),
            pl.BlockSpec((4, 128, 128), lambda i: (0, 0, 0)),
            pl.BlockSpec((4, 128, 128), lambda i: (0, 0, 0)),
        ],
        out_specs=[row_spec(d), row_spec(768), row_spec(512), row_spec(512), row_spec(1536),
                   row_spec(512)],
        out_shape=[
            jax.ShapeDtypeStruct((rows, d), bf16),
            jax.ShapeDtypeStruct((rows, 768), bf16),
            jax.ShapeDtypeStruct((rows, 512), f32),
            jax.ShapeDtypeStruct((rows, 512), bf16),
            jax.ShapeDtypeStruct((rows, 1536), f32),
            jax.ShapeDtypeStruct((rows, 512), f32),
        ],
        compiler_params=_cparams("arbitrary"),
        name="in_proj",
    )(xa, xb, g, mods, mods, cos_t, sin_t, w_pad, w2p, gbias, ln_g, ln_b, ws, bs_b)


def _sink_column(sink_ref, kvh, rows_per_head):
    r = lax.broadcasted_iota(jnp.int32, (Q_PER_KV * rows_per_head, 1), 0) // rows_per_head
    col = jnp.full(r.shape, sink_ref[kvh * Q_PER_KV], f32)
    for g in range(1, Q_PER_KV):
        col = jnp.where(r == g, sink_ref[kvh * Q_PER_KV + g], col)
    return col


def _win_attn_kernel(sink_ref, main_ref, prev_ref, next_ref, ctx_ref, o_ref, *, tiles_per_seq, seq_len):
    i = pl.program_id(0)
    tq = main_ref.shape[0]
    n_sub = tq // ATT_BLOCK
    blk0 = (i % tiles_per_seq) * n_sub
    blocks_per_seq = seq_len // ATT_BLOCK
    r = lax.broadcasted_iota(jnp.int32, (Q_PER_KV * ATT_BLOCK, ATT_BLOCK), 0) % ATT_BLOCK
    j = lax.broadcasted_iota(jnp.int32, (Q_PER_KV * ATT_BLOCK, ATT_BLOCK), 1)
    keep_prev = j >= r
    keep_next = j <= r
    n_ctx = ctx_ref.shape[0]

    def with_ones(v):
        return jnp.concatenate([v, jnp.ones_like(v)], axis=1)

    for kvh in range(N_KV_HEADS):
        kc = 512 + kvh * HEAD_DIM
        vc = 640 + kvh * HEAD_DIM
        k_all = jnp.concatenate([prev_ref[:, kc:kc + HEAD_DIM], main_ref[:, kc:kc + HEAD_DIM],
                                 next_ref[:, kc:kc + HEAD_DIM]], axis=0)
        v_all = with_ones(jnp.concatenate([prev_ref[:, vc:vc + HEAD_DIM], main_ref[:, vc:vc + HEAD_DIM],
                                           next_ref[:, vc:vc + HEAD_DIM]], axis=0))
        k_ctx = ctx_ref[:, kc:kc + HEAD_DIM]
        v_ctx = with_ones(ctx_ref[:, vc:vc + HEAD_DIM])
        sink = _sink_column(sink_ref, kvh, ATT_BLOCK)
        for sb in range(n_sub):
            rows = slice(sb * ATT_BLOCK, (sb + 1) * ATT_BLOCK)
            q = jnp.concatenate(
                [main_ref[rows, (kvh * Q_PER_KV + g) * HEAD_DIM:(kvh * Q_PER_KV + g + 1) * HEAD_DIM]
                 for g in range(Q_PER_KV)], axis=0)
            band = slice(sb * ATT_BLOCK, (sb + 3) * ATT_BLOCK)
            s_band = _dot(q, k_all[band], _NT)
            s_ctx = _dot(q, k_ctx, _NT)
            blk = blk0 + sb
            parts = [jnp.where(keep_prev & (blk >= 1), s_band[:, 0:ATT_BLOCK], -1e30),
                     s_band[:, ATT_BLOCK:2 * ATT_BLOCK],
                     jnp.where(keep_next & (blk <= blocks_per_seq - 2), s_band[:, 2 * ATT_BLOCK:], -1e30)]
            parts += [s_ctx[:, c:c + 128] for c in range(0, n_ctx, 128)]
            m = jnp.maximum(sink, jnp.max(functools.reduce(jnp.maximum, parts), axis=-1, keepdims=True))
            p = jnp.concatenate([jnp.exp(x - m).astype(bf16) for x in parts], axis=1)
            o_sum = (_dot(p[:, 0:3 * ATT_BLOCK], v_all[band]) + _dot(p[:, 3 * ATT_BLOCK:], v_ctx))
            denom = jnp.exp(sink - m) + o_sum[:, HEAD_DIM:HEAD_DIM + 1]
            o = o_sum[:, 0:HEAD_DIM] / denom
            for g in range(Q_PER_KV):
                c0 = (kvh * Q_PER_KV + g) * HEAD_DIM
                o_ref[rows, c0:c0 + HEAD_DIM] = o[g * ATT_BLOCK:(g + 1) * ATT_BLOCK].astype(o_ref.dtype)


def _window_attention(qkv, sink, n_batch, seq_len, n_ctx):
    tq = ROW_TILE
    n_lat = n_batch * seq_len
    tiles_per_seq = seq_len // tq
    sub = tq // ATT_BLOCK
    n_blocks = n_lat // ATT_BLOCK
    ctx_blk0 = n_lat // n_ctx
    w = qkv.shape[1]
    kern = functools.partial(_win_attn_kernel, tiles_per_seq=tiles_per_seq, seq_len=seq_len)
    return pl.pallas_call(
        kern,
        grid=(n_lat // tq,),
        in_specs=[
            pl.BlockSpec(memory_space=pltpu.SMEM),
            pl.BlockSpec((tq, w), lambda i: (i, 0)),
            pl.BlockSpec((ATT_BLOCK, w), lambda i: (jnp.maximum(i * sub - 1, 0), 0)),
            pl.BlockSpec((ATT_BLOCK, w), lambda i: (jnp.minimum(i * sub + sub, n_blocks - 1), 0)),
            pl.BlockSpec((n_ctx, w), lambda i: (ctx_blk0 + i // tiles_per_seq, 0)),
        ],
        out_specs=pl.BlockSpec((tq, 512), lambda i: (i, 0)),
        out_shape=jax.ShapeDtypeStruct((n_lat, 512), bf16),
        compiler_params=_cparams("arbitrary"),
        name="window_attention",
    )(sink, qkv, qkv, qkv, qkv)


def _ctx_attn_kernel(sink_ref, qkv_ref, o_ref):
    n = qkv_ref.shape[0]
    for kvh in range(N_KV_HEADS):
        kc = 512 + kvh * HEAD_DIM
        vc = 640 + kvh * HEAD_DIM
        q = jnp.concatenate(
            [qkv_ref[:, (kvh * Q_PER_KV + g) * HEAD_DIM:(kvh * Q_PER_KV + g + 1) * HEAD_DIM]
             for g in range(Q_PER_KV)], axis=0)
        s = _dot(q, qkv_ref[:, kc:kc + HEAD_DIM], _NT)
        sink = _sink_column(sink_ref, kvh, n)
        m = jnp.maximum(sink, jnp.max(s, axis=-1, keepdims=True))
        p = jnp.exp(s - m)
        denom = jnp.exp(sink - m) + jnp.sum(p, axis=-1, keepdims=True)
        o = _dot(p.astype(bf16), qkv_ref[:, vc:vc + HEAD_DIM]) / denom
        for g in range(Q_PER_KV):
            c0 = (kvh * Q_PER_KV + g) * HEAD_DIM
            o_ref[:, c0:c0 + HEAD_DIM] = o[g * n:(g + 1) * n].astype(o_ref.dtype)


def _context_attention(qkv, sink, n_batch, n_lat, n_ctx):
    w = qkv.shape[1]
    blk0 = n_lat // n_ctx
    return pl.pallas_call(
        _ctx_attn_kernel,
        grid=(n_batch,),
        in_specs=[pl.BlockSpec(memory_space=pltpu.SMEM),
                  pl.BlockSpec((n_ctx, w), lambda b: (blk0 + b, 0))],
        out_specs=pl.BlockSpec((n_ctx, 512), lambda b: (b, 0)),
        out_shape=jax.ShapeDtypeStruct((n_batch * n_ctx, 512), bf16),
        compiler_params=_cparams("arbitrary"),
        name="context_attention",
    )(sink, qkv)


def _pool_kernel(main_ref, prev_ref, next_ref, pw_ref, ps_ref, o_ref, xe_ref, *, tiles_per_seq, seq_len):
    i = pl.program_id(0)
    tp = main_ref.shape[0]
    t_in_seq = i % tiles_per_seq
    h = POOL_HALO
    xe_ref[0:h, :] = jnp.where(t_in_seq == 0, 0.0, prev_ref[...])
    xe_ref[h:h + tp, :] = main_ref[...]
    xe_ref[h + tp:2 * h + tp, :] = jnp.where(t_in_seq == tiles_per_seq - 1, 0.0, next_ref[...])
    pos = t_in_seq * tp + lax.broadcasted_iota(jnp.int32, (tp, 1), 0)
    for gi, w in enumerate(POOL_WINDOWS):
        cols = slice(gi * 128, (gi + 1) * 128)
        acc = xe_ref[h - w // 2:h - w // 2 + tp, cols]
        for u in range(-w // 2 + 1, w // 2):
            acc = acc + xe_ref[h + u:h + u + tp, cols]
        lo = jnp.maximum(pos - w // 2, 0)
        hi = jnp.minimum(pos + w // 2, seq_len)
        cnt = (hi - lo).astype(f32)
        pooled = acc / cnt - main_ref[:, cols]
        y = _dot(pooled.astype(bf16), pw_ref[gi])
        o_ref[:, cols] = (y * ps_ref[:, cols]).astype(o_ref.dtype)


def _multi_scale_pool(z, row0, n_rows, seq_len, tp, p_w, p_scale):
    tiles_per_seq = seq_len // tp
    t0 = row0 // tp
    h0 = row0 // POOL_HALO
    hb = tp // POOL_HALO
    n_halo = n_rows // POOL_HALO
    kern = functools.partial(_pool_kernel, tiles_per_seq=tiles_per_seq, seq_len=seq_len)
    return pl.pallas_call(
        kern,
        grid=(n_rows // tp,),
        in_specs=[
            pl.BlockSpec((tp, 512), lambda i: (t0 + i, 0)),
            pl.BlockSpec((POOL_HALO, 512), lambda i: (h0 + jnp.maximum(i * hb - 1, 0), 0)),
            pl.BlockSpec((POOL_HALO, 512), lambda i: (h0 + jnp.minimum((i + 1) * hb, n_halo - 1), 0)),
            pl.BlockSpec((4, 128, 128), lambda i: (0, 0, 0)),
            pl.BlockSpec((1, 512), lambda i: (0, 0)),
        ],
        out_specs=pl.BlockSpec((tp, 512), lambda i: (i, 0)),
        out_shape=jax.ShapeDtypeStruct((n_rows, 512), bf16),
        scratch_shapes=[pltpu.VMEM((tp + 2 * POOL_HALO, 512), f32)],
        compiler_params=_cparams("arbitrary"),
        name="multi_scale_pool",
    )(z, z, z, p_w, p_scale)


GLA_SEGMENTS = (1, 2, 4, 8, 16, 32, 64)


def _gla_tables():
    c = GLA_CHUNK
    i = np.arange(c)[:, None]
    j = np.arange(c)[None, :]
    tri = (j <= i).astype(np.float32)
    mask = np.zeros((2, 7, c, c), np.float32)
    mask[:, 0] = np.eye(c)
    for lv, s in enumerate(GLA_SEGMENTS[:-1]):
        m = ((i // (2 * s)) == (j // (2 * s))) & ((i // s) % 2 == 1) & ((j // s) % 2 == 0)
        mask[0, 1 + lv] = m
        mask[1, 1 + lv] = m.T
    mask = np.tile(mask, (1, 1, 1, GLA_HEADS))
    return tri, mask


def _segment_sums(g, cum):
    c = GLA_CHUNK
    row = lax.broadcasted_iota(jnp.int32, (c, 1), 0)
    zero = jnp.zeros_like(g)
    before = [None] + [pltpu.roll(g, kk, 0) for kk in (1, 2, 3)]
    after = [None] + [pltpu.roll(g, c - kk, 0) for kk in (1, 2, 3)]
    a, r = {1: g}, {1: zero}
    for s in (2, 4):
        pos = row % s
        a_s, r_s = g, zero
        for kk in range(1, s):
            a_s = a_s + jnp.where(pos >= kk, before[kk], 0.0)
            r_s = r_s + jnp.where(pos < s - kk, after[kk], 0.0)
        a[s], r[s] = a_s, r_s
    blocks = [cum[8 * b:8 * b + 8] for b in range(c // 8)]
    last = [cum[8 * b + 7:8 * b + 8] for b in range(c // 8)]
    for s in (8, 16, 32, 64):
        nb = s // 8
        a_blk, r_blk = [], []
        for b in range(c // 8):
            prev_end = (b // nb) * nb - 1
            a_blk.append(blocks[b] - last[prev_end] if prev_end >= 0 else blocks[b])
            r_blk.append(last[(b // nb + 1) * nb - 1] - blocks[b])
        a[s] = jnp.concatenate(a_blk, axis=0)
        r[s] = jnp.concatenate(r_blk, axis=0)
    return a, r


def _gla_chunk(d, q, k, v, g, tri_ref, mask_ref, st_ref):
    c = GLA_CHUNK
    w = g.shape[1]
    g_hi = g.astype(bf16)
    r1 = g - g_hi.astype(f32)
    g_mid = r1.astype(bf16)
    g_lo = (r1 - g_mid.astype(f32)).astype(bf16)
    cum3 = _dot(tri_ref[...], jnp.concatenate([g_hi, g_mid, g_lo], axis=1))
    cum = cum3[:, 0:w] + cum3[:, w:2 * w] + cum3[:, 2 * w:3 * w]
    a, r = _segment_sums(g, cum)
    if d == 0:
        cq = [a[s] for s in GLA_SEGMENTS]
        ck = [r[s] for s in GLA_SEGMENTS]
    else:
        cq = [r[s] + g for s in GLA_SEGMENTS]
        ck = [a[s] - g for s in GLA_SEGMENTS]
    total = cum[c - 1:c]

    q = q * (GLA_DK ** -0.5)
    head_of_lane = lax.broadcasted_iota(jnp.int32, (1, w), 1) // GLA_DK
    vhead_of_lane = lax.broadcasted_iota(jnp.int32, (1, v.shape[1]), 1) // GLA_DV

    def stack_heads(x, lane_head):
        return jnp.concatenate([jnp.where(lane_head == h, x, jnp.zeros_like(x))
                                for h in range(GLA_HEADS)], axis=0)

    kb = k.astype(bf16)
    k_plain = stack_heads(kb, head_of_lane)
    q01 = jnp.concatenate([q.astype(bf16), (q * jnp.exp(cq[0])).astype(bf16)], axis=0)
    s01 = _dot(q01, k_plain, _NT)
    att = mask_ref[d, 0] * s01[0:c] + mask_ref[d, 1] * s01[c:2 * c]
    for lv in range(1, 6):
        qs = (q * jnp.exp(cq[lv])).astype(bf16)
        ks = stack_heads((k * jnp.exp(ck[lv])).astype(bf16), head_of_lane)
        att = att + mask_ref[d, 1 + lv] * _dot(qs, ks, _NT)
    v_heads = stack_heads(v.astype(bf16), vhead_of_lane)
    st = st_ref[d]
    q_state = (q * jnp.exp(cq[6])).astype(bf16)
    o = _dot(att.astype(bf16), v_heads) + _dot(q_state, st.astype(bf16), _NT)
    k_state = stack_heads((k * jnp.exp(ck[6])).astype(bf16), head_of_lane)
    st_ref[d] = st * jnp.exp(total) + _dot(v_heads, k_state, _TN)
    return o


def _gla_kernel(qf_ref, kf_ref, vf_ref, gf_ref, qb_ref, kb_ref, vb_ref, gb_ref,
                tri_ref, mask_ref, of_ref, ob_ref, st_ref):
    @pl.when(pl.program_id(1) == 0)
    def _():
        st_ref[...] = jnp.zeros_like(st_ref)

    n_chunks = qf_ref.shape[0] // GLA_CHUNK
    for ci in range(n_chunks):
        rf = slice(ci * GLA_CHUNK, (ci + 1) * GLA_CHUNK)
        of_ref[rf, :] = _gla_chunk(0, qf_ref[rf, :], kf_ref[rf, :], vf_ref[rf, :], gf_ref[rf, :],
                                   tri_ref, mask_ref, st_ref)
        cb = n_chunks - 1 - ci
        rb = slice(cb * GLA_CHUNK, (cb + 1) * GLA_CHUNK)
        ob_ref[rb, :] = _gla_chunk(1, qb_ref[rb, :], kb_ref[rb, :], vb_ref[rb, :], gb_ref[rb, :],
                                   tri_ref, mask_ref, st_ref)


def _gla_scan(z_gla, gates, n_batch, seq_len, n_ctx):
    rows = z_gla.shape[0]
    tb = GLA_BLOCK
    assert n_ctx == tb
    lat_blocks = seq_len // tb
    ctx_blk0 = n_batch * lat_blocks
    tri, mask = _gla_tables()

    def fwd_row(b, s):
        return jnp.where(s == 0, ctx_blk0 + b, b * lat_blocks + s - 1)

    def bwd_row(b, s):
        return jnp.where(s == 0, ctx_blk0 + b, b * lat_blocks + lat_blocks - s)

    def specs(row, direction):
        return [
            pl.BlockSpec((tb, 256), lambda b, s: (row(b, s), 0)),
            pl.BlockSpec((tb, 256), lambda b, s: (row(b, s), 1)),
            pl.BlockSpec((tb, 512), lambda b, s: (row(b, s), 1)),
            pl.BlockSpec((tb, 256), lambda b, s: (row(b, s), direction)),
        ]

    const = lambda shape: pl.BlockSpec(shape, lambda b, s: (0,) * len(shape))
    return pl.pallas_call(
        _gla_kernel,
        grid=(n_batch, 1 + lat_blocks),
        in_specs=specs(fwd_row, 0) + specs(bwd_row, 1) + [
            const(tri.shape), const(mask.shape)],
        out_specs=[pl.BlockSpec((tb, 512), lambda b, s: (fwd_row(b, s), 0)),
                   pl.BlockSpec((tb, 512), lambda b, s: (bwd_row(b, s), 0))],
        out_shape=[jax.ShapeDtypeStruct((rows, 512), f32)] * 2,
        scratch_shapes=[pltpu.VMEM((2, GLA_HEADS * GLA_DV, GLA_HEADS * GLA_DK), f32)],
        compiler_params=_cparams("arbitrary", "arbitrary"),
        name="gla_scan",
    )(z_gla, z_gla, z_gla, gates, z_gla, z_gla, z_gla, gates,
      jnp.asarray(tri, bf16), jnp.asarray(mask, f32))


def _merge_kernel(h_ref, a_ref, ac_ref, p_ref, pc_ref, s_ref, of_ref, ob_ref, r_ref, ng_ref, wg_ref, wb_ref,
                  o_ref, g_ref, *, n_lat_tiles):
    @pl.when(pl.program_id(1) == 0)
    def _():
        o = of_ref[...] + ob_ref[...]
        r = r_ref[...]
        gate = r * jax.nn.sigmoid(r)
        for h in range(GLA_HEADS):
            cols = slice(h * GLA_DV, (h + 1) * GLA_DV)
            oh = o[:, cols]
            oh = oh * lax.rsqrt(jnp.mean(oh * oh, axis=-1, keepdims=True) + EPS)
            g_ref[:, cols] = (oh * ng_ref[:, cols] * gate[:, cols]).astype(g_ref.dtype)

    hb = h_ref[...]
    is_lat = pl.program_id(0) < n_lat_tiles
    branches = (jnp.where(is_lat, a_ref[...], ac_ref[...]), jnp.where(is_lat, p_ref[...], pc_ref[...]),
                s_ref[...], g_ref[...])
    acc = None
    for br, b in enumerate(branches):
        gate = jax.nn.sigmoid(_dot(hb, wg_ref[br]))
        term = gate * _dot(b, wb_ref[br])
        acc = term if acc is None else acc + term
    o_ref[...] = acc.astype(o_ref.dtype)


def _gated_merge(h, a_lat, a_ctx, p_lat, p_ctx, s_br, o_f, o_b, z_gla, norm_g, n_rows, n_lat_tiles, layer,
                 wg, wb):
    d = h.shape[1]
    tm, tn = ROW_TILE, 512
    bspec = pl.BlockSpec((tm, 512), lambda i, j: (i, 0))
    lat_spec = pl.BlockSpec((tm, 512), lambda i, j: (jnp.minimum(i, n_lat_tiles - 1), 0))
    ctx_spec = pl.BlockSpec((tm, 512), lambda i, j: (jnp.maximum(i - n_lat_tiles, 0), 0),
                            pipeline_mode=pl.Buffered(1))
    return pl.pallas_call(
        functools.partial(_merge_kernel, n_lat_tiles=n_lat_tiles),
        grid=(n_rows // tm, d // tn),
        in_specs=[pl.BlockSpec((tm, d), lambda i, j: (i, 0)), lat_spec, ctx_spec, lat_spec, ctx_spec,
                  bspec, bspec, bspec,
                  pl.BlockSpec((tm, 512), lambda i, j: (i, 2)),
                  pl.BlockSpec((1, 512), lambda i, j: (0, 0)),
                  pl.BlockSpec((None, 4, d, tn), lambda i, j: (layer, 0, 0, j)),
                  pl.BlockSpec((None, 4, 512, tn), lambda i, j: (layer, 0, 0, j))],
        out_specs=pl.BlockSpec((tm, tn), lambda i, j: (i, j)),
        out_shape=jax.ShapeDtypeStruct((n_rows, d), bf16),
        scratch_shapes=[pltpu.VMEM((tm, 512), bf16)],
        compiler_params=_cparams("arbitrary", "arbitrary"),
        name="gated_merge",
    )(h, a_lat, a_ctx, p_lat, p_ctx, s_br, o_f, o_b, z_gla, norm_g, wg, wb)


def _out_proj_kernel(acc_ref, wo_ref, xa_ref, xb_ref, gate_ref, o_ref, *, n_a_tiles):
    x = _two_source_rows(xa_ref, xb_ref, n_a_tiles)
    o_ref[...] = x + gate_ref[...] * _dot(acc_ref[...], wo_ref[...])


def _out_proj_residual(acc, layer, wo, xa, xb, n_a_tiles, b_tile0, mods, n_rows, tiles_per_mod):
    d = acc.shape[1]
    tm = ROW_TILE
    return pl.pallas_call(
        functools.partial(_out_proj_kernel, n_a_tiles=n_a_tiles),
        grid=(n_rows // tm,),
        in_specs=[pl.BlockSpec((tm, d), lambda i: (i, 0)),
                  pl.BlockSpec((None, d, d), lambda i: (layer, 0, 0))]
        + _two_source_specs(tm, d, n_a_tiles, b_tile0)
        + [pl.BlockSpec((None, 1, d), lambda i: (i // tiles_per_mod, 0, 2))],
        out_specs=pl.BlockSpec((tm, d), lambda i: (i, 0)),
        out_shape=jax.ShapeDtypeStruct((n_rows, d), f32),
        compiler_params=_cparams("arbitrary"),
        name="out_proj_residual",
    )(acc, wo, xa, xb, mods)


def _router_kernel(x_ref, g_ref, sh_ref, sc_ref, rw_ref, rb_ref, tri_ref, info_ref, cnt_ref, carry_ref):
    i = pl.program_id(0)

    @pl.when(i == 0)
    def _():
        carry_ref[...] = jnp.zeros_like(carry_ref)

    h2 = _norm_mod(x_ref[...], g_ref[...], sh_ref[...], sc_ref[...])
    tm = h2.shape[0]
    h_hi = h2.astype(bf16)
    h_lo = (h2 - h_hi.astype(f32)).astype(bf16)
    part = _dot(rw_ref[...], h_hi, _NT)
    logits = (part[0:N_EXPERTS] + part[N_EXPERTS:2 * N_EXPERTS]
              + _dot(rw_ref[0:N_EXPERTS, :], h_lo, _NT))
    score = jax.nn.sigmoid(logits)
    biased = score + rb_ref[...]
    b = [biased[e:e + 1] for e in range(N_EXPERTS)]
    sc = [score[e:e + 1] for e in range(N_EXPERTS)]

    def top2_sum(v):
        hi01, lo01 = jnp.maximum(v[0], v[1]), jnp.minimum(v[0], v[1])
        hi23, lo23 = jnp.maximum(v[2], v[3]), jnp.minimum(v[2], v[3])
        return jnp.maximum(hi01, hi23) + jnp.maximum(jnp.minimum(hi01, hi23), jnp.maximum(lo01, lo23))

    best = jnp.zeros((1, tm), jnp.int32)
    best_score = top2_sum(b[0:4])
    for gidx in range(1, N_GROUPS):
        gs = top2_sum(b[4 * gidx:4 * gidx + 4])
        take = gs > best_score
        best = jnp.where(take, gidx, best)
        best_score = jnp.where(take, gs, best_score)
    cb, cs = [], []
    for kk in range(EXPERTS_PER_GROUP):
        vb, vs = b[kk], sc[kk]
        for gidx in range(1, N_GROUPS):
            vb = jnp.where(best == gidx, b[4 * gidx + kk], vb)
            vs = jnp.where(best == gidx, sc[4 * gidx + kk], vs)
        cb.append(vb)
        cs.append(vs)
    i1 = jnp.zeros((1, tm), jnp.int32)
    m1, w1 = cb[0], cs[0]
    for kk in range(1, EXPERTS_PER_GROUP):
        take = cb[kk] > m1
        i1 = jnp.where(take, kk, i1)
        m1 = jnp.where(take, cb[kk], m1)
        w1 = jnp.where(take, cs[kk], w1)
    i2 = jnp.where(i1 == 0, 1, 0).astype(jnp.int32)
    m2 = jnp.where(i1 == 0, cb[1], cb[0])
    w2 = jnp.where(i1 == 0, cs[1], cs[0])
    for kk in range(1, EXPERTS_PER_GROUP):
        take = (cb[kk] > m2) & (i1 != kk) & (i2 != kk)
        i2 = jnp.where(take, kk, i2)
        m2 = jnp.where(take, cb[kk], m2)
        w2 = jnp.where(take, cs[kk], w2)
    e1 = best * EXPERTS_PER_GROUP + i1
    e2 = best * EXPERTS_PER_GROUP + i2
    wsum = w1 + w2
    erow = lax.broadcasted_iota(jnp.int32, (N_EXPERTS, tm), 0)
    oh1 = (erow == e1).astype(f32)
    oh2 = (erow == e2).astype(f32)
    oh = oh1 + oh2
    rank = _dot(oh.astype(bf16), tri_ref[...]) + carry_ref[:, 0:1]
    carry_ref[...] = carry_ref[...] + jnp.sum(oh, axis=1, keepdims=True)
    r1 = jnp.sum(oh1 * rank, axis=0, keepdims=True)
    r2 = jnp.sum(oh2 * rank, axis=0, keepdims=True)
    info_ref[...] = jnp.concatenate(
        [e1.astype(f32), e2.astype(f32), w1 / wsum, w2 / wsum, r1, r2,
         jnp.zeros((2, tm), f32)], axis=0)
    cnt_ref[...] = carry_ref[...]


def _router(x_all, n_rows, g, mods, rw_t, rb_col, tiles_per_mod):
    d = x_all.shape[1]
    tm = ROW_TILE
    tri = jnp.asarray(np.triu(np.ones((tm, tm), np.float32), k=1), bf16)
    mod_spec = lambda k: pl.BlockSpec((None, 1, d), lambda i: (i // tiles_per_mod, 0, k))
    return pl.pallas_call(
        _router_kernel,
        grid=(n_rows // tm,),
        in_specs=[pl.BlockSpec((tm, d), lambda i: (i, 0)),
                  pl.BlockSpec((1, d), lambda i: (0, 0)),
                  mod_spec(3), mod_spec(4),
                  pl.BlockSpec((2 * N_EXPERTS, d), lambda i: (0, 0)),
                  pl.BlockSpec((N_EXPERTS, 1), lambda i: (0, 0)),
                  pl.BlockSpec((tm, tm), lambda i: (0, 0))],
        out_specs=[pl.BlockSpec((8, tm), lambda i: (0, i)),
                   pl.BlockSpec((N_EXPERTS, 128), lambda i: (0, 0))],
        out_shape=[jax.ShapeDtypeStruct((8, n_rows), f32),
                   jax.ShapeDtypeStruct((N_EXPERTS, 128), f32)],
        scratch_shapes=[pltpu.VMEM((N_EXPERTS, 128), f32)],
        compiler_params=_cparams("arbitrary"),
        name="moe_router",
    )(x_all, g, mods, mods, rw_t, rb_col, tri)


def _dispatch_kernel(pos_ref, zpos_ref, x_ref, g_ref, sh_ref, sc_ref, xs_ref, buf_ref, zero_ref, sem, zsem):
    i = pl.program_id(0)
    tm = x_ref.shape[0]

    @pl.when(i == 0)
    def _():
        zero_ref[...] = jnp.zeros_like(zero_ref)

        def zero_tile(row):
            cp = pltpu.make_async_copy(zero_ref, xs_ref.at[pl.ds(pl.multiple_of(row, 8), tm)], zsem)
            cp.start()
            cp.wait()

        for e in range(N_EXPERTS):
            zero_tile((zpos_ref[0, e] // 8) * 8)

        def tail(t, c):
            zero_tile(t * tm)
            return c

        lax.fori_loop(zpos_ref[0, N_EXPERTS], xs_ref.shape[0] // tm, tail, 0)

    slot = i % 2

    def wait_slot(s):
        for k in range(2):
            pltpu.make_async_copy(buf_ref.at[s], xs_ref.at[pl.ds(0, tm)], sem.at[s]).wait()

    @pl.when(i >= 2)
    def _():
        wait_slot(slot)

    buf_ref[slot] = _norm_mod(x_ref[...], g_ref[...], sh_ref[...], sc_ref[...])
    src = buf_ref.at[slot]
    for r in range(tm):
        for k in range(2):
            dst = pos_ref[k, i * tm + r]
            pltpu.make_async_copy(src.at[pl.ds(r, 1)], xs_ref.at[pl.ds(dst, 1)], sem.at[slot]).start()

    @pl.when(i == pl.num_programs(0) - 1)
    def _():
        wait_slot(slot)
        wait_slot(1 - slot)


def _dispatch(pos, zpos, x_all, n_rows, g, mods, tiles_per_mod, p_rows):
    d = x_all.shape[1]
    tm = MOE_TILE
    tpm = tiles_per_mod * (ROW_TILE // tm)
    mod_spec = lambda k: pl.BlockSpec((None, 1, d), lambda i, pos, zpos: (i // tpm, 0, k))
    return pl.pallas_call(
        _dispatch_kernel,
        grid_spec=pltpu.PrefetchScalarGridSpec(
            num_scalar_prefetch=2,
            grid=(n_rows // tm,),
            in_specs=[pl.BlockSpec((tm, d), lambda i, pos, zpos: (i, 0)),
                      pl.BlockSpec((1, d), lambda i, pos, zpos: (0, 0)),
                      mod_spec(3), mod_spec(4)],
            out_specs=pl.BlockSpec(memory_space=pl.ANY),
            scratch_shapes=[pltpu.VMEM((2, tm, d), f32), pltpu.VMEM((tm, d), f32),
                            pltpu.SemaphoreType.DMA((2,)), pltpu.SemaphoreType.DMA(())],
        ),
        out_shape=jax.ShapeDtypeStruct((p_rows, d), f32),
        compiler_params=_cparams("arbitrary"),
        name="moe_dispatch",
    )(pos, zpos, x_all, g, mods, mods)


def _expert_kernel(sched_ref, nu_ref, xs_ref, wg_hbm, wu_hbm, wd_hbm, y_ref, wg_buf, wu_buf, wd_buf, sem,
                   *, layer):
    i = pl.program_id(0)
    slot = sched_ref[2, i]
    n_used = nu_ref[0, N_EXPERTS]

    def weight_copies(e, s):
        return [pltpu.make_async_copy(hbm.at[layer, e], buf.at[s], sem.at[s])
                for hbm, buf in ((wg_hbm, wg_buf), (wu_hbm, wu_buf), (wd_hbm, wd_buf))]

    @pl.when(i == 0)
    def _():
        for cp in weight_copies(sched_ref[0, 0], 0):
            cp.start()

    @pl.when((sched_ref[1, i] == 1) & (i < n_used))
    def _():
        for cp in weight_copies(sched_ref[0, i], slot):
            cp.wait()
        nxt = sched_ref[3, i]

        @pl.when(nxt >= 0)
        def _():
            for cp in weight_copies(nxt, 1 - slot):
                cp.start()

    @pl.when(i < n_used)
    def _():
        x = xs_ref[...]
        a = _dot(x, wg_buf[slot])
        act = (a * jax.nn.sigmoid(a)) * _dot(x, wu_buf[slot])
        y_ref[...] = _dot(act, wd_buf[slot])

    @pl.when(i >= n_used)
    def _():
        y_ref[...] = jnp.zeros_like(y_ref)


def _expert_ffn(sched, n_used, xs, layer, wg, wu, wd):
    p_rows, d = xs.shape
    de = wg.shape[3]
    tm = MOE_TILE
    return pl.pallas_call(
        functools.partial(_expert_kernel, layer=layer),
        grid_spec=pltpu.PrefetchScalarGridSpec(
            num_scalar_prefetch=2,
            grid=(p_rows // tm,),
            in_specs=[pl.BlockSpec((tm, d), lambda i, sc, nu: (jnp.where(i < nu[0, N_EXPERTS], i, 0), 0)),
                      pl.BlockSpec(memory_space=pl.ANY), pl.BlockSpec(memory_space=pl.ANY),
                      pl.BlockSpec(memory_space=pl.ANY)],
            out_specs=pl.BlockSpec((tm, d), lambda i, sc, nu: (i, 0)),
            scratch_shapes=[pltpu.VMEM((2, d, de), f32), pltpu.VMEM((2, d, de), f32),
                            pltpu.VMEM((2, de, d), f32), pltpu.SemaphoreType.DMA((2,))],
        ),
        out_shape=jax.ShapeDtypeStruct((p_rows, d), f32),
        compiler_params=_cparams("arbitrary", vmem_limit=EXPERT_VMEM_LIMIT),
        name="moe_experts",
    )(sched, n_used, xs, wg, wu, wd)


def _combine_kernel(pos_ref, x_ref, w_ref, gate_ref, fg_ref, y_ref, o_ref, ybuf_ref, sem, *, final_norm):
    i = pl.program_id(0)
    tm = x_ref.shape[0]
    slot = i % 2

    def gather(step, s):
        for r in range(tm):
            for k in range(2):
                src = pos_ref[k, step * tm + r]
                pltpu.make_async_copy(y_ref.at[pl.ds(src, 1)], ybuf_ref.at[s, k].at[pl.ds(r, 1)],
                                      sem.at[s]).start()

    @pl.when(i == 0)
    def _():
        gather(0, 0)

    @pl.when(i + 1 < pl.num_programs(0))
    def _():
        gather(i + 1, 1 - slot)

    for k in range(2):
        pltpu.make_async_copy(y_ref.at[pl.ds(0, tm)], ybuf_ref.at[slot, k], sem.at[slot]).wait()
    w = w_ref[...]
    moe = w[:, 0:1] * ybuf_ref[slot, 0] + w[:, 1:2] * ybuf_ref[slot, 1]
    x = x_ref[...] + gate_ref[...] * moe
    if final_norm:
        x = x * lax.rsqrt(jnp.mean(x * x, axis=-1, keepdims=True) + EPS) * fg_ref[...]
    o_ref[...] = x


def _combine(pos, x_all, n_rows, w_tok, mods, final_g, y, tiles_per_mod, final_norm):
    d = x_all.shape[1]
    tm = MOE_TILE
    tpm = tiles_per_mod * (ROW_TILE // tm)
    kern = functools.partial(_combine_kernel, final_norm=final_norm)
    return pl.pallas_call(
        kern,
        grid_spec=pltpu.PrefetchScalarGridSpec(
            num_scalar_prefetch=1,
            grid=(n_rows // tm,),
            in_specs=[pl.BlockSpec((tm, d), lambda i, pos: (i, 0)),
                      pl.BlockSpec((tm, 2), lambda i, pos: (i, 0)),
                      pl.BlockSpec((None, 1, d), lambda i, pos: (i // tpm, 0, 5)),
                      pl.BlockSpec((1, d), lambda i, pos: (0, 0)),
                      pl.BlockSpec(memory_space=pl.ANY)],
            out_specs=pl.BlockSpec((tm, d), lambda i, pos: (i, 0)),
            scratch_shapes=[pltpu.VMEM((2, 2, tm, d), f32), pltpu.SemaphoreType.DMA((2,))],
        ),
        out_shape=jax.ShapeDtypeStruct((n_rows, d), f32),
        compiler_params=_cparams("arbitrary"),
        name="moe_combine",
    )(pos, x_all, w_tok, mods, final_g, y)


def _plan_kernel(info_ref, cnt_ref, pos_ref, sched_ref, misc_ref, *, tm):
    cnt = cnt_ref[...]
    padded = jnp.floor((cnt + (tm - 1)) * (1.0 / tm)) * tm
    starts, ends = [], []
    run = jnp.zeros((1, 128), f32)
    for e in range(N_EXPERTS):
        starts.append(run)
        run = run + padded[e:e + 1]
        ends.append(run)
    total = run
    e1, e2 = info_ref[0:1, :], info_ref[1:2, :]
    p1, p2 = info_ref[4:5, :], info_ref[5:6, :]
    for e in range(N_EXPERTS):
        s_e = starts[e][:, 0:1]
        p1 = p1 + jnp.where(e1 == e, s_e, 0.0)
        p2 = p2 + jnp.where(e2 == e, s_e, 0.0)
    pos_ref[...] = jnp.concatenate([p1, p2], axis=0).astype(jnp.int32)
    slot_e, next_e = [], [None] * N_EXPERTS
    ordinal = jnp.full((1, 128), -1.0, f32)
    for e in range(N_EXPERTS):
        ordinal = ordinal + jnp.where(cnt[e:e + 1] > 0, 1.0, 0.0)
        slot_e.append(ordinal - 2.0 * jnp.floor(ordinal * 0.5))
    nxt = jnp.full((1, 128), -1.0, f32)
    for e in reversed(range(N_EXPERTS)):
        next_e[e] = nxt
        nxt = jnp.where(cnt[e:e + 1] > 0, float(e), nxt)
    n_t = sched_ref.shape[1]
    t0 = lax.broadcasted_iota(jnp.int32, (1, n_t), 1).astype(f32) * tm
    te = jnp.zeros((1, n_t), f32)
    for e in range(N_EXPERTS):
        te = te + jnp.where(ends[e][:, 0:1] <= t0, 1.0, 0.0)
    te = jnp.minimum(te, N_EXPERTS - 1.0)
    first = jnp.zeros((1, n_t), f32)
    slot = jnp.zeros((1, n_t), f32)
    nxt_t = jnp.zeros((1, n_t), f32)
    for e in range(N_EXPERTS):
        here = te == e
        first = first + jnp.where(here & (t0 == starts[e][:, 0:1]), 1.0, 0.0)
        slot = slot + jnp.where(here, slot_e[e][:, 0:1], 0.0)
        nxt_t = nxt_t + jnp.where(here, next_e[e][:, 0:1], 0.0)
    first = jnp.where(t0 < total[:, 0:1], first, 0.0)
    sched_ref[...] = jnp.concatenate([te, first, slot, nxt_t, jnp.zeros((4, n_t), f32)],
                                     axis=0).astype(jnp.int32)
    lane = lax.broadcasted_iota(jnp.int32, (1, 128), 1)
    zrow = jnp.where(lane == N_EXPERTS, total * (1.0 / tm), 0.0)
    for e in range(N_EXPERTS):
        zrow = zrow + jnp.where(lane == e, starts[e] + cnt[e:e + 1], 0.0)
    misc_ref[...] = jnp.concatenate([zrow, jnp.zeros((7, 128), f32)], axis=0).astype(jnp.int32)


def _plan(info, counts, n_rows, n_tiles):
    n_t = ((n_tiles + 127) // 128) * 128
    full = lambda shape: pl.BlockSpec(shape, lambda: (0,) * len(shape))
    return pl.pallas_call(
        functools.partial(_plan_kernel, tm=MOE_TILE),
        in_specs=[full((8, n_rows)), full((N_EXPERTS, 128))],
        out_specs=[full((2, n_rows)), full((8, n_t)), full((8, 128))],
        out_shape=[jax.ShapeDtypeStruct((2, n_rows), jnp.int32),
                   jax.ShapeDtypeStruct((8, n_t), jnp.int32),
                   jax.ShapeDtypeStruct((8, 128), jnp.int32)],
        name="moe_plan",
    )(info, counts)


def _moe(x_all, n_rows, norm_g, mods, tiles_per_mod, rw_t, rb_col, layer, wg, wu, wd, final_g, final_norm):
    info, counts = _router(x_all, n_rows, norm_g, mods, rw_t, rb_col, tiles_per_mod)
    tm = MOE_TILE
    p_rows = 2 * n_rows + (N_EXPERTS + 1) * tm
    pos, sched, misc = _plan(info, counts, n_rows, p_rows // tm)
    xs = _dispatch(pos, misc, x_all, n_rows, norm_g, mods, tiles_per_mod, p_rows)
    y = _expert_ffn(sched, misc, xs, layer, wg, wu, wd)
    w_tok = jnp.transpose(info[2:4])
    return _combine(pos, x_all, n_rows, w_tok, mods, final_g, y, tiles_per_mod, final_norm)


def _rope_tables(seq_len, n_batch, n_ctx_rows):
    rows = seq_len // GRID_W
    row = jnp.repeat(jnp.arange(rows), GRID_W)
    col = jnp.tile(jnp.arange(GRID_W), rows)
    nf = HEAD_DIM // 4
    inv_freq = ROPE_BASE ** (-jnp.arange(nf, dtype=f32) / nf)
    ang_r = row[:, None].astype(f32) * inv_freq
    ang_c = col[:, None].astype(f32) * inv_freq
    cos64 = jnp.concatenate([jnp.cos(ang_r)] * 2 + [jnp.cos(ang_c)] * 2, axis=-1)
    sin64 = jnp.concatenate([-jnp.sin(ang_r), jnp.sin(ang_r), -jnp.sin(ang_c), jnp.sin(ang_c)], axis=-1)
    cos_t = jnp.tile(jnp.concatenate([cos64, cos64], axis=-1), (n_batch, 1))
    sin_t = jnp.tile(jnp.concatenate([sin64, sin64], axis=-1), (n_batch, 1))
    cos_t = jnp.concatenate([cos_t, jnp.ones((n_ctx_rows, 128), f32)], axis=0)
    sin_t = jnp.concatenate([sin_t, jnp.zeros((n_ctx_rows, 128), f32)], axis=0)
    return cos_t, sin_t


def kernel(x, c, ctx, c_ctx, w_mod, b_mod, norm1_g, norm2_g, final_norm_g, w_in, a_sink,
           p_w, p_scale, c_ln_g, c_ln_b, c_ws, c_bs, g_w2, g_b, g_norm_g,
           w_branch, w_gate, w_out, router_w, router_b, e_gate, e_up, e_down):
    n_batch, seq_len, d = x.shape
    n_ctx = ctx.shape[1]
    depth = w_mod.shape[0]
    n_lat = n_batch * seq_len
    n_all = n_lat + n_batch * n_ctx
    tiles_per_mod = seq_len // ROW_TILE

    cond = jnp.zeros((8, d), f32).at[0:n_batch].set(c).at[n_batch].set(c_ctx)
    mods_all = _modulation(cond, w_mod, b_mod)
    cos_t, sin_t = _rope_tables(seq_len, n_batch, n_batch * n_ctx)
    rw_f = jnp.transpose(router_w)
    rw_hi = rw_f.astype(bf16)
    rw_t = jnp.concatenate([rw_hi, (rw_f - rw_hi.astype(f32)).astype(bf16)], axis=0)
    rb_col = router_b.reshape(N_EXPERTS, 1)
    final_g = final_norm_g.reshape(1, d)

    w_in_b = jnp.pad(w_in, ((0, 0), (0, 0), (0, 96))).astype(bf16)
    w_gate_b = w_gate.astype(bf16)
    w_branch_b = w_branch.astype(bf16)
    w_out_b = w_out.astype(bf16)

    n_lat_tiles = n_lat // ROW_TILE
    xa, xb, b_tile0 = x.reshape(n_lat, d), ctx.reshape(n_batch * n_ctx, d), 0
    for l in range(depth):
        last = l == depth - 1
        n_out = n_lat if last else n_all
        mods = mods_all[l].reshape(8, 1, 6 * d)
        w2p = jnp.zeros((128, 512), f32)
        w2p = w2p.at[0:GLA_RANK, 0:256].set(g_w2[l, 0]).at[GLA_RANK:2 * GLA_RANK, 256:512].set(g_w2[l, 1])
        w2_hi = w2p.astype(bf16)
        w2_split = jnp.concatenate([w2_hi, w2_hi, (w2p - w2_hi.astype(f32)).astype(bf16)], axis=0)
        bs_b = jnp.broadcast_to(c_bs[l][:, :, None], (4, SG_CHUNK, 128))
        h, qkv, z_pool, s_br, z_gla, gates = _in_proj(
            xa, xb, n_lat_tiles, b_tile0, n_all, l, norm1_g[l].reshape(1, d), mods, cos_t, sin_t,
            w_in_b, w2_split, g_b[l].reshape(1, 512), c_ln_g[l].reshape(1, 512),
            c_ln_b[l].reshape(1, 512), c_ws[l].astype(bf16), bs_b, tiles_per_mod)
        a_br = _window_attention(qkv, a_sink[l], n_batch, seq_len, n_ctx)
        pw = p_w[l].astype(bf16)
        ps = p_scale[l].reshape(1, 512)
        p_br = _multi_scale_pool(z_pool, 0, n_lat, seq_len, ROW_TILE, pw, ps)
        o_f, o_b = _gla_scan(z_gla, gates, n_batch, seq_len, n_ctx)
        if last:
            a_c, p_c = a_br, p_br
        else:
            a_c = _context_attention(qkv, a_sink[l], n_batch, n_lat, n_ctx)
            p_c = _multi_scale_pool(z_pool, n_lat, n_batch * n_ctx, n_ctx, n_ctx, pw, ps)
        acc = _gated_merge(h, a_br, a_c, p_br, p_c, s_br, o_f, o_b, z_gla, g_norm_g[l].reshape(1, 512),
                           n_out, n_lat_tiles, l, w_gate_b, w_branch_b)
        x_all = _out_proj_residual(acc, l, w_out_b, xa, xb, n_lat_tiles, b_tile0, mods, n_out,
                                   tiles_per_mod)
        x_all = _moe(x_all, n_out, norm2_g[l].reshape(1, d), mods, tiles_per_mod, rw_t, rb_col,
                     l, e_gate, e_up, e_down, final_g, last)
        xa, xb, b_tile0 = x_all, x_all, n_lat_tiles
    return x_all.reshape(n_batch, seq_len, d)
```

```python
import functools

import numpy as np
import jax
import jax.numpy as jnp
from jax import lax
from jax.experimental import pallas as pl
from jax.experimental.pallas import tpu as pltpu

f32 = jnp.float32
bf16 = jnp.bfloat16

EPS = 1e-6
GRID_W = 64
ROPE_BASE = 10000.0

HEAD_DIM = 64
N_Q_HEADS = 8
N_KV_HEADS = 2
Q_PER_KV = N_Q_HEADS // N_KV_HEADS
ATT_BLOCK = 128
ATT_STACK = 4
POOL_WINDOWS = (2, 4, 8, 16)
POOL_HALO = 8
SG_CHUNK = 128
GLA_HEADS = 4
GLA_DK = 64
GLA_DV = 128
GLA_RANK = 16
GLA_TAU = 16.0
GLA_CHUNK = 64
N_EXPERTS = 16
EXPERTS_PER_GROUP = 4
N_GROUPS = 4

ROW_TILE = 512
GLA_BLOCK = 256
MOE_TILE = 256
V7X_VMEM_BYTES = 64 * 1024 * 1024
VMEM_LIMIT = V7X_VMEM_BYTES * 7 // 8
EXPERT_VMEM_LIMIT = V7X_VMEM_BYTES * 31 // 32


def _cparams(*sem, vmem_limit=VMEM_LIMIT):
    return pltpu.CompilerParams(dimension_semantics=sem, vmem_limit_bytes=vmem_limit)


def _dot(a, b, dims=(((1,), (0,)), ((), ())), precision=None):
    return lax.dot_general(a, b, dims, precision=precision, preferred_element_type=f32)


_NT = (((1,), (1,)), ((), ()))
_TN = (((0,), (0,)), ((), ()))


def _mod_kernel(c_ref, w_ref, b_ref, o_ref):
    c = c_ref[...]
    a = c * jax.nn.sigmoid(c)
    w = w_ref[...]
    a_hi = a.astype(bf16)
    a_lo = (a - a_hi.astype(f32)).astype(bf16)
    w_hi = w.astype(bf16)
    w_lo = (w - w_hi.astype(f32)).astype(bf16)
    n = a.shape[0]
    both = _dot(jnp.concatenate([a_hi, a_lo], axis=0), w_hi)
    o_ref[...] = both[0:n] + both[n:2 * n] + _dot(a_hi, w_lo) + b_ref[...]


def _modulation(cond, w_mod, b_mod):
    n_layers, d, d6 = w_mod.shape
    tn = 1024
    return pl.pallas_call(
        _mod_kernel,
        grid=(n_layers, d6 // tn),
        in_specs=[
            pl.BlockSpec((8, d), lambda l, j: (0, 0)),
            pl.BlockSpec((None, d, tn), lambda l, j: (l, 0, j)),
            pl.BlockSpec((None, 1, tn), lambda l, j: (l, 0, j)),
        ],
        out_specs=pl.BlockSpec((None, 8, tn), lambda l, j: (l, 0, j)),
        out_shape=jax.ShapeDtypeStruct((n_layers, 8, d6), f32),
        compiler_params=_cparams("arbitrary", "arbitrary"),
        name="modulation",
    )(cond, w_mod, b_mod.reshape(n_layers, 1, d6))


def _norm_mod(x, g, shift, scale):
    y = x * lax.rsqrt(jnp.mean(x * x, axis=-1, keepdims=True) + EPS)
    return (y * g) * (1.0 + scale) + shift


def _two_source_rows(xa_ref, xb_ref, n_a_tiles):
    return jnp.where(pl.program_id(0) < n_a_tiles, xa_ref[...], xb_ref[...])


def _two_source_specs(tm, d, n_a_tiles, b_tile0):
    return [pl.BlockSpec((tm, d), lambda i: (jnp.minimum(i, n_a_tiles - 1), 0)),
            pl.BlockSpec((tm, d), lambda i: (b_tile0 + jnp.maximum(i - n_a_tiles, 0), 0),
                         pipeline_mode=pl.Buffered(1))]


def _spatial_gate_rows(uv, lg_ref, lb_ref, ws_ref, bs_ref, o_ref):
    a = jax.nn.gelu(uv)
    u = a[:, 0:512]
    v = a[:, 512:1024]
    mu = jnp.mean(v, axis=-1, keepdims=True)
    var = jnp.mean(jnp.square(v - mu), axis=-1, keepdims=True)
    vn = ((v - mu) * lax.rsqrt(var + EPS) * lg_ref[...] + lb_ref[...]).astype(bf16)
    for c in range(uv.shape[0] // SG_CHUNK):
        rows = slice(c * SG_CHUNK, (c + 1) * SG_CHUNK)
        for g in range(4):
            cols = slice(g * 128, (g + 1) * 128)
            mixed = _dot(ws_ref[g], vn[rows, cols]) + bs_ref[g]
            o_ref[rows, cols] = (u[rows, cols] * mixed).astype(o_ref.dtype)


def _in_proj_kernel(xa_ref, xb_ref, g_ref, sh_ref, sc_ref, cos_ref, sin_ref, w_ref,
                    w2_ref, gb_ref, lg_ref, lb_ref, ws_ref, bs_ref,
                    h_ref, qkv_ref, pool_ref, sg_ref, gla_ref, gate_ref, *, n_a_tiles):
    x = _two_source_rows(xa_ref, xb_ref, n_a_tiles)
    hb = _norm_mod(x, g_ref[...], sh_ref[...], sc_ref[...]).astype(bf16)
    h_ref[...] = hb

    cos = cos_ref[...]
    sin = sin_ref[...]
    lane = lax.broadcasted_iota(jnp.int32, cos.shape, 1)
    first_half = (lane % 32) < 16

    def rope(z):
        rot = jnp.where(first_half, pltpu.roll(z, 112, 1), pltpu.roll(z, 16, 1))
        return z * cos + rot * sin

    zq = _dot(hb, w_ref[:, 0:512])
    for c in range(4):
        qkv_ref[:, c * 128:(c + 1) * 128] = (
            rope(zq[:, c * 128:(c + 1) * 128]) * (HEAD_DIM ** -0.5)).astype(bf16)
    zkv = _dot(hb, w_ref[:, 512:768])
    qkv_ref[:, 512:640] = rope(zkv[:, 0:128]).astype(bf16)
    qkv_ref[:, 640:768] = zkv[:, 128:256].astype(bf16)
    pool_ref[...] = _dot(hb, w_ref[:, 768:1280])
    _spatial_gate_rows(_dot(hb, w_ref[:, 1280:2304]), lg_ref, lb_ref, ws_ref, bs_ref, sg_ref)
    gla_ref[...] = _dot(hb, w_ref[:, 2304:3840])
    low_rank = _dot(hb, w_ref[:, 3840:3968])
    lr_hi = low_rank.astype(bf16)
    lr_lo = (low_rank - lr_hi.astype(f32)).astype(bf16)
    logit = _dot(jnp.concatenate([lr_hi, lr_lo, lr_hi], axis=1), w2_ref[...]) + gb_ref[...]
    gate_ref[...] = _log_sigmoid(logit) / GLA_TAU


def _log_sigmoid(x):
    return jnp.minimum(x, 0.0) - jnp.log(1.0 + jnp.exp(-jnp.abs(x)))


def _in_proj(xa, xb, n_a_tiles, b_tile0, rows, layer, g, mods, cos_t, sin_t, w_pad, w2p, gbias,
             ln_g, ln_b, ws, bs_b, tiles_per_mod):
    d = xa.shape[1]
    n_w = w_pad.shape[2]
    tm = ROW_TILE
    mod_spec = lambda k: pl.BlockSpec((None, 1, d), lambda i: (i // tiles_per_mod, 0, k))
    row_spec = lambda w: pl.BlockSpec((tm, w), lambda i: (i, 0))
    return pl.pallas_call(
        functools.partial(_in_proj_kernel, n_a_tiles=n_a_tiles),
        grid=(rows // tm,),
        in_specs=_two_source_specs(tm, d, n_a_tiles, b_tile0) + [
            pl.BlockSpec((1, d), lambda i: (0, 0)),
            mod_spec(0), mod_spec(1),
            row_spec(128), row_spec(128),
            pl.BlockSpec((None, d, n_w), lambda i: (layer, 0, 0), pipeline_mode=pl.Buffered(1)),
            pl.BlockSpec((384, 512), lambda i: (0, 0)),
            pl.BlockSpec((1, 512), lambda i: (0, 0)),
            pl.BlockSpec((1, 512), lambda i: (0, 0)),
            pl.BlockSpec((1, 512), lambda i: (0, 0)),
            pl.BlockSpec((4, 128, 128), lambda i: (0, 0, 0)),
            pl.BlockSpec((4, 128, 128), lambda i: (0, 0, 0)),
        ],
        out_specs=[row_spec(d), row_spec(768), row_spec(512), row_spec(512), row_spec(1536),
                   row_spec(512)],
        out_shape=[
            jax.ShapeDtypeStruct((rows, d), bf16),
            jax.ShapeDtypeStruct((rows, 768), bf16),
            jax.ShapeDtypeStruct((rows, 512), f32),
            jax.ShapeDtypeStruct((rows, 512), bf16),
            jax.ShapeDtypeStruct((rows, 1536), f32),
            jax.ShapeDtypeStruct((rows, 512), f32),
        ],
        compiler_params=_cparams("arbitrary"),
        name="in_proj",
    )(xa, xb, g, mods, mods, cos_t, sin_t, w_pad, w2p, gbias, ln_g, ln_b, ws, bs_b)


def _sink_column(sink_ref, kvh, rows_per_head):
    r = lax.broadcasted_iota(jnp.int32, (Q_PER_KV * rows_per_head, 1), 0) // rows_per_head
    col = jnp.full(r.shape, sink_ref[kvh * Q_PER_KV], f32)
    for g in range(1, Q_PER_KV):
        col = jnp.where(r == g, sink_ref[kvh * Q_PER_KV + g], col)
    return col


def _win_attn_kernel(sink_ref, main_ref, prev_ref, next_ref, ctx_ref, o_ref, *, tiles_per_seq, seq_len):
    i = pl.program_id(0)
    tq = main_ref.shape[0]
    n_sub = tq // ATT_BLOCK
    blk0 = (i % tiles_per_seq) * n_sub
    blocks_per_seq = seq_len // ATT_BLOCK
    r = lax.broadcasted_iota(jnp.int32, (ATT_STACK * ATT_BLOCK, ATT_BLOCK), 0) % ATT_BLOCK
    j = lax.broadcasted_iota(jnp.int32, (ATT_STACK * ATT_BLOCK, ATT_BLOCK), 1)
    keep_prev = j >= r
    keep_next = j <= r
    n_ctx = ctx_ref.shape[0]

    def with_ones(v):
        return jnp.concatenate([v, jnp.ones_like(v)], axis=1)

    for kvh in range(N_KV_HEADS):
        kc = 512 + kvh * HEAD_DIM
        vc = 640 + kvh * HEAD_DIM
        k_all = jnp.concatenate([prev_ref[:, kc:kc + HEAD_DIM], main_ref[:, kc:kc + HEAD_DIM],
                                 next_ref[:, kc:kc + HEAD_DIM]], axis=0)
        v_all = with_ones(jnp.concatenate([prev_ref[:, vc:vc + HEAD_DIM], main_ref[:, vc:vc + HEAD_DIM],
                                           next_ref[:, vc:vc + HEAD_DIM]], axis=0))
        k_ctx = ctx_ref[:, kc:kc + HEAD_DIM]
        v_ctx = with_ones(ctx_ref[:, vc:vc + HEAD_DIM])
        sink_all = _sink_column(sink_ref, kvh, ATT_BLOCK)
        for sb, g0 in ((sb, g0) for sb in range(n_sub) for g0 in range(0, Q_PER_KV, ATT_STACK)):
            rows = slice(sb * ATT_BLOCK, (sb + 1) * ATT_BLOCK)
            heads = range(g0, g0 + ATT_STACK)
            sink = sink_all[g0 * ATT_BLOCK:(g0 + ATT_STACK) * ATT_BLOCK]
            q = jnp.concatenate(
                [main_ref[rows, (kvh * Q_PER_KV + g) * HEAD_DIM:(kvh * Q_PER_KV + g + 1) * HEAD_DIM]
                 for g in heads], axis=0)
            band = slice(sb * ATT_BLOCK, (sb + 3) * ATT_BLOCK)
            s_band = _dot(q, k_all[band], _NT)
            s_ctx = _dot(q, k_ctx, _NT)
            blk = blk0 + sb
            parts = [jnp.where(keep_prev & (blk >= 1), s_band[:, 0:ATT_BLOCK], -1e30),
                     s_band[:, ATT_BLOCK:2 * ATT_BLOCK],
                     jnp.where(keep_next & (blk <= blocks_per_seq - 2), s_band[:, 2 * ATT_BLOCK:], -1e30)]
            parts += [s_ctx[:, c:c + 128] for c in range(0, n_ctx, 128)]
            m = jnp.maximum(sink, jnp.max(functools.reduce(jnp.maximum, parts), axis=-1, keepdims=True))
            p = jnp.concatenate([jnp.exp(x - m).astype(bf16) for x in parts], axis=1)
            o_sum = (_dot(p[:, 0:3 * ATT_BLOCK], v_all[band]) + _dot(p[:, 3 * ATT_BLOCK:], v_ctx))
            denom = jnp.exp(sink - m) + o_sum[:, HEAD_DIM:HEAD_DIM + 1]
            o = o_sum[:, 0:HEAD_DIM] / denom
            for n, g in enumerate(heads):
                c0 = (kvh * Q_PER_KV + g) * HEAD_DIM
                o_ref[rows, c0:c0 + HEAD_DIM] = o[n * ATT_BLOCK:(n + 1) * ATT_BLOCK].astype(o_ref.dtype)


def _window_attention(qkv, sink, n_batch, seq_len, n_ctx):
    tq = ROW_TILE
    n_lat = n_batch * seq_len
    tiles_per_seq = seq_len // tq
    sub = tq // ATT_BLOCK
    n_blocks = n_lat // ATT_BLOCK
    ctx_blk0 = n_lat // n_ctx
    w = qkv.shape[1]
    kern = functools.partial(_win_attn_kernel, tiles_per_seq=tiles_per_seq, seq_len=seq_len)
    return pl.pallas_call(
        kern,
        grid=(n_lat // tq,),
        in_specs=[
            pl.BlockSpec(memory_space=pltpu.SMEM),
            pl.BlockSpec((tq, w), lambda i: (i, 0)),
            pl.BlockSpec((ATT_BLOCK, w), lambda i: (jnp.maximum(i * sub - 1, 0), 0)),
            pl.BlockSpec((ATT_BLOCK, w), lambda i: (jnp.minimum(i * sub + sub, n_blocks - 1), 0)),
            pl.BlockSpec((n_ctx, w), lambda i: (ctx_blk0 + i // tiles_per_seq, 0)),
        ],
        out_specs=pl.BlockSpec((tq, 512), lambda i: (i, 0)),
        out_shape=jax.ShapeDtypeStruct((n_lat, 512), bf16),
        compiler_params=_cparams("arbitrary"),
        name="window_attention",
    )(sink, qkv, qkv, qkv, qkv)


def _ctx_attn_kernel(sink_ref, qkv_ref, o_ref):
    n = qkv_ref.shape[0]
    for kvh in range(N_KV_HEADS):
        kc = 512 + kvh * HEAD_DIM
        vc = 640 + kvh * HEAD_DIM
        q = jnp.concatenate(
            [qkv_ref[:, (kvh * Q_PER_KV + g) * HEAD_DIM:(kvh * Q_PER_KV + g + 1) * HEAD_DIM]
             for g in range(Q_PER_KV)], axis=0)
        s = _dot(q, qkv_ref[:, kc:kc + HEAD_DIM], _NT)
        sink = _sink_column(sink_ref, kvh, n)
        m = jnp.maximum(sink, jnp.max(s, axis=-1, keepdims=True))
        p = jnp.exp(s - m)
        denom = jnp.exp(sink - m) + jnp.sum(p, axis=-1, keepdims=True)
        o = _dot(p.astype(bf16), qkv_ref[:, vc:vc + HEAD_DIM]) / denom
        for g in range(Q_PER_KV):
            c0 = (kvh * Q_PER_KV + g) * HEAD_DIM
            o_ref[:, c0:c0 + HEAD_DIM] = o[g * n:(g + 1) * n].astype(o_ref.dtype)


def _context_attention(qkv, sink, n_batch, n_lat, n_ctx):
    w = qkv.shape[1]
    blk0 = n_lat // n_ctx
    return pl.pallas_call(
        _ctx_attn_kernel,
        grid=(n_batch,),
        in_specs=[pl.BlockSpec(memory_space=pltpu.SMEM),
                  pl.BlockSpec((n_ctx, w), lambda b: (blk0 + b, 0))],
        out_specs=pl.BlockSpec((n_ctx, 512), lambda b: (b, 0)),
        out_shape=jax.ShapeDtypeStruct((n_batch * n_ctx, 512), bf16),
        compiler_params=_cparams("arbitrary"),
        name="context_attention",
    )(sink, qkv)


def _pool_kernel(main_ref, prev_ref, next_ref, pw_ref, ps_ref, o_ref, xe_ref, *, tiles_per_seq, seq_len):
    i = pl.program_id(0)
    tp = main_ref.shape[0]
    t_in_seq = i % tiles_per_seq
    h = POOL_HALO
    xe_ref[0:h, :] = jnp.where(t_in_seq == 0, 0.0, prev_ref[...])
    xe_ref[h:h + tp, :] = main_ref[...]
    xe_ref[h + tp:2 * h + tp, :] = jnp.where(t_in_seq == tiles_per_seq - 1, 0.0, next_ref[...])
    pos = t_in_seq * tp + lax.broadcasted_iota(jnp.int32, (tp, 1), 0)
    for gi, w in enumerate(POOL_WINDOWS):
        cols = slice(gi * 128, (gi + 1) * 128)
        acc = xe_ref[h - w // 2:h - w // 2 + tp, cols]
        for u in range(-w // 2 + 1, w // 2):
            acc = acc + xe_ref[h + u:h + u + tp, cols]
        lo = jnp.maximum(pos - w // 2, 0)
        hi = jnp.minimum(pos + w // 2, seq_len)
        cnt = (hi - lo).astype(f32)
        pooled = acc / cnt - main_ref[:, cols]
        y = _dot(pooled.astype(bf16), pw_ref[gi])
        o_ref[:, cols] = (y * ps_ref[:, cols]).astype(o_ref.dtype)


def _multi_scale_pool(z, row0, n_rows, seq_len, tp, p_w, p_scale):
    tiles_per_seq = seq_len // tp
    t0 = row0 // tp
    h0 = row0 // POOL_HALO
    hb = tp // POOL_HALO
    n_halo = n_rows // POOL_HALO
    kern = functools.partial(_pool_kernel, tiles_per_seq=tiles_per_seq, seq_len=seq_len)
    return pl.pallas_call(
        kern,
        grid=(n_rows // tp,),
        in_specs=[
            pl.BlockSpec((tp, 512), lambda i: (t0 + i, 0)),
            pl.BlockSpec((POOL_HALO, 512), lambda i: (h0 + jnp.maximum(i * hb - 1, 0), 0)),
            pl.BlockSpec((POOL_HALO, 512), lambda i: (h0 + jnp.minimum((i + 1) * hb, n_halo - 1), 0)),
            pl.BlockSpec((4, 128, 128), lambda i: (0, 0, 0)),
            pl.BlockSpec((1, 512), lambda i: (0, 0)),
        ],
        out_specs=pl.BlockSpec((tp, 512), lambda i: (i, 0)),
        out_shape=jax.ShapeDtypeStruct((n_rows, 512), bf16),
        scratch_shapes=[pltpu.VMEM((tp + 2 * POOL_HALO, 512), f32)],
        compiler_params=_cparams("arbitrary"),
        name="multi_scale_pool",
    )(z, z, z, p_w, p_scale)


GLA_SEGMENTS = (1, 2, 4, 8, 16, 32, 64)


def _gla_tables():
    c = GLA_CHUNK
    i = np.arange(c)[:, None]
    j = np.arange(c)[None, :]
    tri = (j <= i).astype(np.float32)
    mask = np.zeros((2, 7, c, c), np.float32)
    mask[:, 0] = np.eye(c)
    for lv, s in enumerate(GLA_SEGMENTS[:-1]):
        m = ((i // (2 * s)) == (j // (2 * s))) & ((i // s) % 2 == 1) & ((j // s) % 2 == 0)
        mask[0, 1 + lv] = m
        mask[1, 1 + lv] = m.T
    mask = np.tile(mask, (1, 1, 1, GLA_HEADS))
    return tri, mask


def _segment_sums(g, cum):
    c = GLA_CHUNK
    row = lax.broadcasted_iota(jnp.int32, (c, 1), 0)
    zero = jnp.zeros_like(g)
    before = [None] + [pltpu.roll(g, kk, 0) for kk in (1, 2, 3)]
    after = [None] + [pltpu.roll(g, c - kk, 0) for kk in (1, 2, 3)]
    a, r = {1: g}, {1: zero}
    for s in (2, 4):
        pos = row % s
        a_s, r_s = g, zero
        for kk in range(1, s):
            a_s = a_s + jnp.where(pos >= kk, before[kk], 0.0)
            r_s = r_s + jnp.where(pos < s - kk, after[kk], 0.0)
        a[s], r[s] = a_s, r_s
    blocks = [cum[8 * b:8 * b + 8] for b in range(c // 8)]
    last = [cum[8 * b + 7:8 * b + 8] for b in range(c // 8)]
    for s in (8, 16, 32, 64):
        nb = s // 8
        a_blk, r_blk = [], []
        for b in range(c // 8):
            prev_end = (b // nb) * nb - 1
            a_blk.append(blocks[b] - last[prev_end] if prev_end >= 0 else blocks[b])
            r_blk.append(last[(b // nb + 1) * nb - 1] - blocks[b])
        a[s] = jnp.concatenate(a_blk, axis=0)
        r[s] = jnp.concatenate(r_blk, axis=0)
    return a, r


def _gla_chunk(d, q, k, v, g, tri_ref, mask_ref, st_ref):
    c = GLA_CHUNK
    w = g.shape[1]
    g_hi = g.astype(bf16)
    r1 = g - g_hi.astype(f32)
    g_mid = r1.astype(bf16)
    g_lo = (r1 - g_mid.astype(f32)).astype(bf16)
    cum3 = _dot(tri_ref[...], jnp.concatenate([g_hi, g_mid, g_lo], axis=1))
    cum = cum3[:, 0:w] + cum3[:, w:2 * w] + cum3[:, 2 * w:3 * w]
    a, r = _segment_sums(g, cum)
    if d == 0:
        cq = [a[s] for s in GLA_SEGMENTS]
        ck = [r[s] for s in GLA_SEGMENTS]
    else:
        cq = [r[s] + g for s in GLA_SEGMENTS]
        ck = [a[s] - g for s in GLA_SEGMENTS]
    total = cum[c - 1:c]

    q = q * (GLA_DK ** -0.5)
    head_of_lane = lax.broadcasted_iota(jnp.int32, (1, w), 1) // GLA_DK
    vhead_of_lane = lax.broadcasted_iota(jnp.int32, (1, v.shape[1]), 1) // GLA_DV

    def stack_heads(x, lane_head):
        return jnp.concatenate([jnp.where(lane_head == h, x, jnp.zeros_like(x))
                                for h in range(GLA_HEADS)], axis=0)

    kb = k.astype(bf16)
    k_plain = stack_heads(kb, head_of_lane)
    q01 = jnp.concatenate([q.astype(bf16), (q * jnp.exp(cq[0])).astype(bf16)], axis=0)
    s01 = _dot(q01, k_plain, _NT)
    att = mask_ref[d, 0] * s01[0:c] + mask_ref[d, 1] * s01[c:2 * c]
    for lv in range(1, 6):
        qs = (q * jnp.exp(cq[lv])).astype(bf16)
        ks = stack_heads((k * jnp.exp(ck[lv])).astype(bf16), head_of_lane)
        att = att + mask_ref[d, 1 + lv] * _dot(qs, ks, _NT)
    v_heads = stack_heads(v.astype(bf16), vhead_of_lane)
    st = st_ref[d]
    q_state = (q * jnp.exp(cq[6])).astype(bf16)
    o = _dot(att.astype(bf16), v_heads) + _dot(q_state, st.astype(bf16), _NT)
    k_state = stack_heads((k * jnp.exp(ck[6])).astype(bf16), head_of_lane)
    st_ref[d] = st * jnp.exp(total) + _dot(v_heads, k_state, _TN)
    return o


def _gla_kernel(qf_ref, kf_ref, vf_ref, gf_ref, qb_ref, kb_ref, vb_ref, gb_ref,
                tri_ref, mask_ref, of_ref, ob_ref, st_ref):
    @pl.when(pl.program_id(1) == 0)
    def _():
        st_ref[...] = jnp.zeros_like(st_ref)

    n_chunks = qf_ref.shape[0] // GLA_CHUNK
    for ci in range(n_chunks):
        rf = slice(ci * GLA_CHUNK, (ci + 1) * GLA_CHUNK)
        of_ref[rf, :] = _gla_chunk(0, qf_ref[rf, :], kf_ref[rf, :], vf_ref[rf, :], gf_ref[rf, :],
                                   tri_ref, mask_ref, st_ref)
        cb = n_chunks - 1 - ci
        rb = slice(cb * GLA_CHUNK, (cb + 1) * GLA_CHUNK)
        ob_ref[rb, :] = _gla_chunk(1, qb_ref[rb, :], kb_ref[rb, :], vb_ref[rb, :], gb_ref[rb, :],
                                   tri_ref, mask_ref, st_ref)


def _gla_scan(z_gla, gates, n_batch, seq_len, n_ctx):
    rows = z_gla.shape[0]
    tb = GLA_BLOCK
    assert n_ctx == tb
    lat_blocks = seq_len // tb
    ctx_blk0 = n_batch * lat_blocks
    tri, mask = _gla_tables()

    def fwd_row(b, s):
        return jnp.where(s == 0, ctx_blk0 + b, b * lat_blocks + s - 1)

    def bwd_row(b, s):
        return jnp.where(s == 0, ctx_blk0 + b, b * lat_blocks + lat_blocks - s)

    def specs(row, direction):
        return [
            pl.BlockSpec((tb, 256), lambda b, s: (row(b, s), 0)),
            pl.BlockSpec((tb, 256), lambda b, s: (row(b, s), 1)),
            pl.BlockSpec((tb, 512), lambda b, s: (row(b, s), 1)),
            pl.BlockSpec((tb, 256), lambda b, s: (row(b, s), direction)),
        ]

    const = lambda shape: pl.BlockSpec(shape, lambda b, s: (0,) * len(shape))
    return pl.pallas_call(
        _gla_kernel,
        grid=(n_batch, 1 + lat_blocks),
        in_specs=specs(fwd_row, 0) + specs(bwd_row, 1) + [
            const(tri.shape), const(mask.shape)],
        out_specs=[pl.BlockSpec((tb, 512), lambda b, s: (fwd_row(b, s), 0)),
                   pl.BlockSpec((tb, 512), lambda b, s: (bwd_row(b, s), 0))],
        out_shape=[jax.ShapeDtypeStruct((rows, 512), f32)] * 2,
        scratch_shapes=[pltpu.VMEM((2, GLA_HEADS * GLA_DV, GLA_HEADS * GLA_DK), f32)],
        compiler_params=_cparams("arbitrary", "arbitrary"),
        name="gla_scan",
    )(z_gla, z_gla, z_gla, gates, z_gla, z_gla, z_gla, gates,
      jnp.asarray(tri, bf16), jnp.asarray(mask, f32))


def _merge_kernel(h_ref, a_ref, ac_ref, p_ref, pc_ref, s_ref, of_ref, ob_ref, r_ref, ng_ref, wg_ref, wb_ref,
                  o_ref, g_ref, *, n_lat_tiles):
    @pl.when(pl.program_id(1) == 0)
    def _():
        o = of_ref[...] + ob_ref[...]
        r = r_ref[...]
        gate = r * jax.nn.sigmoid(r)
        for h in range(GLA_HEADS):
            cols = slice(h * GLA_DV, (h + 1) * GLA_DV)
            oh = o[:, cols]
            oh = oh * lax.rsqrt(jnp.mean(oh * oh, axis=-1, keepdims=True) + EPS)
            g_ref[:, cols] = (oh * ng_ref[:, cols] * gate[:, cols]).astype(g_ref.dtype)

    hb = h_ref[...]
    is_lat = pl.program_id(0) < n_lat_tiles
    branches = (jnp.where(is_lat, a_ref[...], ac_ref[...]), jnp.where(is_lat, p_ref[...], pc_ref[...]),
                s_ref[...], g_ref[...])
    acc = None
    for br, b in enumerate(branches):
        gate = jax.nn.sigmoid(_dot(hb, wg_ref[br]))
        term = gate * _dot(b, wb_ref[br])
        acc = term if acc is None else acc + term
    o_ref[...] = acc.astype(o_ref.dtype)


def _gated_merge(h, a_lat, a_ctx, p_lat, p_ctx, s_br, o_f, o_b, z_gla, norm_g, n_rows, n_lat_tiles, layer,
                 wg, wb):
    d = h.shape[1]
    tm, tn = ROW_TILE, 512
    bspec = pl.BlockSpec((tm, 512), lambda i, j: (i, 0))
    lat_spec = pl.BlockSpec((tm, 512), lambda i, j: (jnp.minimum(i, n_lat_tiles - 1), 0))
    ctx_spec = pl.BlockSpec((tm, 512), lambda i, j: (jnp.maximum(i - n_lat_tiles, 0), 0),
                            pipeline_mode=pl.Buffered(1))
    return pl.pallas_call(
        functools.partial(_merge_kernel, n_lat_tiles=n_lat_tiles),
        grid=(n_rows // tm, d // tn),
        in_specs=[pl.BlockSpec((tm, d), lambda i, j: (i, 0)), lat_spec, ctx_spec, lat_spec, ctx_spec,
                  bspec, bspec, bspec,
                  pl.BlockSpec((tm, 512), lambda i, j: (i, 2)),
                  pl.BlockSpec((1, 512), lambda i, j: (0, 0)),
                  pl.BlockSpec((None, 4, d, tn), lambda i, j: (layer, 0, 0, j)),
                  pl.BlockSpec((None, 4, 512, tn), lambda i, j: (layer, 0, 0, j))],
        out_specs=pl.BlockSpec((tm, tn), lambda i, j: (i, j)),
        out_shape=jax.ShapeDtypeStruct((n_rows, d), bf16),
        scratch_shapes=[pltpu.VMEM((tm, 512), bf16)],
        compiler_params=_cparams("arbitrary", "arbitrary"),
        name="gated_merge",
    )(h, a_lat, a_ctx, p_lat, p_ctx, s_br, o_f, o_b, z_gla, norm_g, wg, wb)


def _out_proj_kernel(acc_ref, wo_ref, xa_ref, xb_ref, gate_ref, o_ref, *, n_a_tiles):
    x = _two_source_rows(xa_ref, xb_ref, n_a_tiles)
    o_ref[...] = x + gate_ref[...] * _dot(acc_ref[...], wo_ref[...])


def _out_proj_residual(acc, layer, wo, xa, xb, n_a_tiles, b_tile0, mods, n_rows, tiles_per_mod):
    d = acc.shape[1]
    tm = ROW_TILE
    return pl.pallas_call(
        functools.partial(_out_proj_kernel, n_a_tiles=n_a_tiles),
        grid=(n_rows // tm,),
        in_specs=[pl.BlockSpec((tm, d), lambda i: (i, 0)),
                  pl.BlockSpec((None, d, d), lambda i: (layer, 0, 0))]
        + _two_source_specs(tm, d, n_a_tiles, b_tile0)
        + [pl.BlockSpec((None, 1, d), lambda i: (i // tiles_per_mod, 0, 2))],
        out_specs=pl.BlockSpec((tm, d), lambda i: (i, 0)),
        out_shape=jax.ShapeDtypeStruct((n_rows, d), f32),
        compiler_params=_cparams("arbitrary"),
        name="out_proj_residual",
    )(acc, wo, xa, xb, mods)


def _router_kernel(x_ref, g_ref, sh_ref, sc_ref, rw_ref, rb_ref, tri_ref, info_ref, cnt_ref, h2_ref, carry_ref):
    i = pl.program_id(0)

    @pl.when(i == 0)
    def _():
        carry_ref[...] = jnp.zeros_like(carry_ref)

    h2 = _norm_mod(x_ref[...], g_ref[...], sh_ref[...], sc_ref[...])
    h2_ref[...] = h2
    tm = h2.shape[0]
    h_hi = h2.astype(bf16)
    h_lo = (h2 - h_hi.astype(f32)).astype(bf16)
    part = _dot(rw_ref[...], h_hi, _NT)
    logits = (part[0:N_EXPERTS] + part[N_EXPERTS:2 * N_EXPERTS]
              + _dot(rw_ref[0:N_EXPERTS, :], h_lo, _NT))
    score = jax.nn.sigmoid(logits)
    biased = score + rb_ref[...]
    b = [biased[e:e + 1] for e in range(N_EXPERTS)]
    sc = [score[e:e + 1] for e in range(N_EXPERTS)]

    def top2_sum(v):
        hi01, lo01 = jnp.maximum(v[0], v[1]), jnp.minimum(v[0], v[1])
        hi23, lo23 = jnp.maximum(v[2], v[3]), jnp.minimum(v[2], v[3])
        return jnp.maximum(hi01, hi23) + jnp.maximum(jnp.minimum(hi01, hi23), jnp.maximum(lo01, lo23))

    best = jnp.zeros((1, tm), jnp.int32)
    best_score = top2_sum(b[0:4])
    for gidx in range(1, N_GROUPS):
        gs = top2_sum(b[4 * gidx:4 * gidx + 4])
        take = gs > best_score
        best = jnp.where(take, gidx, best)
        best_score = jnp.where(take, gs, best_score)
    cb, cs = [], []
    for kk in range(EXPERTS_PER_GROUP):
        vb, vs = b[kk], sc[kk]
        for gidx in range(1, N_GROUPS):
            vb = jnp.where(best == gidx, b[4 * gidx + kk], vb)
            vs = jnp.where(best == gidx, sc[4 * gidx + kk], vs)
        cb.append(vb)
        cs.append(vs)
    i1 = jnp.zeros((1, tm), jnp.int32)
    m1, w1 = cb[0], cs[0]
    for kk in range(1, EXPERTS_PER_GROUP):
        take = cb[kk] > m1
        i1 = jnp.where(take, kk, i1)
        m1 = jnp.where(take, cb[kk], m1)
        w1 = jnp.where(take, cs[kk], w1)
    i2 = jnp.where(i1 == 0, 1, 0).astype(jnp.int32)
    m2 = jnp.where(i1 == 0, cb[1], cb[0])
    w2 = jnp.where(i1 == 0, cs[1], cs[0])
    for kk in range(1, EXPERTS_PER_GROUP):
        take = (cb[kk] > m2) & (i1 != kk) & (i2 != kk)
        i2 = jnp.where(take, kk, i2)
        m2 = jnp.where(take, cb[kk], m2)
        w2 = jnp.where(take, cs[kk], w2)
    e1 = best * EXPERTS_PER_GROUP + i1
    e2 = best * EXPERTS_PER_GROUP + i2
    wsum = w1 + w2
    erow = lax.broadcasted_iota(jnp.int32, (N_EXPERTS, tm), 0)
    oh1 = (erow == e1).astype(f32)
    oh2 = (erow == e2).astype(f32)
    oh = oh1 + oh2
    rank = _dot(oh.astype(bf16), tri_ref[...]) + carry_ref[:, 0:1]
    carry_ref[...] = carry_ref[...] + jnp.sum(oh, axis=1, keepdims=True)
    r1 = jnp.sum(oh1 * rank, axis=0, keepdims=True)
    r2 = jnp.sum(oh2 * rank, axis=0, keepdims=True)
    info_ref[...] = jnp.concatenate(
        [e1.astype(f32), e2.astype(f32), w1 / wsum, w2 / wsum, r1, r2,
         jnp.zeros((2, tm), f32)], axis=0)
    cnt_ref[...] = carry_ref[...]


def _router(x_all, n_rows, g, mods, rw_t, rb_col, tiles_per_mod):
    d = x_all.shape[1]
    tm = ROW_TILE
    tri = jnp.asarray(np.triu(np.ones((tm, tm), np.float32), k=1), bf16)
    mod_spec = lambda k: pl.BlockSpec((None, 1, d), lambda i: (i // tiles_per_mod, 0, k))
    return pl.pallas_call(
        _router_kernel,
        grid=(n_rows // tm,),
        in_specs=[pl.BlockSpec((tm, d), lambda i: (i, 0)),
                  pl.BlockSpec((1, d), lambda i: (0, 0)),
                  mod_spec(3), mod_spec(4),
                  pl.BlockSpec((2 * N_EXPERTS, d), lambda i: (0, 0)),
                  pl.BlockSpec((N_EXPERTS, 1), lambda i: (0, 0)),
                  pl.BlockSpec((tm, tm), lambda i: (0, 0))],
        out_specs=[pl.BlockSpec((8, tm), lambda i: (0, i)),
                   pl.BlockSpec((N_EXPERTS, 128), lambda i: (0, 0)),
                   pl.BlockSpec((tm, d), lambda i: (i, 0))],
        out_shape=[jax.ShapeDtypeStruct((8, n_rows), f32),
                   jax.ShapeDtypeStruct((N_EXPERTS, 128), f32),
                   jax.ShapeDtypeStruct((n_rows, d), f32)],
        scratch_shapes=[pltpu.VMEM((N_EXPERTS, 128), f32)],
        compiler_params=_cparams("arbitrary"),
        name="moe_router",
    )(x_all, g, mods, mods, rw_t, rb_col, tri)


def _expert_kernel(sched_ref, nu_ref, src_ref, h2_hbm, wg_hbm, wu_hbm, wd_hbm, y_ref,
                   wg_buf, wu_buf, wd_buf, x_buf, sem, xsem, *, layer):
    i = pl.program_id(0)
    slot = sched_ref[2, i]
    n_used = nu_ref[0, N_EXPERTS]
    tm = y_ref.shape[0]
    xslot = i % 2

    def gather_rows(tile, s):
        for r in range(tm):
            pltpu.make_async_copy(h2_hbm.at[pl.ds(src_ref[tile * tm + r], 1)], x_buf.at[s].at[pl.ds(r, 1)],
                                  xsem.at[s]).start()

    @pl.when(i == 0)
    def _():
        gather_rows(0, 0)

    @pl.when(i + 1 < n_used)
    def _():
        gather_rows(i + 1, 1 - xslot)

    @pl.when(i < n_used)
    def _():
        pltpu.make_async_copy(h2_hbm.at[pl.ds(0, tm)], x_buf.at[xslot], xsem.at[xslot]).wait()

    def weight_copies(e, s):
        return [pltpu.make_async_copy(hbm.at[layer, e], buf.at[s], sem.at[s])
                for hbm, buf in ((wg_hbm, wg_buf), (wu_hbm, wu_buf), (wd_hbm, wd_buf))]

    @pl.when(i == 0)
    def _():
        for cp in weight_copies(sched_ref[0, 0], 0):
            cp.start()

    @pl.when((sched_ref[1, i] == 1) & (i < n_used))
    def _():
        for cp in weight_copies(sched_ref[0, i], slot):
            cp.wait()
        nxt = sched_ref[3, i]

        @pl.when(nxt >= 0)
        def _():
            for cp in weight_copies(nxt, 1 - slot):
                cp.start()

    @pl.when(i < n_used)
    def _():
        x = x_buf[xslot]
        a = _dot(x, wg_buf[slot])
        act = (a * jax.nn.sigmoid(a)) * _dot(x, wu_buf[slot])
        y_ref[...] = _dot(act, wd_buf[slot])

    @pl.when(i >= n_used)
    def _():
        y_ref[...] = jnp.zeros_like(y_ref)


def _expert_ffn(sched, n_used, src, h2, p_rows, layer, wg, wu, wd):
    d = h2.shape[1]
    de = wg.shape[3]
    tm = MOE_TILE
    return pl.pallas_call(
        functools.partial(_expert_kernel, layer=layer),
        grid_spec=pltpu.PrefetchScalarGridSpec(
            num_scalar_prefetch=3,
            grid=(p_rows // tm,),
            in_specs=[pl.BlockSpec(memory_space=pl.ANY), pl.BlockSpec(memory_space=pl.ANY),
                      pl.BlockSpec(memory_space=pl.ANY), pl.BlockSpec(memory_space=pl.ANY)],
            out_specs=pl.BlockSpec((tm, d), lambda i, sc, nu, src: (i, 0)),
            scratch_shapes=[pltpu.VMEM((2, d, de), f32), pltpu.VMEM((2, d, de), f32),
                            pltpu.VMEM((2, de, d), f32), pltpu.VMEM((2, tm, d), f32),
                            pltpu.SemaphoreType.DMA((2,)), pltpu.SemaphoreType.DMA((2,))],
        ),
        out_shape=jax.ShapeDtypeStruct((p_rows, d), f32),
        compiler_params=_cparams("arbitrary", vmem_limit=EXPERT_VMEM_LIMIT),
        name="moe_experts",
    )(sched, n_used, src, h2, wg, wu, wd)


def _combine_kernel(pos_ref, x_ref, w_ref, gate_ref, fg_ref, y_ref, o_ref, ybuf_ref, sem, *, final_norm):
    i = pl.program_id(0)
    tm = x_ref.shape[0]
    slot = i % 2

    def gather(step, s):
        for r in range(tm):
            for k in range(2):
                src = pos_ref[k, step * tm + r]
                pltpu.make_async_copy(y_ref.at[pl.ds(src, 1)], ybuf_ref.at[s, k].at[pl.ds(r, 1)],
                                      sem.at[s]).start(priority=k)

    @pl.when(i == 0)
    def _():
        gather(0, 0)

    @pl.when(i + 1 < pl.num_programs(0))
    def _():
        gather(i + 1, 1 - slot)

    for k in range(2):
        pltpu.make_async_copy(y_ref.at[pl.ds(0, tm)], ybuf_ref.at[slot, k], sem.at[slot]).wait()
    w = w_ref[...]
    moe = w[:, 0:1] * ybuf_ref[slot, 0] + w[:, 1:2] * ybuf_ref[slot, 1]
    x = x_ref[...] + gate_ref[...] * moe
    if final_norm:
        x = x * lax.rsqrt(jnp.mean(x * x, axis=-1, keepdims=True) + EPS) * fg_ref[...]
    o_ref[...] = x


def _combine(pos, x_all, n_rows, w_tok, mods, final_g, y, tiles_per_mod, final_norm):
    d = x_all.shape[1]
    tm = MOE_TILE
    tpm = tiles_per_mod * (ROW_TILE // tm)
    kern = functools.partial(_combine_kernel, final_norm=final_norm)
    return pl.pallas_call(
        kern,
        grid_spec=pltpu.PrefetchScalarGridSpec(
            num_scalar_prefetch=1,
            grid=(n_rows // tm,),
            in_specs=[pl.BlockSpec((tm, d), lambda i, pos: (i, 0)),
                      pl.BlockSpec((tm, 2), lambda i, pos: (i, 0)),
                      pl.BlockSpec((None, 1, d), lambda i, pos: (i // tpm, 0, 5)),
                      pl.BlockSpec((1, d), lambda i, pos: (0, 0)),
                      pl.BlockSpec(memory_space=pl.ANY)],
            out_specs=pl.BlockSpec((tm, d), lambda i, pos: (i, 0)),
            scratch_shapes=[pltpu.VMEM((2, 2, tm, d), f32), pltpu.SemaphoreType.DMA((2,))],
        ),
        out_shape=jax.ShapeDtypeStruct((n_rows, d), f32),
        compiler_params=_cparams("arbitrary"),
        name="moe_combine",
    )(pos, x_all, w_tok, mods, final_g, y)


def _plan_kernel(info_ref, cnt_ref, pos_ref, sched_ref, misc_ref, *, tm):
    cnt = cnt_ref[...]
    padded = jnp.floor((cnt + (tm - 1)) * (1.0 / tm)) * tm
    starts, ends = [], []
    run = jnp.zeros((1, 128), f32)
    for e in range(N_EXPERTS):
        starts.append(run)
        run = run + padded[e:e + 1]
        ends.append(run)
    total = run
    e1, e2 = info_ref[0:1, :], info_ref[1:2, :]
    p1, p2 = info_ref[4:5, :], info_ref[5:6, :]
    for e in range(N_EXPERTS):
        s_e = starts[e][:, 0:1]
        p1 = p1 + jnp.where(e1 == e, s_e, 0.0)
        p2 = p2 + jnp.where(e2 == e, s_e, 0.0)
    pos_ref[...] = jnp.concatenate([p1, p2], axis=0).astype(jnp.int32)
    slot_e, next_e = [], [None] * N_EXPERTS
    ordinal = jnp.full((1, 128), -1.0, f32)
    for e in range(N_EXPERTS):
        ordinal = ordinal + jnp.where(cnt[e:e + 1] > 0, 1.0, 0.0)
        slot_e.append(ordinal - 2.0 * jnp.floor(ordinal * 0.5))
    nxt = jnp.full((1, 128), -1.0, f32)
    for e in reversed(range(N_EXPERTS)):
        next_e[e] = nxt
        nxt = jnp.where(cnt[e:e + 1] > 0, float(e), nxt)
    n_t = sched_ref.shape[1]
    t0 = lax.broadcasted_iota(jnp.int32, (1, n_t), 1).astype(f32) * tm
    te = jnp.zeros((1, n_t), f32)
    for e in range(N_EXPERTS):
        te = te + jnp.where(ends[e][:, 0:1] <= t0, 1.0, 0.0)
    te = jnp.minimum(te, N_EXPERTS - 1.0)
    first = jnp.zeros((1, n_t), f32)
    slot = jnp.zeros((1, n_t), f32)
    nxt_t = jnp.zeros((1, n_t), f32)
    for e in range(N_EXPERTS):
        here = te == e
        first = first + jnp.where(here & (t0 == starts[e][:, 0:1]), 1.0, 0.0)
        slot = slot + jnp.where(here, slot_e[e][:, 0:1], 0.0)
        nxt_t = nxt_t + jnp.where(here, next_e[e][:, 0:1], 0.0)
    first = jnp.where(t0 < total[:, 0:1], first, 0.0)
    sched_ref[...] = jnp.concatenate([te, first, slot, nxt_t, jnp.zeros((4, n_t), f32)],
                                     axis=0).astype(jnp.int32)
    lane = lax.broadcasted_iota(jnp.int32, (1, 128), 1)
    zrow = jnp.where(lane == N_EXPERTS, total * (1.0 / tm), 0.0)
    for e in range(N_EXPERTS):
        zrow = zrow + jnp.where(lane == e, starts[e] + cnt[e:e + 1], 0.0)
    misc_ref[...] = jnp.concatenate([zrow, jnp.zeros((7, 128), f32)], axis=0).astype(jnp.int32)


def _plan(info, counts, n_rows, n_tiles):
    n_t = ((n_tiles + 127) // 128) * 128
    full = lambda shape: pl.BlockSpec(shape, lambda: (0,) * len(shape))
    return pl.pallas_call(
        functools.partial(_plan_kernel, tm=MOE_TILE),
        in_specs=[full((8, n_rows)), full((N_EXPERTS, 128))],
        out_specs=[full((2, n_rows)), full((8, n_t)), full((8, 128))],
        out_shape=[jax.ShapeDtypeStruct((2, n_rows), jnp.int32),
                   jax.ShapeDtypeStruct((8, n_t), jnp.int32),
                   jax.ShapeDtypeStruct((8, 128), jnp.int32)],
        name="moe_plan",
    )(info, counts)


def _moe(x_all, n_rows, norm_g, mods, tiles_per_mod, rw_t, rb_col, layer, wg, wu, wd, final_g, final_norm):
    info, counts, h2 = _router(x_all, n_rows, norm_g, mods, rw_t, rb_col, tiles_per_mod)
    tm = MOE_TILE
    p_rows = 2 * n_rows + N_EXPERTS * tm
    pos, sched, misc = _plan(info, counts, n_rows, p_rows // tm)
    token = jnp.arange(n_rows, dtype=jnp.int32)
    src = jnp.zeros((p_rows,), jnp.int32).at[pos.reshape(-1)].set(jnp.concatenate([token, token]))
    y = _expert_ffn(sched, misc, src, h2, p_rows, layer, wg, wu, wd)
    w_tok = jnp.transpose(info[2:4])
    return _combine(pos, x_all, n_rows, w_tok, mods, final_g, y, tiles_per_mod, final_norm)


def _rope_tables(seq_len, n_batch, n_ctx_rows):
    rows = seq_len // GRID_W
    row = jnp.repeat(jnp.arange(rows), GRID_W)
    col = jnp.tile(jnp.arange(GRID_W), rows)
    nf = HEAD_DIM // 4
    inv_freq = ROPE_BASE ** (-jnp.arange(nf, dtype=f32) / nf)
    ang_r = row[:, None].astype(f32) * inv_freq
    ang_c = col[:, None].astype(f32) * inv_freq
    cos64 = jnp.concatenate([jnp.cos(ang_r)] * 2 + [jnp.cos(ang_c)] * 2, axis=-1)
    sin64 = jnp.concatenate([-jnp.sin(ang_r), jnp.sin(ang_r), -jnp.sin(ang_c), jnp.sin(ang_c)], axis=-1)
    cos_t = jnp.tile(jnp.concatenate([cos64, cos64], axis=-1), (n_batch, 1))
    sin_t = jnp.tile(jnp.concatenate([sin64, sin64], axis=-1), (n_batch, 1))
    cos_t = jnp.concatenate([cos_t, jnp.ones((n_ctx_rows, 128), f32)], axis=0)
    sin_t = jnp.concatenate([sin_t, jnp.zeros((n_ctx_rows, 128), f32)], axis=0)
    return cos_t, sin_t


def kernel(x, c, ctx, c_ctx, w_mod, b_mod, norm1_g, norm2_g, final_norm_g, w_in, a_sink,
           p_w, p_scale, c_ln_g, c_ln_b, c_ws, c_bs, g_w2, g_b, g_norm_g,
           w_branch, w_gate, w_out, router_w, router_b, e_gate, e_up, e_down):
    n_batch, seq_len, d = x.shape
    n_ctx = ctx.shape[1]
    depth = w_mod.shape[0]
    n_lat = n_batch * seq_len
    n_all = n_lat + n_batch * n_ctx
    tiles_per_mod = seq_len // ROW_TILE

    cond = jnp.zeros((8, d), f32).at[0:n_batch].set(c).at[n_batch].set(c_ctx)
    mods_all = _modulation(cond, w_mod, b_mod)
    cos_t, sin_t = _rope_tables(seq_len, n_batch, n_batch * n_ctx)
    rw_f = jnp.transpose(router_w)
    rw_hi = rw_f.astype(bf16)
    rw_t = jnp.concatenate([rw_hi, (rw_f - rw_hi.astype(f32)).astype(bf16)], axis=0)
    rb_col = router_b.reshape(N_EXPERTS, 1)
    final_g = final_norm_g.reshape(1, d)

    w_in_b = jnp.pad(w_in, ((0, 0), (0, 0), (0, 96))).astype(bf16)
    w_gate_b = w_gate.astype(bf16)
    w_branch_b = w_branch.astype(bf16)
    w_out_b = w_out.astype(bf16)

    n_lat_tiles = n_lat // ROW_TILE
    xa, xb, b_tile0 = x.reshape(n_lat, d), ctx.reshape(n_batch * n_ctx, d), 0
    for l in range(depth):
        last = l == depth - 1
        n_out = n_lat if last else n_all
        mods = mods_all[l].reshape(8, 1, 6 * d)
        w2p = jnp.zeros((128, 512), f32)
        w2p = w2p.at[0:GLA_RANK, 0:256].set(g_w2[l, 0]).at[GLA_RANK:2 * GLA_RANK, 256:512].set(g_w2[l, 1])
        w2_hi = w2p.astype(bf16)
        w2_split = jnp.concatenate([w2_hi, w2_hi, (w2p - w2_hi.astype(f32)).astype(bf16)], axis=0)
        bs_b = jnp.broadcast_to(c_bs[l][:, :, None], (4, SG_CHUNK, 128))
        h, qkv, z_pool, s_br, z_gla, gates = _in_proj(
            xa, xb, n_lat_tiles, b_tile0, n_all, l, norm1_g[l].reshape(1, d), mods, cos_t, sin_t,
            w_in_b, w2_split, g_b[l].reshape(1, 512), c_ln_g[l].reshape(1, 512),
            c_ln_b[l].reshape(1, 512), c_ws[l].astype(bf16), bs_b, tiles_per_mod)
        a_br = _window_attention(qkv, a_sink[l], n_batch, seq_len, n_ctx)
        pw = p_w[l].astype(bf16)
        ps = p_scale[l].reshape(1, 512)
        p_br = _multi_scale_pool(z_pool, 0, n_lat, seq_len, ROW_TILE, pw, ps)
        o_f, o_b = _gla_scan(z_gla, gates, n_batch, seq_len, n_ctx)
        if last:
            a_c, p_c = a_br, p_br
        else:
            a_c = _context_attention(qkv, a_sink[l], n_batch, n_lat, n_ctx)
            p_c = _multi_scale_pool(z_pool, n_lat, n_batch * n_ctx, n_ctx, n_ctx, pw, ps)
        acc = _gated_merge(h, a_br, a_c, p_br, p_c, s_br, o_f, o_b, z_gla, g_norm_g[l].reshape(1, 512),
                           n_out, n_lat_tiles, l, w_gate_b, w_branch_b)
        x_all = _out_proj_residual(acc, l, w_out_b, xa, xb, n_lat_tiles, b_tile0, mods, n_out,
                                   tiles_per_mod)
        x_all = _moe(x_all, n_out, norm2_g[l].reshape(1, d), mods, tiles_per_mod, rw_t, rb_col,
                     l, e_gate, e_up, e_down, final_g, last)
        xa, xb, b_tile0 = x_all, x_all, n_lat_tiles
    return x_all.reshape(n_batch, seq_len, d)
```

```python
import functools

import numpy as np
import jax
import jax.numpy as jnp
from jax import lax
from jax.experimental import pallas as pl
from jax.experimental.pallas import tpu as pltpu

f32 = jnp.float32
bf16 = jnp.bfloat16

EPS = 1e-6
GRID_W = 64
ROPE_BASE = 10000.0

HEAD_DIM = 64
N_Q_HEADS = 8
N_KV_HEADS = 2
Q_PER_KV = N_Q_HEADS // N_KV_HEADS
ATT_BLOCK = 128
POOL_WINDOWS = (2, 4, 8, 16)
POOL_HALO = 8
SG_CHUNK = 128
GLA_HEADS = 4
GLA_DK = 64
GLA_DV = 128
GLA_RANK = 16
GLA_TAU = 16.0
GLA_CHUNK = 64
N_EXPERTS = 16
EXPERTS_PER_GROUP = 4
N_GROUPS = 4

ROW_TILE = 512
GLA_BLOCK = 256
MOE_TILE = 256
V7X_VMEM_BYTES = 64 * 1024 * 1024
VMEM_LIMIT = V7X_VMEM_BYTES * 7 // 8
EXPERT_VMEM_LIMIT = V7X_VMEM_BYTES * 31 // 32


def _cparams(*sem, vmem_limit=VMEM_LIMIT):
    return pltpu.CompilerParams(dimension_semantics=sem, vmem_limit_bytes=vmem_limit)


def _dot(a, b, dims=(((1,), (0,)), ((), ())), precision=None):
    return lax.dot_general(a, b, dims, precision=precision, preferred_element_type=f32)


_NT = (((1,), (1,)), ((), ()))
_TN = (((0,), (0,)), ((), ()))


def _mod_kernel(c_ref, w_ref, b_ref, o_ref):
    c = c_ref[...]
    a = c * jax.nn.sigmoid(c)
    w = w_ref[...]
    a_hi = a.astype(bf16)
    a_lo = (a - a_hi.astype(f32)).astype(bf16)
    w_hi = w.astype(bf16)
    w_lo = (w - w_hi.astype(f32)).astype(bf16)
    n = a.shape[0]
    both = _dot(jnp.concatenate([a_hi, a_lo], axis=0), w_hi)
    o_ref[...] = both[0:n] + both[n:2 * n] + _dot(a_hi, w_lo) + b_ref[...]


def _modulation(cond, w_mod, b_mod):
    n_layers, d, d6 = w_mod.shape
    tn = 1024
    return pl.pallas_call(
        _mod_kernel,
        grid=(n_layers, d6 // tn),
        in_specs=[
            pl.BlockSpec((8, d), lambda l, j: (0, 0)),
            pl.BlockSpec((None, d, tn), lambda l, j: (l, 0, j)),
            pl.BlockSpec((None, 1, tn), lambda l, j: (l, 0, j)),
        ],
        out_specs=pl.BlockSpec((None, 8, tn), lambda l, j: (l, 0, j)),
        out_shape=jax.ShapeDtypeStruct((n_layers, 8, d6), f32),
        compiler_params=_cparams("arbitrary", "arbitrary"),
        name="modulation",
    )(cond, w_mod, b_mod.reshape(n_layers, 1, d6))


def _norm_mod(x, g, shift, scale):
    y = x * lax.rsqrt(jnp.mean(x * x, axis=-1, keepdims=True) + EPS)
    return (y * g) * (1.0 + scale) + shift


def _two_source_rows(xa_ref, xb_ref, n_a_tiles):
    return jnp.where(pl.program_id(0) < n_a_tiles, xa_ref[...], xb_ref[...])


def _two_source_specs(tm, d, n_a_tiles, b_tile0):
    return [pl.BlockSpec((tm, d), lambda i: (jnp.minimum(i, n_a_tiles - 1), 0)),
            pl.BlockSpec((tm, d), lambda i: (b_tile0 + jnp.maximum(i - n_a_tiles, 0), 0),
                         pipeline_mode=pl.Buffered(1))]


def _spatial_gate_rows(uv, lg_ref, lb_ref, ws_ref, bs_ref, o_ref):
    a = jax.nn.gelu(uv)
    u = a[:, 0:512]
    v = a[:, 512:1024]
    mu = jnp.mean(v, axis=-1, keepdims=True)
    var = jnp.mean(jnp.square(v - mu), axis=-1, keepdims=True)
    vn = ((v - mu) * lax.rsqrt(var + EPS) * lg_ref[...] + lb_ref[...]).astype(bf16)
    for c in range(uv.shape[0] // SG_CHUNK):
        rows = slice(c * SG_CHUNK, (c + 1) * SG_CHUNK)
        for g in range(4):
            cols = slice(g * 128, (g + 1) * 128)
            mixed = _dot(ws_ref[g], vn[rows, cols]) + bs_ref[g]
            o_ref[rows, cols] = (u[rows, cols] * mixed).astype(o_ref.dtype)


def _in_proj_kernel(xa_ref, xb_ref, g_ref, sh_ref, sc_ref, cos_ref, sin_ref, w_ref,
                    w2_ref, gb_ref, lg_ref, lb_ref, ws_ref, bs_ref,
                    h_ref, qkv_ref, pool_ref, sg_ref, gla_ref, gate_ref, *, n_a_tiles):
    x = _two_source_rows(xa_ref, xb_ref, n_a_tiles)
    hb = _norm_mod(x, g_ref[...], sh_ref[...], sc_ref[...]).astype(bf16)
    h_ref[...] = hb

    cos = cos_ref[...]
    sin = sin_ref[...]
    lane = lax.broadcasted_iota(jnp.int32, cos.shape, 1)
    first_half = (lane % 32) < 16

    def rope(z):
        rot = jnp.where(first_half, pltpu.roll(z, 112, 1), pltpu.roll(z, 16, 1))
        return z * cos + rot * sin

    zq = _dot(hb, w_ref[:, 0:512])
    for c in range(4):
        qkv_ref[:, c * 128:(c + 1) * 128] = (
            rope(zq[:, c * 128:(c + 1) * 128]) * (HEAD_DIM ** -0.5)).astype(bf16)
    zkv = _dot(hb, w_ref[:, 512:768])
    qkv_ref[:, 512:640] = rope(zkv[:, 0:128]).astype(bf16)
    qkv_ref[:, 640:768] = zkv[:, 128:256].astype(bf16)
    pool_ref[...] = _dot(hb, w_ref[:, 768:1280])
    _spatial_gate_rows(_dot(hb, w_ref[:, 1280:2304]), lg_ref, lb_ref, ws_ref, bs_ref, sg_ref)
    gla_ref[...] = _dot(hb, w_ref[:, 2304:3840])
    low_rank = _dot(hb, w_ref[:, 3840:3968])
    lr_hi = low_rank.astype(bf16)
    lr_lo = (low_rank - lr_hi.astype(f32)).astype(bf16)
    logit = _dot(jnp.concatenate([lr_hi, lr_lo, lr_hi], axis=1), w2_ref[...]) + gb_ref[...]
    gate_ref[...] = _log_sigmoid(logit) / GLA_TAU


def _log_sigmoid(x):
    return jnp.minimum(x, 0.0) - jnp.log(1.0 + jnp.exp(-jnp.abs(x)))


def _in_proj(xa, xb, n_a_tiles, b_tile0, rows, layer, g, mods, cos_t, sin_t, w_pad, w2p, gbias,
             ln_g, ln_b, ws, bs_b, tiles_per_mod):
    d = xa.shape[1]
    n_w = w_pad.shape[2]
    tm = ROW_TILE
    mod_spec = lambda k: pl.BlockSpec((None, 1, d), lambda i: (i // tiles_per_mod, 0, k))
    row_spec = lambda w: pl.BlockSpec((tm, w), lambda i: (i, 0))
    return pl.pallas_call(
        functools.partial(_in_proj_kernel, n_a_tiles=n_a_tiles),
        grid=(rows // tm,),
        in_specs=_two_source_specs(tm, d, n_a_tiles, b_tile0) + [
            pl.BlockSpec((1, d), lambda i: (0, 0)),
            mod_spec(0), mod_spec(1),
            row_spec(128), row_spec(128),
            pl.BlockSpec((None, d, n_w), lambda i: (layer, 0, 0), pipeline_mode=pl.Buffered(1)),
            pl.BlockSpec((384, 512), lambda i: (0, 0)),
            pl.BlockSpec((1, 512), lambda i: (0, 0)),
            pl.BlockSpec((1, 512), lambda i: (0, 0)),
            pl.BlockSpec((1, 512), lambda i: (0, 0)),
            pl.BlockSpec((4, 128, 128), lambda i: (0, 0, 0)),
            pl.BlockSpec((4, 128, 128), lambda i: (0, 0, 0)),
        ],
        out_specs=[row_spec(d), row_spec(768), row_spec(512), row_spec(512), row_spec(1536),
                   row_spec(512)],
        out_shape=[
            jax.ShapeDtypeStruct((rows, d), bf16),
            jax.ShapeDtypeStruct((rows, 768), bf16),
            jax.ShapeDtypeStruct((rows, 512), f32),
            jax.ShapeDtypeStruct((rows, 512), bf16),
            jax.ShapeDtypeStruct((rows, 1536), f32),
            jax.ShapeDtypeStruct((rows, 512), f32),
        ],
        compiler_params=_cparams("arbitrary"),
        name="in_proj",
    )(xa, xb, g, mods, mods, cos_t, sin_t, w_pad, w2p, gbias, ln_g, ln_b, ws, bs_b)


def _sink_column(sink_ref, kvh, rows_per_head):
    r = lax.broadcasted_iota(jnp.int32, (Q_PER_KV * rows_per_head, 1), 0) // rows_per_head
    col = jnp.full(r.shape, sink_ref[kvh * Q_PER_KV], f32)
    for g in range(1, Q_PER_KV):
        col = jnp.where(r == g, sink_ref[kvh * Q_PER_KV + g], col)
    return col


def _win_attn_kernel(sink_ref, main_ref, prev_ref, next_ref, ctx_ref, o_ref, *, tiles_per_seq, seq_len):
    i = pl.program_id(0)
    tq = main_ref.shape[0]
    n_sub = tq // ATT_BLOCK
    blk0 = (i % tiles_per_seq) * n_sub
    blocks_per_seq = seq_len // ATT_BLOCK
    r = lax.broadcasted_iota(jnp.int32, (Q_PER_KV * ATT_BLOCK, ATT_BLOCK), 0) % ATT_BLOCK
    j = lax.broadcasted_iota(jnp.int32, (Q_PER_KV * ATT_BLOCK, ATT_BLOCK), 1)
    keep_prev = j >= r
    keep_next = j <= r
    n_ctx = ctx_ref.shape[0]

    def with_ones(v):
        return jnp.concatenate([v, jnp.ones_like(v)], axis=1)

    for kvh in range(N_KV_HEADS):
        kc = 512 + kvh * HEAD_DIM
        vc = 640 + kvh * HEAD_DIM
        k_all = jnp.concatenate([prev_ref[:, kc:kc + HEAD_DIM], main_ref[:, kc:kc + HEAD_DIM],
                                 next_ref[:, kc:kc + HEAD_DIM]], axis=0)
        v_all = with_ones(jnp.concatenate([prev_ref[:, vc:vc + HEAD_DIM], main_ref[:, vc:vc + HEAD_DIM],
                                           next_ref[:, vc:vc + HEAD_DIM]], axis=0))
        k_ctx = ctx_ref[:, kc:kc + HEAD_DIM]
        v_ctx = with_ones(ctx_ref[:, vc:vc + HEAD_DIM])
        sink = _sink_column(sink_ref, kvh, ATT_BLOCK)
        for sb in range(n_sub):
            rows = slice(sb * ATT_BLOCK, (sb + 1) * ATT_BLOCK)
            q = jnp.concatenate(
                [main_ref[rows, (kvh * Q_PER_KV + g) * HEAD_DIM:(kvh * Q_PER_KV + g + 1) * HEAD_DIM]
                 for g in range(Q_PER_KV)], axis=0)
            band = slice(sb * ATT_BLOCK, (sb + 3) * ATT_BLOCK)
            s_band = _dot(q, k_all[band], _NT)
            s_ctx = _dot(q, k_ctx, _NT)
            blk = blk0 + sb
            parts = [jnp.where(keep_prev & (blk >= 1), s_band[:, 0:ATT_BLOCK], -1e30),
                     s_band[:, ATT_BLOCK:2 * ATT_BLOCK],
                     jnp.where(keep_next & (blk <= blocks_per_seq - 2), s_band[:, 2 * ATT_BLOCK:], -1e30)]
            parts += [s_ctx[:, c:c + 128] for c in range(0, n_ctx, 128)]
            m = jnp.maximum(sink, jnp.max(functools.reduce(jnp.maximum, parts), axis=-1, keepdims=True))
            p = jnp.concatenate([jnp.exp(x - m).astype(bf16) for x in parts], axis=1)
            o_sum = (_dot(p[:, 0:3 * ATT_BLOCK], v_all[band]) + _dot(p[:, 3 * ATT_BLOCK:], v_ctx))
            denom = jnp.exp(sink - m) + o_sum[:, HEAD_DIM:HEAD_DIM + 1]
            o = o_sum[:, 0:HEAD_DIM] / denom
            for g in range(Q_PER_KV):
                c0 = (kvh * Q_PER_KV + g) * HEAD_DIM
                o_ref[rows, c0:c0 + HEAD_DIM] = o[g * ATT_BLOCK:(g + 1) * ATT_BLOCK].astype(o_ref.dtype)


def _window_attention(qkv, sink, n_batch, seq_len, n_ctx):
    tq = ROW_TILE
    n_lat = n_batch * seq_len
    tiles_per_seq = seq_len // tq
    sub = tq // ATT_BLOCK
    n_blocks = n_lat // ATT_BLOCK
    ctx_blk0 = n_lat // n_ctx
    w = qkv.shape[1]
    kern = functools.partial(_win_attn_kernel, tiles_per_seq=tiles_per_seq, seq_len=seq_len)
    return pl.pallas_call(
        kern,
        grid=(n_lat // tq,),
        in_specs=[
            pl.BlockSpec(memory_space=pltpu.SMEM),
            pl.BlockSpec((tq, w), lambda i: (i, 0)),
            pl.BlockSpec((ATT_BLOCK, w), lambda i: (jnp.maximum(i * sub - 1, 0), 0)),
            pl.BlockSpec((ATT_BLOCK, w), lambda i: (jnp.minimum(i * sub + sub, n_blocks - 1), 0)),
            pl.BlockSpec((n_ctx, w), lambda i: (ctx_blk0 + i // tiles_per_seq, 0)),
        ],
        out_specs=pl.BlockSpec((tq, 512), lambda i: (i, 0)),
        out_shape=jax.ShapeDtypeStruct((n_lat, 512), bf16),
        compiler_params=_cparams("arbitrary"),
        name="window_attention",
    )(sink, qkv, qkv, qkv, qkv)


def _ctx_attn_kernel(sink_ref, qkv_ref, o_ref):
    n = qkv_ref.shape[0]
    for kvh in range(N_KV_HEADS):
        kc = 512 + kvh * HEAD_DIM
        vc = 640 + kvh * HEAD_DIM
        q = jnp.concatenate(
            [qkv_ref[:, (kvh * Q_PER_KV + g) * HEAD_DIM:(kvh * Q_PER_KV + g + 1) * HEAD_DIM]
             for g in range(Q_PER_KV)], axis=0)
        s = _dot(q, qkv_ref[:, kc:kc + HEAD_DIM], _NT)
        sink = _sink_column(sink_ref, kvh, n)
        m = jnp.maximum(sink, jnp.max(s, axis=-1, keepdims=True))
        p = jnp.exp(s - m)
        denom = jnp.exp(sink - m) + jnp.sum(p, axis=-1, keepdims=True)
        o = _dot(p.astype(bf16), qkv_ref[:, vc:vc + HEAD_DIM]) / denom
        for g in range(Q_PER_KV):
            c0 = (kvh * Q_PER_KV + g) * HEAD_DIM
            o_ref[:, c0:c0 + HEAD_DIM] = o[g * n:(g + 1) * n].astype(o_ref.dtype)


def _context_attention(qkv, sink, n_batch, n_lat, n_ctx):
    w = qkv.shape[1]
    blk0 = n_lat // n_ctx
    return pl.pallas_call(
        _ctx_attn_kernel,
        grid=(n_batch,),
        in_specs=[pl.BlockSpec(memory_space=pltpu.SMEM),
                  pl.BlockSpec((n_ctx, w), lambda b: (blk0 + b, 0))],
        out_specs=pl.BlockSpec((n_ctx, 512), lambda b: (b, 0)),
        out_shape=jax.ShapeDtypeStruct((n_batch * n_ctx, 512), bf16),
        compiler_params=_cparams("arbitrary"),
        name="context_attention",
    )(sink, qkv)


def _pool_kernel(main_ref, prev_ref, next_ref, pw_ref, ps_ref, o_ref, xe_ref, *, tiles_per_seq, seq_len):
    i = pl.program_id(0)
    tp = main_ref.shape[0]
    t_in_seq = i % tiles_per_seq
    h = POOL_HALO
    xe_ref[0:h, :] = jnp.where(t_in_seq == 0, 0.0, prev_ref[...])
    xe_ref[h:h + tp, :] = main_ref[...]
    xe_ref[h + tp:2 * h + tp, :] = jnp.where(t_in_seq == tiles_per_seq - 1, 0.0, next_ref[...])
    pos = t_in_seq * tp + lax.broadcasted_iota(jnp.int32, (tp, 1), 0)
    for gi, w in enumerate(POOL_WINDOWS):
        cols = slice(gi * 128, (gi + 1) * 128)
        acc = xe_ref[h - w // 2:h - w // 2 + tp, cols]
        for u in range(-w // 2 + 1, w // 2):
            acc = acc + xe_ref[h + u:h + u + tp, cols]
        lo = jnp.maximum(pos - w // 2, 0)
        hi = jnp.minimum(pos + w // 2, seq_len)
        cnt = (hi - lo).astype(f32)
        pooled = acc / cnt - main_ref[:, cols]
        y = _dot(pooled.astype(bf16), pw_ref[gi])
        o_ref[:, cols] = (y * ps_ref[:, cols]).astype(o_ref.dtype)


def _multi_scale_pool(z, row0, n_rows, seq_len, tp, p_w, p_scale):
    tiles_per_seq = seq_len // tp
    t0 = row0 // tp
    h0 = row0 // POOL_HALO
    hb = tp // POOL_HALO
    n_halo = n_rows // POOL_HALO
    kern = functools.partial(_pool_kernel, tiles_per_seq=tiles_per_seq, seq_len=seq_len)
    return pl.pallas_call(
        kern,
        grid=(n_rows // tp,),
        in_specs=[
            pl.BlockSpec((tp, 512), lambda i: (t0 + i, 0)),
            pl.BlockSpec((POOL_HALO, 512), lambda i: (h0 + jnp.maximum(i * hb - 1, 0), 0)),
            pl.BlockSpec((POOL_HALO, 512), lambda i: (h0 + jnp.minimum((i + 1) * hb, n_halo - 1), 0)),
            pl.BlockSpec((4, 128, 128), lambda i: (0, 0, 0)),
            pl.BlockSpec((1, 512), lambda i: (0, 0)),
        ],
        out_specs=pl.BlockSpec((tp, 512), lambda i: (i, 0)),
        out_shape=jax.ShapeDtypeStruct((n_rows, 512), bf16),
        scratch_shapes=[pltpu.VMEM((tp + 2 * POOL_HALO, 512), f32)],
        compiler_params=_cparams("arbitrary"),
        name="multi_scale_pool",
    )(z, z, z, p_w, p_scale)


GLA_SEGMENTS = (1, 2, 4, 8, 16, 32, 64)


def _gla_tables():
    c = GLA_CHUNK
    i = np.arange(c)[:, None]
    j = np.arange(c)[None, :]
    tri = (j <= i).astype(np.float32)
    mask = np.zeros((2, 7, c, c), np.float32)
    mask[:, 0] = np.eye(c)
    for lv, s in enumerate(GLA_SEGMENTS[:-1]):
        m = ((i // (2 * s)) == (j // (2 * s))) & ((i // s) % 2 == 1) & ((j // s) % 2 == 0)
        mask[0, 1 + lv] = m
        mask[1, 1 + lv] = m.T
    mask = np.tile(mask, (1, 1, 1, GLA_HEADS))
    return tri, mask


def _segment_sums(g, cum):
    c = GLA_CHUNK
    row = lax.broadcasted_iota(jnp.int32, (c, 1), 0)
    zero = jnp.zeros_like(g)
    before = [None] + [pltpu.roll(g, kk, 0) for kk in (1, 2, 3)]
    after = [None] + [pltpu.roll(g, c - kk, 0) for kk in (1, 2, 3)]
    a, r = {1: g}, {1: zero}
    for s in (2, 4):
        pos = row % s
        a_s, r_s = g, zero
        for kk in range(1, s):
            a_s = a_s + jnp.where(pos >= kk, before[kk], 0.0)
            r_s = r_s + jnp.where(pos < s - kk, after[kk], 0.0)
        a[s], r[s] = a_s, r_s
    blocks = [cum[8 * b:8 * b + 8] for b in range(c // 8)]
    last = [cum[8 * b + 7:8 * b + 8] for b in range(c // 8)]
    for s in (8, 16, 32, 64):
        nb = s // 8
        a_blk, r_blk = [], []
        for b in range(c // 8):
            prev_end = (b // nb) * nb - 1
            a_blk.append(blocks[b] - last[prev_end] if prev_end >= 0 else blocks[b])
            r_blk.append(last[(b // nb + 1) * nb - 1] - blocks[b])
        a[s] = jnp.concatenate(a_blk, axis=0)
        r[s] = jnp.concatenate(r_blk, axis=0)
    return a, r


def _gla_chunk(d, q, k, v, g, tri_ref, mask_ref, st_ref):
    c = GLA_CHUNK
    w = g.shape[1]
    g_hi = g.astype(bf16)
    r1 = g - g_hi.astype(f32)
    g_mid = r1.astype(bf16)
    g_lo = (r1 - g_mid.astype(f32)).astype(bf16)
    cum3 = _dot(tri_ref[...], jnp.concatenate([g_hi, g_mid, g_lo], axis=1))
    cum = cum3[:, 0:w] + cum3[:, w:2 * w] + cum3[:, 2 * w:3 * w]
    a, r = _segment_sums(g, cum)
    if d == 0:
        cq = [a[s] for s in GLA_SEGMENTS]
        ck = [r[s] for s in GLA_SEGMENTS]
    else:
        cq = [r[s] + g for s in GLA_SEGMENTS]
        ck = [a[s] - g for s in GLA_SEGMENTS]
    total = cum[c - 1:c]

    q = q * (GLA_DK ** -0.5)
    head_of_lane = lax.broadcasted_iota(jnp.int32, (1, w), 1) // GLA_DK
    vhead_of_lane = lax.broadcasted_iota(jnp.int32, (1, v.shape[1]), 1) // GLA_DV

    def stack_heads(x, lane_head):
        return jnp.concatenate([jnp.where(lane_head == h, x, jnp.zeros_like(x))
                                for h in range(GLA_HEADS)], axis=0)

    kb = k.astype(bf16)
    k_plain = stack_heads(kb, head_of_lane)
    q01 = jnp.concatenate([q.astype(bf16), (q * jnp.exp(cq[0])).astype(bf16)], axis=0)
    s01 = _dot(q01, k_plain, _NT)
    att = mask_ref[d, 0] * s01[0:c] + mask_ref[d, 1] * s01[c:2 * c]
    for lv in range(1, 6):
        qs = (q * jnp.exp(cq[lv])).astype(bf16)
        ks = stack_heads((k * jnp.exp(ck[lv])).astype(bf16), head_of_lane)
        att = att + mask_ref[d, 1 + lv] * _dot(qs, ks, _NT)
    v_heads = stack_heads(v.astype(bf16), vhead_of_lane)
    st = st_ref[d]
    q_state = (q * jnp.exp(cq[6])).astype(bf16)
    o = _dot(att.astype(bf16), v_heads) + _dot(q_state, st.astype(bf16), _NT)
    k_state = stack_heads((k * jnp.exp(ck[6])).astype(bf16), head_of_lane)
    st_ref[d] = st * jnp.exp(total) + _dot(v_heads, k_state, _TN)
    return o


def _gla_kernel(qf_ref, kf_ref, vf_ref, gf_ref, qb_ref, kb_ref, vb_ref, gb_ref,
                tri_ref, mask_ref, of_ref, ob_ref, st_ref):
    @pl.when(pl.program_id(1) == 0)
    def _():
        st_ref[...] = jnp.zeros_like(st_ref)

    n_chunks = qf_ref.shape[0] // GLA_CHUNK
    for ci in range(n_chunks):
        rf = slice(ci * GLA_CHUNK, (ci + 1) * GLA_CHUNK)
        of_ref[rf, :] = _gla_chunk(0, qf_ref[rf, :], kf_ref[rf, :], vf_ref[rf, :], gf_ref[rf, :],
                                   tri_ref, mask_ref, st_ref)
        cb = n_chunks - 1 - ci
        rb = slice(cb * GLA_CHUNK, (cb + 1) * GLA_CHUNK)
        ob_ref[rb, :] = _gla_chunk(1, qb_ref[rb, :], kb_ref[rb, :], vb_ref[rb, :], gb_ref[rb, :],
                                   tri_ref, mask_ref, st_ref)


def _gla_scan(z_gla, gates, n_batch, seq_len, n_ctx):
    rows = z_gla.shape[0]
    tb = GLA_BLOCK
    assert n_ctx == tb
    lat_blocks = seq_len // tb
    ctx_blk0 = n_batch * lat_blocks
    tri, mask = _gla_tables()

    def fwd_row(b, s):
        return jnp.where(s == 0, ctx_blk0 + b, b * lat_blocks + s - 1)

    def bwd_row(b, s):
        return jnp.where(s == 0, ctx_blk0 + b, b * lat_blocks + lat_blocks - s)

    def specs(row, direction):
        return [
            pl.BlockSpec((tb, 256), lambda b, s: (row(b, s), 0)),
            pl.BlockSpec((tb, 256), lambda b, s: (row(b, s), 1)),
            pl.BlockSpec((tb, 512), lambda b, s: (row(b, s), 1)),
            pl.BlockSpec((tb, 256), lambda b, s: (row(b, s), direction)),
        ]

    const = lambda shape: pl.BlockSpec(shape, lambda b, s: (0,) * len(shape))
    return pl.pallas_call(
        _gla_kernel,
        grid=(n_batch, 1 + lat_blocks),
        in_specs=specs(fwd_row, 0) + specs(bwd_row, 1) + [
            const(tri.shape), const(mask.shape)],
        out_specs=[pl.BlockSpec((tb, 512), lambda b, s: (fwd_row(b, s), 0)),
                   pl.BlockSpec((tb, 512), lambda b, s: (bwd_row(b, s), 0))],
        out_shape=[jax.ShapeDtypeStruct((rows, 512), f32)] * 2,
        scratch_shapes=[pltpu.VMEM((2, GLA_HEADS * GLA_DV, GLA_HEADS * GLA_DK), f32)],
        compiler_params=_cparams("arbitrary", "arbitrary"),
        name="gla_scan",
    )(z_gla, z_gla, z_gla, gates, z_gla, z_gla, z_gla, gates,
      jnp.asarray(tri, bf16), jnp.asarray(mask, f32))


def _merge_kernel(h_ref, a_ref, ac_ref, p_ref, pc_ref, s_ref, of_ref, ob_ref, r_ref, ng_ref, wg_ref, wb_ref,
                  o_ref, g_ref, *, n_lat_tiles):
    @pl.when(pl.program_id(1) == 0)
    def _():
        o = of_ref[...] + ob_ref[...]
        r = r_ref[...]
        gate = r * jax.nn.sigmoid(r)
        for h in range(GLA_HEADS):
            cols = slice(h * GLA_DV, (h + 1) * GLA_DV)
            oh = o[:, cols]
            oh = oh * lax.rsqrt(jnp.mean(oh * oh, axis=-1, keepdims=True) + EPS)
            g_ref[:, cols] = (oh * ng_ref[:, cols] * gate[:, cols]).astype(g_ref.dtype)

    hb = h_ref[...]
    is_lat = pl.program_id(0) < n_lat_tiles
    branches = (jnp.where(is_lat, a_ref[...], ac_ref[...]), jnp.where(is_lat, p_ref[...], pc_ref[...]),
                s_ref[...], g_ref[...])
    acc = None
    for br, b in enumerate(branches):
        gate = jax.nn.sigmoid(_dot(hb, wg_ref[br]))
        term = gate * _dot(b, wb_ref[br])
        acc = term if acc is None else acc + term
    o_ref[...] = acc.astype(o_ref.dtype)


def _gated_merge(h, a_lat, a_ctx, p_lat, p_ctx, s_br, o_f, o_b, z_gla, norm_g, n_rows, n_lat_tiles, layer,
                 wg, wb):
    d = h.shape[1]
    tm, tn = ROW_TILE, 512
    bspec = pl.BlockSpec((tm, 512), lambda i, j: (i, 0))
    lat_spec = pl.BlockSpec((tm, 512), lambda i, j: (jnp.minimum(i, n_lat_tiles - 1), 0))
    ctx_spec = pl.BlockSpec((tm, 512), lambda i, j: (jnp.maximum(i - n_lat_tiles, 0), 0),
                            pipeline_mode=pl.Buffered(1))
    return pl.pallas_call(
        functools.partial(_merge_kernel, n_lat_tiles=n_lat_tiles),
        grid=(n_rows // tm, d // tn),
        in_specs=[pl.BlockSpec((tm, d), lambda i, j: (i, 0)), lat_spec, ctx_spec, lat_spec, ctx_spec,
                  bspec, bspec, bspec,
                  pl.BlockSpec((tm, 512), lambda i, j: (i, 2)),
                  pl.BlockSpec((1, 512), lambda i, j: (0, 0)),
                  pl.BlockSpec((None, 4, d, tn), lambda i, j: (layer, 0, 0, j)),
                  pl.BlockSpec((None, 4, 512, tn), lambda i, j: (layer, 0, 0, j))],
        out_specs=pl.BlockSpec((tm, tn), lambda i, j: (i, j)),
        out_shape=jax.ShapeDtypeStruct((n_rows, d), bf16),
        scratch_shapes=[pltpu.VMEM((tm, 512), bf16)],
        compiler_params=_cparams("arbitrary", "arbitrary"),
        name="gated_merge",
    )(h, a_lat, a_ctx, p_lat, p_ctx, s_br, o_f, o_b, z_gla, norm_g, wg, wb)


def _out_proj_kernel(acc_ref, wo_ref, xa_ref, xb_ref, gate_ref, o_ref, *, n_a_tiles):
    x = _two_source_rows(xa_ref, xb_ref, n_a_tiles)
    o_ref[...] = x + gate_ref[...] * _dot(acc_ref[...], wo_ref[...])


def _out_proj_residual(acc, layer, wo, xa, xb, n_a_tiles, b_tile0, mods, n_rows, tiles_per_mod):
    d = acc.shape[1]
    tm = ROW_TILE
    return pl.pallas_call(
        functools.partial(_out_proj_kernel, n_a_tiles=n_a_tiles),
        grid=(n_rows // tm,),
        in_specs=[pl.BlockSpec((tm, d), lambda i: (i, 0)),
                  pl.BlockSpec((None, d, d), lambda i: (layer, 0, 0))]
        + _two_source_specs(tm, d, n_a_tiles, b_tile0)
        + [pl.BlockSpec((None, 1, d), lambda i: (i // tiles_per_mod, 0, 2))],
        out_specs=pl.BlockSpec((tm, d), lambda i: (i, 0)),
        out_shape=jax.ShapeDtypeStruct((n_rows, d), f32),
        compiler_params=_cparams("arbitrary"),
        name="out_proj_residual",
    )(acc, wo, xa, xb, mods)


def _router_kernel(x_ref, g_ref, sh_ref, sc_ref, rw_ref, rb_ref, tri_ref, info_ref, cnt_ref, carry_ref):
    i = pl.program_id(0)

    @pl.when(i == 0)
    def _():
        carry_ref[...] = jnp.zeros_like(carry_ref)

    h2 = _norm_mod(x_ref[...], g_ref[...], sh_ref[...], sc_ref[...])
    tm = h2.shape[0]
    h_hi = h2.astype(bf16)
    h_lo = (h2 - h_hi.astype(f32)).astype(bf16)
    part = _dot(rw_ref[...], h_hi, _NT)
    logits = (part[0:N_EXPERTS] + part[N_EXPERTS:2 * N_EXPERTS]
              + _dot(rw_ref[0:N_EXPERTS, :], h_lo, _NT))
    score = jax.nn.sigmoid(logits)
    biased = score + rb_ref[...]
    b = [biased[e:e + 1] for e in range(N_EXPERTS)]
    sc = [score[e:e + 1] for e in range(N_EXPERTS)]

    def top2_sum(v):
        hi01, lo01 = jnp.maximum(v[0], v[1]), jnp.minimum(v[0], v[1])
        hi23, lo23 = jnp.maximum(v[2], v[3]), jnp.minimum(v[2], v[3])
        return jnp.maximum(hi01, hi23) + jnp.maximum(jnp.minimum(hi01, hi23), jnp.maximum(lo01, lo23))

    best = jnp.zeros((1, tm), jnp.int32)
    best_score = top2_sum(b[0:4])
    for gidx in range(1, N_GROUPS):
        gs = top2_sum(b[4 * gidx:4 * gidx + 4])
        take = gs > best_score
        best = jnp.where(take, gidx, best)
        best_score = jnp.where(take, gs, best_score)
    cb, cs = [], []
    for kk in range(EXPERTS_PER_GROUP):
        vb, vs = b[kk], sc[kk]
        for gidx in range(1, N_GROUPS):
            vb = jnp.where(best == gidx, b[4 * gidx + kk], vb)
            vs = jnp.where(best == gidx, sc[4 * gidx + kk], vs)
        cb.append(vb)
        cs.append(vs)
    i1 = jnp.zeros((1, tm), jnp.int32)
    m1, w1 = cb[0], cs[0]
    for kk in range(1, EXPERTS_PER_GROUP):
        take = cb[kk] > m1
        i1 = jnp.where(take, kk, i1)
        m1 = jnp.where(take, cb[kk], m1)
        w1 = jnp.where(take, cs[kk], w1)
    i2 = jnp.where(i1 == 0, 1, 0).astype(jnp.int32)
    m2 = jnp.where(i1 == 0, cb[1], cb[0])
    w2 = jnp.where(i1 == 0, cs[1], cs[0])
    for kk in range(1, EXPERTS_PER_GROUP):
        take = (cb[kk] > m2) & (i1 != kk) & (i2 != kk)
        i2 = jnp.where(take, kk, i2)
        m2 = jnp.where(take, cb[kk], m2)
        w2 = jnp.where(take, cs[kk], w2)
    e1 = best * EXPERTS_PER_GROUP + i1
    e2 = best * EXPERTS_PER_GROUP + i2
    wsum = w1 + w2
    erow = lax.broadcasted_iota(jnp.int32, (N_EXPERTS, tm), 0)
    oh1 = (erow == e1).astype(f32)
    oh2 = (erow == e2).astype(f32)
    oh = oh1 + oh2
    rank = _dot(oh.astype(bf16), tri_ref[...]) + carry_ref[:, 0:1]
    carry_ref[...] = carry_ref[...] + jnp.sum(oh, axis=1, keepdims=True)
    r1 = jnp.sum(oh1 * rank, axis=0, keepdims=True)
    r2 = jnp.sum(oh2 * rank, axis=0, keepdims=True)
    info_ref[...] = jnp.concatenate(
        [e1.astype(f32), e2.astype(f32), w1 / wsum, w2 / wsum, r1, r2,
         jnp.zeros((2, tm), f32)], axis=0)
    cnt_ref[...] = carry_ref[...]


def _router(x_all, n_rows, g, mods, rw_t, rb_col, tiles_per_mod):
    d = x_all.shape[1]
    tm = ROW_TILE
    tri = jnp.asarray(np.triu(np.ones((tm, tm), np.float32), k=1), bf16)
    mod_spec = lambda k: pl.BlockSpec((None, 1, d), lambda i: (i // tiles_per_mod, 0, k))
    return pl.pallas_call(
        _router_kernel,
        grid=(n_rows // tm,),
        in_specs=[pl.BlockSpec((tm, d), lambda i: (i, 0)),
                  pl.BlockSpec((1, d), lambda i: (0, 0)),
                  mod_spec(3), mod_spec(4),
                  pl.BlockSpec((2 * N_EXPERTS, d), lambda i: (0, 0)),
                  pl.BlockSpec((N_EXPERTS, 1), lambda i: (0, 0)),
                  pl.BlockSpec((tm, tm), lambda i: (0, 0))],
        out_specs=[pl.BlockSpec((8, tm), lambda i: (0, i)),
                   pl.BlockSpec((N_EXPERTS, 128), lambda i: (0, 0))],
        out_shape=[jax.ShapeDtypeStruct((8, n_rows), f32),
                   jax.ShapeDtypeStruct((N_EXPERTS, 128), f32)],
        scratch_shapes=[pltpu.VMEM((N_EXPERTS, 128), f32)],
        compiler_params=_cparams("arbitrary"),
        name="moe_router",
    )(x_all, g, mods, mods, rw_t, rb_col, tri)


def _dispatch_kernel(pos_ref, zpos_ref, x_ref, g_ref, sh_ref, sc_ref, xs_ref, buf_ref, zero_ref, sem, zsem):
    i = pl.program_id(0)
    tm = x_ref.shape[0]

    @pl.when(i == 0)
    def _():
        zero_ref[...] = jnp.zeros_like(zero_ref)

        def zero_tile(row):
            cp = pltpu.make_async_copy(zero_ref, xs_ref.at[pl.ds(pl.multiple_of(row, 8), tm)], zsem)
            cp.start()
            cp.wait()

        for e in range(N_EXPERTS):
            zero_tile((zpos_ref[0, e] // 8) * 8)

        def tail(t, c):
            zero_tile(t * tm)
            return c

        lax.fori_loop(zpos_ref[0, N_EXPERTS], xs_ref.shape[0] // tm, tail, 0)

    slot = i % 2

    def wait_slot(s):
        for k in range(2):
            pltpu.make_async_copy(buf_ref.at[s], xs_ref.at[pl.ds(0, tm)], sem.at[s]).wait()

    @pl.when(i >= 2)
    def _():
        wait_slot(slot)

    buf_ref[slot] = _norm_mod(x_ref[...], g_ref[...], sh_ref[...], sc_ref[...])
    src = buf_ref.at[slot]
    for r in range(tm):
        for k in range(2):
            dst = pos_ref[k, i * tm + r]
            pltpu.make_async_copy(src.at[pl.ds(r, 1)], xs_ref.at[pl.ds(dst, 1)],
                                  sem.at[slot]).start(priority=k)

    @pl.when(i == pl.num_programs(0) - 1)
    def _():
        wait_slot(slot)
        wait_slot(1 - slot)


def _dispatch(pos, zpos, x_all, n_rows, g, mods, tiles_per_mod, p_rows):
    d = x_all.shape[1]
    tm = MOE_TILE
    tpm = tiles_per_mod * (ROW_TILE // tm)
    mod_spec = lambda k: pl.BlockSpec((None, 1, d), lambda i, pos, zpos: (i // tpm, 0, k))
    return pl.pallas_call(
        _dispatch_kernel,
        grid_spec=pltpu.PrefetchScalarGridSpec(
            num_scalar_prefetch=2,
            grid=(n_rows // tm,),
            in_specs=[pl.BlockSpec((tm, d), lambda i, pos, zpos: (i, 0)),
                      pl.BlockSpec((1, d), lambda i, pos, zpos: (0, 0)),
                      mod_spec(3), mod_spec(4)],
            out_specs=pl.BlockSpec(memory_space=pl.ANY),
            scratch_shapes=[pltpu.VMEM((2, tm, d), f32), pltpu.VMEM((tm, d), f32),
                            pltpu.SemaphoreType.DMA((2,)), pltpu.SemaphoreType.DMA(())],
        ),
        out_shape=jax.ShapeDtypeStruct((p_rows, d), f32),
        compiler_params=_cparams("arbitrary"),
        name="moe_dispatch",
    )(pos, zpos, x_all, g, mods, mods)


def _expert_kernel(sched_ref, nu_ref, xs_ref, wg_hbm, wu_hbm, wd_hbm, y_ref, wg_buf, wu_buf, wd_buf, sem,
                   *, layer):
    i = pl.program_id(0)
    slot = sched_ref[2, i]
    n_used = nu_ref[0, N_EXPERTS]

    def weight_copies(e, s):
        return [pltpu.make_async_copy(hbm.at[layer, e], buf.at[s], sem.at[s])
                for hbm, buf in ((wg_hbm, wg_buf), (wu_hbm, wu_buf), (wd_hbm, wd_buf))]

    @pl.when(i == 0)
    def _():
        for cp in weight_copies(sched_ref[0, 0], 0):
            cp.start()

    @pl.when((sched_ref[1, i] == 1) & (i < n_used))
    def _():
        for cp in weight_copies(sched_ref[0, i], slot):
            cp.wait()
        nxt = sched_ref[3, i]

        @pl.when(nxt >= 0)
        def _():
            for cp in weight_copies(nxt, 1 - slot):
                cp.start()

    @pl.when(i < n_used)
    def _():
        x = xs_ref[...].astype(bf16)
        a = _dot(x, wg_buf[slot])
        act = (a * jax.nn.sigmoid(a)) * _dot(x, wu_buf[slot])
        y_ref[...] = _dot(act.astype(bf16), wd_buf[slot])

    @pl.when(i >= n_used)
    def _():
        y_ref[...] = jnp.zeros_like(y_ref)


def _expert_ffn(sched, n_used, xs, layer, wg, wu, wd):
    p_rows, d = xs.shape
    de = wg.shape[3]
    tm = MOE_TILE
    return pl.pallas_call(
        functools.partial(_expert_kernel, layer=layer),
        grid_spec=pltpu.PrefetchScalarGridSpec(
            num_scalar_prefetch=2,
            grid=(p_rows // tm,),
            in_specs=[pl.BlockSpec((tm, d), lambda i, sc, nu: (jnp.where(i < nu[0, N_EXPERTS], i, 0), 0)),
                      pl.BlockSpec(memory_space=pl.ANY), pl.BlockSpec(memory_space=pl.ANY),
                      pl.BlockSpec(memory_space=pl.ANY)],
            out_specs=pl.BlockSpec((tm, d), lambda i, sc, nu: (i, 0)),
            scratch_shapes=[pltpu.VMEM((2, d, de), f32), pltpu.VMEM((2, d, de), f32),
                            pltpu.VMEM((2, de, d), f32), pltpu.SemaphoreType.DMA((2,))],
        ),
        out_shape=jax.ShapeDtypeStruct((p_rows, d), f32),
        compiler_params=_cparams("arbitrary", vmem_limit=EXPERT_VMEM_LIMIT),
        name="moe_experts",
    )(sched, n_used, xs, wg, wu, wd)


def _combine_kernel(pos_ref, x_ref, w_ref, gate_ref, fg_ref, y_ref, o_ref, ybuf_ref, sem, *, final_norm):
    i = pl.program_id(0)
    tm = x_ref.shape[0]
    slot = i % 2

    def gather(step, s):
        for r in range(tm):
            for k in range(2):
                src = pos_ref[k, step * tm + r]
                pltpu.make_async_copy(y_ref.at[pl.ds(src, 1)], ybuf_ref.at[s, k].at[pl.ds(r, 1)],
                                      sem.at[s]).start(priority=k)

    @pl.when(i == 0)
    def _():
        gather(0, 0)

    @pl.when(i + 1 < pl.num_programs(0))
    def _():
        gather(i + 1, 1 - slot)

    for k in range(2):
        pltpu.make_async_copy(y_ref.at[pl.ds(0, tm)], ybuf_ref.at[slot, k], sem.at[slot]).wait()
    w = w_ref[...]
    moe = w[:, 0:1] * ybuf_ref[slot, 0] + w[:, 1:2] * ybuf_ref[slot, 1]
    x = x_ref[...] + gate_ref[...] * moe
    if final_norm:
        x = x * lax.rsqrt(jnp.mean(x * x, axis=-1, keepdims=True) + EPS) * fg_ref[...]
    o_ref[...] = x


def _combine(pos, x_all, n_rows, w_tok, mods, final_g, y, tiles_per_mod, final_norm):
    d = x_all.shape[1]
    tm = MOE_TILE
    tpm = tiles_per_mod * (ROW_TILE // tm)
    kern = functools.partial(_combine_kernel, final_norm=final_norm)
    return pl.pallas_call(
        kern,
        grid_spec=pltpu.PrefetchScalarGridSpec(
            num_scalar_prefetch=1,
            grid=(n_rows // tm,),
            in_specs=[pl.BlockSpec((tm, d), lambda i, pos: (i, 0)),
                      pl.BlockSpec((tm, 2), lambda i, pos: (i, 0)),
                      pl.BlockSpec((None, 1, d), lambda i, pos: (i // tpm, 0, 5)),
                      pl.BlockSpec((1, d), lambda i, pos: (0, 0)),
                      pl.BlockSpec(memory_space=pl.ANY)],
            out_specs=pl.BlockSpec((tm, d), lambda i, pos: (i, 0)),
            scratch_shapes=[pltpu.VMEM((2, 2, tm, d), f32), pltpu.SemaphoreType.DMA((2,))],
        ),
        out_shape=jax.ShapeDtypeStruct((n_rows, d), f32),
        compiler_params=_cparams("arbitrary"),
        name="moe_combine",
    )(pos, x_all, w_tok, mods, final_g, y)


def _plan_kernel(info_ref, cnt_ref, pos_ref, sched_ref, misc_ref, *, tm):
    cnt = cnt_ref[...]
    padded = jnp.floor((cnt + (tm - 1)) * (1.0 / tm)) * tm
    starts, ends = [], []
    run = jnp.zeros((1, 128), f32)
    for e in range(N_EXPERTS):
        starts.append(run)
        run = run + padded[e:e + 1]
        ends.append(run)
    total = run
    e1, e2 = info_ref[0:1, :], info_ref[1:2, :]
    p1, p2 = info_ref[4:5, :], info_ref[5:6, :]
    for e in range(N_EXPERTS):
        s_e = starts[e][:, 0:1]
        p1 = p1 + jnp.where(e1 == e, s_e, 0.0)
        p2 = p2 + jnp.where(e2 == e, s_e, 0.0)
    pos_ref[...] = jnp.concatenate([p1, p2], axis=0).astype(jnp.int32)
    slot_e, next_e = [], [None] * N_EXPERTS
    ordinal = jnp.full((1, 128), -1.0, f32)
    for e in range(N_EXPERTS):
        ordinal = ordinal + jnp.where(cnt[e:e + 1] > 0, 1.0, 0.0)
        slot_e.append(ordinal - 2.0 * jnp.floor(ordinal * 0.5))
    nxt = jnp.full((1, 128), -1.0, f32)
    for e in reversed(range(N_EXPERTS)):
        next_e[e] = nxt
        nxt = jnp.where(cnt[e:e + 1] > 0, float(e), nxt)
    n_t = sched_ref.shape[1]
    t0 = lax.broadcasted_iota(jnp.int32, (1, n_t), 1).astype(f32) * tm
    te = jnp.zeros((1, n_t), f32)
    for e in range(N_EXPERTS):
        te = te + jnp.where(ends[e][:, 0:1] <= t0, 1.0, 0.0)
    te = jnp.minimum(te, N_EXPERTS - 1.0)
    first = jnp.zeros((1, n_t), f32)
    slot = jnp.zeros((1, n_t), f32)
    nxt_t = jnp.zeros((1, n_t), f32)
    for e in range(N_EXPERTS):
        here = te == e
        first = first + jnp.where(here & (t0 == starts[e][:, 0:1]), 1.0, 0.0)
        slot = slot + jnp.where(here, slot_e[e][:, 0:1], 0.0)
        nxt_t = nxt_t + jnp.where(here, next_e[e][:, 0:1], 0.0)
    first = jnp.where(t0 < total[:, 0:1], first, 0.0)
    sched_ref[...] = jnp.concatenate([te, first, slot, nxt_t, jnp.zeros((4, n_t), f32)],
                                     axis=0).astype(jnp.int32)
    lane = lax.broadcasted_iota(jnp.int32, (1, 128), 1)
    zrow = jnp.where(lane == N_EXPERTS, total * (1.0 / tm), 0.0)
    for e in range(N_EXPERTS):
        zrow = zrow + jnp.where(lane == e, starts[e] + cnt[e:e + 1], 0.0)
    misc_ref[...] = jnp.concatenate([zrow, jnp.zeros((7, 128), f32)], axis=0).astype(jnp.int32)


def _plan(info, counts, n_rows, n_tiles):
    n_t = ((n_tiles + 127) // 128) * 128
    full = lambda shape: pl.BlockSpec(shape, lambda: (0,) * len(shape))
    return pl.pallas_call(
        functools.partial(_plan_kernel, tm=MOE_TILE),
        in_specs=[full((8, n_rows)), full((N_EXPERTS, 128))],
        out_specs=[full((2, n_rows)), full((8, n_t)), full((8, 128))],
        out_shape=[jax.ShapeDtypeStruct((2, n_rows), jnp.int32),
                   jax.ShapeDtypeStruct((8, n_t), jnp.int32),
                   jax.ShapeDtypeStruct((8, 128), jnp.int32)],
        name="moe_plan",
    )(info, counts)


def _moe(x_all, n_rows, norm_g, mods, tiles_per_mod, rw_t, rb_col, layer, wg, wu, wd, final_g, final_norm):
    info, counts = _router(x_all, n_rows, norm_g, mods, rw_t, rb_col, tiles_per_mod)
    tm = MOE_TILE
    p_rows = 2 * n_rows + (N_EXPERTS + 1) * tm
    pos, sched, misc = _plan(info, counts, n_rows, p_rows // tm)
    xs = _dispatch(pos, misc, x_all, n_rows, norm_g, mods, tiles_per_mod, p_rows)
    y = _expert_ffn(sched, misc, xs, layer, wg, wu, wd)
    w_tok = jnp.transpose(info[2:4])
    return _combine(pos, x_all, n_rows, w_tok, mods, final_g, y, tiles_per_mod, final_norm)


def _rope_tables(seq_len, n_batch, n_ctx_rows):
    rows = seq_len // GRID_W
    row = jnp.repeat(jnp.arange(rows), GRID_W)
    col = jnp.tile(jnp.arange(GRID_W), rows)
    nf = HEAD_DIM // 4
    inv_freq = ROPE_BASE ** (-jnp.arange(nf, dtype=f32) / nf)
    ang_r = row[:, None].astype(f32) * inv_freq
    ang_c = col[:, None].astype(f32) * inv_freq
    cos64 = jnp.concatenate([jnp.cos(ang_r)] * 2 + [jnp.cos(ang_c)] * 2, axis=-1)
    sin64 = jnp.concatenate([-jnp.sin(ang_r), jnp.sin(ang_r), -jnp.sin(ang_c), jnp.sin(ang_c)], axis=-1)
    cos_t = jnp.tile(jnp.concatenate([cos64, cos64], axis=-1), (n_batch, 1))
    sin_t = jnp.tile(jnp.concatenate([sin64, sin64], axis=-1), (n_batch, 1))
    cos_t = jnp.concatenate([cos_t, jnp.ones((n_ctx_rows, 128), f32)], axis=0)
    sin_t = jnp.concatenate([sin_t, jnp.zeros((n_ctx_rows, 128), f32)], axis=0)
    return cos_t, sin_t


def kernel(x, c, ctx, c_ctx, w_mod, b_mod, norm1_g, norm2_g, final_norm_g, w_in, a_sink,
           p_w, p_scale, c_ln_g, c_ln_b, c_ws, c_bs, g_w2, g_b, g_norm_g,
           w_branch, w_gate, w_out, router_w, router_b, e_gate, e_up, e_down):
    n_batch, seq_len, d = x.shape
    n_ctx = ctx.shape[1]
    depth = w_mod.shape[0]
    n_lat = n_batch * seq_len
    n_all = n_lat + n_batch * n_ctx
    tiles_per_mod = seq_len // ROW_TILE

    cond = jnp.zeros((8, d), f32).at[0:n_batch].set(c).at[n_batch].set(c_ctx)
    mods_all = _modulation(cond, w_mod, b_mod)
    cos_t, sin_t = _rope_tables(seq_len, n_batch, n_batch * n_ctx)
    rw_f = jnp.transpose(router_w)
    rw_hi = rw_f.astype(bf16)
    rw_t = jnp.concatenate([rw_hi, (rw_f - rw_hi.astype(f32)).astype(bf16)], axis=0)
    rb_col = router_b.reshape(N_EXPERTS, 1)
    final_g = final_norm_g.reshape(1, d)

    w_in_b = jnp.pad(w_in, ((0, 0), (0, 0), (0, 96))).astype(bf16)
    w_gate_b = w_gate.astype(bf16)
    w_branch_b = w_branch.astype(bf16)
    w_out_b = w_out.astype(bf16)

    n_lat_tiles = n_lat // ROW_TILE
    xa, xb, b_tile0 = x.reshape(n_lat, d), ctx.reshape(n_batch * n_ctx, d), 0
    for l in range(depth):
        last = l == depth - 1
        n_out = n_lat if last else n_all
        mods = mods_all[l].reshape(8, 1, 6 * d)
        w2p = jnp.zeros((128, 512), f32)
        w2p = w2p.at[0:GLA_RANK, 0:256].set(g_w2[l, 0]).at[GLA_RANK:2 * GLA_RANK, 256:512].set(g_w2[l, 1])
        w2_hi = w2p.astype(bf16)
        w2_split = jnp.concatenate([w2_hi, w2_hi, (w2p - w2_hi.astype(f32)).astype(bf16)], axis=0)
        bs_b = jnp.broadcast_to(c_bs[l][:, :, None], (4, SG_CHUNK, 128))
        h, qkv, z_pool, s_br, z_gla, gates = _in_proj(
            xa, xb, n_lat_tiles, b_tile0, n_all, l, norm1_g[l].reshape(1, d), mods, cos_t, sin_t,
            w_in_b, w2_split, g_b[l].reshape(1, 512), c_ln_g[l].reshape(1, 512),
            c_ln_b[l].reshape(1, 512), c_ws[l].astype(bf16), bs_b, tiles_per_mod)
        a_br = _window_attention(qkv, a_sink[l], n_batch, seq_len, n_ctx)
        pw = p_w[l].astype(bf16)
        ps = p_scale[l].reshape(1, 512)
        p_br = _multi_scale_pool(z_pool, 0, n_lat, seq_len, ROW_TILE, pw, ps)
        o_f, o_b = _gla_scan(z_gla, gates, n_batch, seq_len, n_ctx)
        if last:
            a_c, p_c = a_br, p_br
        else:
            a_c = _context_attention(qkv, a_sink[l], n_batch, n_lat, n_ctx)
            p_c = _multi_scale_pool(z_pool, n_lat, n_batch * n_ctx, n_ctx, n_ctx, pw, ps)
        acc = _gated_merge(h, a_br, a_c, p_br, p_c, s_br, o_f, o_b, z_gla, g_norm_g[l].reshape(1, 512),
                           n_out, n_lat_tiles, l, w_gate_b, w_branch_b)
        x_all = _out_proj_residual(acc, l, w_out_b, xa, xb, n_lat_tiles, b_tile0, mods, n_out,
                                   tiles_per_mod)
        x_all = _moe(x_all, n_out, norm2_g[l].reshape(1, d), mods, tiles_per_mod, rw_t, rb_col,
                     l, e_gate, e_up, e_down, final_g, last)
        xa, xb, b_tile0 = x_all, x_all, n_lat_tiles
    return x_all.reshape(n_batch, seq_len, d)
```
